```python
import math
import jax, jax.numpy as jnp
from jax import lax
import numpy as np

D_MODEL = 2048
BATCH = 1
SEQ = 8192
DEPTH = 1

HEAD_DIM = 128
N_Q_HEADS = 8
N_KV_HEADS = 2
Q_PER_KV = N_Q_HEADS // N_KV_HEADS
Q_BLOCK = 128
GRID_W = 64
ROPE_THETA = 10000.0
ROPE_PAIRS = HEAD_DIM // 4
GMLP_GROUPS = 4
GMLP_GROUP_DIM = 128
GMLP_WIDTH = GMLP_GROUPS * GMLP_GROUP_DIM
CHUNK = 128
MEM_TOKENS = 256
N_MEM_HEADS = 4
MEM_WIDTH = N_MEM_HEADS * HEAD_DIM
N_BRANCHES = 3
D_FF = 4 * D_MODEL
EPS = 1e-6

ATTN_Q_W = N_Q_HEADS * HEAD_DIM
ATTN_KV_W = N_KV_HEADS * HEAD_DIM
SPLITS = [ATTN_Q_W, ATTN_KV_W, ATTN_KV_W, 2 * GMLP_WIDTH, MEM_WIDTH, N_BRANCHES * D_MODEL]
IN_W = sum(SPLITS)
SPLIT_IDX = list(np.cumsum(SPLITS)[:-1].tolist())

kernel_name = "hybrid_gated_attn_gmlp_memxattn_encoder"


def rms_norm(x, g):
    xf = x.astype(jnp.float32)
    y = xf * lax.rsqrt(jnp.mean(xf * xf, axis=-1, keepdims=True) + EPS)
    return (y * g.astype(jnp.float32)).astype(x.dtype)


def axial_rope_tables(S, dtype):
    rows = S // GRID_W
    t_row = jnp.repeat(jnp.arange(rows), GRID_W).astype(jnp.float32)
    t_col = jnp.tile(jnp.arange(GRID_W), rows).astype(jnp.float32)
    inv = ROPE_THETA ** (-jnp.arange(ROPE_PAIRS, dtype=jnp.float32) / ROPE_PAIRS)
    ar = t_row[:, None] * inv
    ac = t_col[:, None] * inv
    ang = jnp.concatenate([ar, ar, ac, ac], axis=-1)
    return jnp.cos(ang).astype(dtype), jnp.sin(ang).astype(dtype)


def apply_axial_rope(x, cos, sin):
    xs = x.reshape(*x.shape[:-1], 2, 2, ROPE_PAIRS)
    rot = jnp.stack([-xs[..., 1, :], xs[..., 0, :]], axis=-2).reshape(x.shape)
    return x * cos[None, :, None, :] + rot * sin[None, :, None, :]


def self_attention(q, k, v):
    B, S = q.shape[0], q.shape[1]
    nb = S // Q_BLOCK
    q = q * jnp.asarray(HEAD_DIM ** -0.5, q.dtype)
    qb = q.reshape(B, nb, Q_BLOCK, N_KV_HEADS, Q_PER_KV, HEAD_DIM).transpose(1, 0, 2, 3, 4, 5)

    def block(qblk):
        s = jnp.einsum('bqgrd,bkgd->bgrqk', qblk, k).astype(jnp.float32)
        p = jax.nn.softmax(s, axis=-1).astype(v.dtype)
        return jnp.einsum('bgrqk,bkgd->bqgrd', p, v)

    o = lax.map(block, qb)
    return o.transpose(1, 0, 2, 3, 4, 5).reshape(B, S, N_Q_HEADS * HEAD_DIM)


def chunked_gmlp(z, sgu_g, w_s, b_s):
    B, S = z.shape[0], z.shape[1]
    u, v = jnp.split(z, 2, axis=-1)
    v = rms_norm(v, sgu_g)
    vc = v.reshape(B, S // CHUNK, CHUNK, GMLP_GROUPS, GMLP_GROUP_DIM)
    mixed = jnp.einsum('gij,bcjgd->bcigd', w_s, vc) + b_s.T[None, None, :, :, None]
    return u * mixed.reshape(B, S, GMLP_WIDTH)


def memory_attention(qm, mem_n, w_mem_kv, mq_g, mk_g):
    B, S = qm.shape[0], qm.shape[1]
    M = mem_n.shape[1]
    km, vm = jnp.split(mem_n @ w_mem_kv, 2, axis=-1)
    qm = rms_norm(qm.reshape(B, S, N_MEM_HEADS, HEAD_DIM), mq_g) * jnp.asarray(HEAD_DIM ** -0.5, qm.dtype)
    km = rms_norm(km.reshape(B, M, N_MEM_HEADS, HEAD_DIM), mk_g)
    vm = vm.reshape(B, M, N_MEM_HEADS, HEAD_DIM)
    s = jnp.einsum('bshd,bmhd->bhsm', qm, km).astype(jnp.float32)
    p = jax.nn.softmax(s, axis=-1).astype(vm.dtype)
    return jnp.einsum('bhsm,bmhd->bshd', p, vm).reshape(B, S, MEM_WIDTH)


def setup_inputs(seed: int = 0) -> dict:
    key = jax.random.key(seed)
    ks = jax.random.split(key, 24)
    f32 = jnp.float32

    def w(k, shape, fan_in, scale=1.0):
        return jax.random.normal(k, shape, f32) * (scale * fan_in ** -0.5)

    def gain(k, shape):
        return 1.0 + 0.02 * jax.random.normal(k, shape, f32)

    L = DEPTH
    return {
        "x": jax.random.normal(ks[0], (BATCH, SEQ, D_MODEL), f32),
        "mem": jax.random.normal(ks[1], (BATCH, MEM_TOKENS, D_MODEL), f32),
        "norm_mix": gain(ks[2], (L, D_MODEL)),
        "w_in": w(ks[3], (L, D_MODEL, IN_W), D_MODEL),
        "q_norm": gain(ks[4], (L, HEAD_DIM)),
        "k_norm": gain(ks[5], (L, HEAD_DIM)),
        "sgu_norm": gain(ks[6], (L, GMLP_WIDTH)),
        "w_spatial": w(ks[7], (L, GMLP_GROUPS, CHUNK, CHUNK), CHUNK),
        "b_spatial": 1.0 + 0.02 * jax.random.normal(ks[8], (L, GMLP_GROUPS, CHUNK), f32),
        "mem_norm": gain(ks[9], (L, D_MODEL)),
        "w_mem_kv": w(ks[10], (L, D_MODEL, 2 * MEM_WIDTH), D_MODEL),
        "mq_norm": gain(ks[11], (L, HEAD_DIM)),
        "mk_norm": gain(ks[12], (L, HEAD_DIM)),
        "w_attn_o": w(ks[13], (L, ATTN_Q_W, D_MODEL), ATTN_Q_W),
        "w_gmlp_o": w(ks[14], (L, GMLP_WIDTH, D_MODEL), GMLP_WIDTH),
        "w_mem_o": w(ks[15], (L, MEM_WIDTH, D_MODEL), MEM_WIDTH),
        "w_out": w(ks[16], (L, D_MODEL, D_MODEL), D_MODEL, 0.5),
        "norm_ffn": gain(ks[17], (L, D_MODEL)),
        "w_ffn_up": w(ks[18], (L, D_MODEL, D_FF), D_MODEL),
        "w_ffn_down": w(ks[19], (L, D_FF, D_MODEL), D_FF, 0.5),
    }


def reference(x, mem, norm_mix, w_in, q_norm, k_norm, sgu_norm, w_spatial, b_spatial,
              mem_norm, w_mem_kv, mq_norm, mk_norm, w_attn_o, w_gmlp_o, w_mem_o, w_out,
              norm_ffn, w_ffn_up, w_ffn_down):
    B, S = x.shape[0], x.shape[1]
    cos, sin = axial_rope_tables(S, x.dtype)
    for l in range(DEPTH):
        h = rms_norm(x, norm_mix[l])
        proj = h @ w_in[l]
        q, k, v, zg, qm, gate_logits = jnp.split(proj, SPLIT_IDX, axis=-1)

        q = apply_axial_rope(rms_norm(q.reshape(B, S, N_Q_HEADS, HEAD_DIM), q_norm[l]), cos, sin)
        k = apply_axial_rope(rms_norm(k.reshape(B, S, N_KV_HEADS, HEAD_DIM), k_norm[l]), cos, sin)
        v = v.reshape(B, S, N_KV_HEADS, HEAD_DIM)
        y_attn = self_attention(q, k, v) @ w_attn_o[l]

        y_gmlp = chunked_gmlp(jax.nn.gelu(zg), sgu_norm[l], w_spatial[l], b_spatial[l]) @ w_gmlp_o[l]

        mem_n = rms_norm(mem, mem_norm[l])
        y_mem = memory_attention(qm, mem_n, w_mem_kv[l], mq_norm[l], mk_norm[l]) @ w_mem_o[l]

        g_a, g_g, g_m = jnp.split(jax.nn.sigmoid(gate_logits), N_BRANCHES, axis=-1)
        merged = g_a * y_attn + g_g * y_gmlp + g_m * y_mem
        x = x + merged @ w_out[l]

        h2 = rms_norm(x, norm_ffn[l])
        x = x + jnp.square(jax.nn.relu(h2 @ w_ffn_up[l])) @ w_ffn_down[l]
    return x
```

```python
import functools
import math

import jax
import jax.numpy as jnp
from jax import lax
from jax.experimental import pallas as pl
from jax.experimental.pallas import tpu as pltpu

D_MODEL = 2048
HEAD_DIM = 128
N_Q_HEADS = 8
N_KV_HEADS = 2
Q_PER_KV = N_Q_HEADS // N_KV_HEADS
GRID_W = 64
ROPE_THETA = 10000.0
ROPE_PAIRS = HEAD_DIM // 4
GMLP_GROUPS = 4
GMLP_WIDTH = GMLP_GROUPS * HEAD_DIM
CHUNK = 128
N_MEM_HEADS = 4
MEM_WIDTH = N_MEM_HEADS * HEAD_DIM
D_FF = 4 * D_MODEL
EPS = 1e-6
ATTN_Q_W = N_Q_HEADS * HEAD_DIM
ATTN_KV_W = N_KV_HEADS * HEAD_DIM

VMEM_LIMIT_BYTES = 56 * 1024 * 1024
LOG2E = math.log2(math.e)
Q_SCALE = HEAD_DIM ** -0.5 * LOG2E
KV_BLOCK = 256

F32 = jnp.float32
BF16 = jnp.bfloat16


def _params(*sem):
    return pltpu.CompilerParams(dimension_semantics=sem, vmem_limit_bytes=VMEM_LIMIT_BYTES)


def _dot(a, b):
    return jnp.dot(a, b, preferred_element_type=F32)


def _head_rms(x, gain):
    ms = jnp.mean(x * x, axis=-1, keepdims=True)
    return x * lax.rsqrt(ms + EPS) * gain


def _rope(y, cos, sin_lo, sin_hi):
    return (y * cos + pltpu.roll(y, HEAD_DIM - ROPE_PAIRS, 1) * sin_lo
            + pltpu.roll(y, ROPE_PAIRS, 1) * sin_hi)


def _rmsnorm_kernel(x_ref, g_ref, o_ref):
    o_ref[...] = _head_rms(x_ref[...], g_ref[...]).astype(o_ref.dtype)


def _rmsnorm(x, gain, tm):
    m, d = x.shape
    return pl.pallas_call(
        _rmsnorm_kernel,
        out_shape=jax.ShapeDtypeStruct((m, d), BF16),
        grid=(m // tm,),
        in_specs=[pl.BlockSpec((tm, d), lambda i: (i, 0)),
                  pl.BlockSpec((1, d), lambda i: (0, 0))],
        out_specs=pl.BlockSpec((tm, d), lambda i: (i, 0)),
        compiler_params=_params("parallel"),
        name="rmsnorm",
    )(x, gain.reshape(1, d))


def _qproj_kernel(h_ref, w_ref, g_ref, cos_ref, slo_ref, shi_ref, o_ref):
    acc = _dot(h_ref[...], w_ref[...])
    cos, slo, shi = cos_ref[...], slo_ref[...], shi_ref[...]
    g = g_ref[...]
    for hd in range(N_Q_HEADS):
        sl = slice(hd * HEAD_DIM, (hd + 1) * HEAD_DIM)
        y = _rope(_head_rms(acc[:, sl], g), cos, slo, shi)
        o_ref[:, sl] = (y * Q_SCALE).astype(o_ref.dtype)


def _qproj(h, w, gain, cos, slo, shi, tm):
    m, d = h.shape
    n = w.shape[1]
    row = lambda i: (i, 0)
    fixed = lambda i: (0, 0)
    return pl.pallas_call(
        _qproj_kernel,
        out_shape=jax.ShapeDtypeStruct((m, n), BF16),
        grid=(m // tm,),
        in_specs=[pl.BlockSpec((tm, d), row), pl.BlockSpec((d, n), fixed),
                  pl.BlockSpec((1, HEAD_DIM), fixed),
                  pl.BlockSpec((tm, HEAD_DIM), row), pl.BlockSpec((tm, HEAD_DIM), row),
                  pl.BlockSpec((tm, HEAD_DIM), row)],
        out_specs=pl.BlockSpec((tm, n), row),
        compiler_params=_params("parallel"),
        name="q_proj",
    )(h, w, gain.reshape(1, HEAD_DIM), cos, slo, shi)


def _kvproj_kernel(h_ref, w_ref, g_ref, cos_ref, slo_ref, shi_ref, kt_ref, v_ref, *, tm):
    acc = _dot(h_ref[...], w_ref[...])
    cos, slo, shi = cos_ref[...], slo_ref[...], shi_ref[...]
    g = g_ref[...]
    for hd in range(N_KV_HEADS):
        sl = slice(hd * HEAD_DIM, (hd + 1) * HEAD_DIM)
        y = _rope(_head_rms(acc[:, sl], g), cos, slo, shi)
        for c in range(tm // KV_BLOCK):
            blk = y[c * KV_BLOCK:(c + 1) * KV_BLOCK, :]
            kt_ref[c, sl, :] = jnp.transpose(blk).astype(kt_ref.dtype)
    v_ref[...] = acc[:, ATTN_KV_W:].astype(v_ref.dtype)


def _kvproj(h, w, gain, cos, slo, shi, tm):
    m, d = h.shape
    n = w.shape[1]
    row = lambda i: (i, 0)
    fixed = lambda i: (0, 0)
    return pl.pallas_call(
        functools.partial(_kvproj_kernel, tm=tm),
        out_shape=(jax.ShapeDtypeStruct((m // KV_BLOCK, ATTN_KV_W, KV_BLOCK), BF16),
                   jax.ShapeDtypeStruct((m, ATTN_KV_W), BF16)),
        grid=(m // tm,),
        in_specs=[pl.BlockSpec((tm, d), row), pl.BlockSpec((d, n), fixed),
                  pl.BlockSpec((1, HEAD_DIM), fixed),
                  pl.BlockSpec((tm, HEAD_DIM), row), pl.BlockSpec((tm, HEAD_DIM), row),
                  pl.BlockSpec((tm, HEAD_DIM), row)],
        out_specs=(pl.BlockSpec((tm // KV_BLOCK, ATTN_KV_W, KV_BLOCK), lambda i: (i, 0, 0)),
                   pl.BlockSpec((tm, ATTN_KV_W), row)),
        compiler_params=_params("parallel"),
        name="kv_proj",
    )(h, w, gain.reshape(1, HEAD_DIM), cos, slo, shi)


def _gmlp_kernel(h_ref, w_ref, g_ref, ws_ref, b_ref, o_ref, *, tm):
    z = jax.nn.gelu(_dot(h_ref[...], w_ref[...]))
    u = z[:, :GMLP_WIDTH]
    vn = _head_rms(z[:, GMLP_WIDTH:], g_ref[...]).astype(BF16)
    bias = b_ref[...]
    for c in range(tm // CHUNK):
        rows = slice(c * CHUNK, (c + 1) * CHUNK)
        for grp in range(GMLP_GROUPS):
            cols = slice(grp * HEAD_DIM, (grp + 1) * HEAD_DIM)
            mixed = _dot(ws_ref[grp], vn[rows, cols]) + bias[:, cols]
            o_ref[rows, cols] = (u[rows, cols] * mixed).astype(o_ref.dtype)


def _gmlp(h, w, sgu_gain, w_spatial, bias_full, tm):
    m, d = h.shape
    n = w.shape[1]
    row = lambda i: (i, 0)
    fixed = lambda i: (0, 0)
    return pl.pallas_call(
        functools.partial(_gmlp_kernel, tm=tm),
        out_shape=jax.ShapeDtypeStruct((m, GMLP_WIDTH), BF16),
        grid=(m // tm,),
        in_specs=[pl.BlockSpec((tm, d), row), pl.BlockSpec((d, n), fixed),
                  pl.BlockSpec((1, GMLP_WIDTH), fixed),
                  pl.BlockSpec((GMLP_GROUPS, CHUNK, CHUNK), lambda i: (0, 0, 0)),
                  pl.BlockSpec((CHUNK, GMLP_WIDTH), fixed)],
        out_specs=pl.BlockSpec((tm, GMLP_WIDTH), row),
        compiler_params=_params("parallel"),
        name="gmlp",
    )(h, w, sgu_gain.reshape(1, GMLP_WIDTH), w_spatial, bias_full)


def _memkv_kernel(mem_ref, g_ref, w_ref, kg_ref, kt_ref, v_ref):
    mem_n = _head_rms(mem_ref[...], g_ref[...]).astype(BF16)
    kv = _dot(mem_n, w_ref[...])
    kg = kg_ref[...]
    for hd in range(N_MEM_HEADS):
        sl = slice(hd * HEAD_DIM, (hd + 1) * HEAD_DIM)
        kt_ref[sl, :] = jnp.transpose(_head_rms(kv[:, sl], kg)).astype(kt_ref.dtype)
    v_ref[...] = kv[:, MEM_WIDTH:].astype(v_ref.dtype)


def _memkv(mem, mem_gain, w, mk_gain):
    mt, d = mem.shape
    return pl.pallas_call(
        _memkv_kernel,
        out_shape=(jax.ShapeDtypeStruct((MEM_WIDTH, mt), BF16),
                   jax.ShapeDtypeStruct((mt, MEM_WIDTH), BF16)),
        compiler_params=pltpu.CompilerParams(vmem_limit_bytes=VMEM_LIMIT_BYTES),
        name="mem_kv",
    )(mem, mem_gain.reshape(1, d), w, mk_gain.reshape(1, HEAD_DIM))


def _memattn_kernel(h_ref, w_ref, g_ref, kt_ref, v_ref, o_ref):
    acc = _dot(h_ref[...], w_ref[...])
    g = g_ref[...]
    for hd in range(N_MEM_HEADS):
        sl = slice(hd * HEAD_DIM, (hd + 1) * HEAD_DIM)
        qn = (_head_rms(acc[:, sl], g) * Q_SCALE).astype(BF16)
        s = _dot(qn, kt_ref[sl, :])
        p = jnp.exp2(s - jnp.max(s, axis=-1, keepdims=True))
        l = jnp.sum(p, axis=-1, keepdims=True)
        o = _dot(p.astype(BF16), v_ref[:, sl])
        o_ref[:, sl] = (o / l).astype(o_ref.dtype)


def _memattn(h, w, mq_gain, km_t, vm, tm):
    m, d = h.shape
    n = w.shape[1]
    mt = vm.shape[0]
    row = lambda i: (i, 0)
    fixed = lambda i: (0, 0)
    return pl.pallas_call(
        _memattn_kernel,
        out_shape=jax.ShapeDtypeStruct((m, MEM_WIDTH), BF16),
        grid=(m // tm,),
        in_specs=[pl.BlockSpec((tm, d), row), pl.BlockSpec((d, n), fixed),
                  pl.BlockSpec((1, HEAD_DIM), fixed),
                  pl.BlockSpec((MEM_WIDTH, mt), fixed), pl.BlockSpec((mt, MEM_WIDTH), fixed)],
        out_specs=pl.BlockSpec((tm, MEM_WIDTH), row),
        compiler_params=_params("parallel"),
        name="mem_attn",
    )(h, w, mq_gain.reshape(1, HEAD_DIM), km_t, vm)


def _gates_kernel(h_ref, w_ref, o_ref):
    z = _dot(h_ref[...], w_ref[...])
    o_ref[...] = (1.0 / (1.0 + jnp.exp(-z))).astype(o_ref.dtype)


def _gates(h, w, tm, tn):
    m, d = h.shape
    n = w.shape[1]
    return pl.pallas_call(
        _gates_kernel,
        out_shape=jax.ShapeDtypeStruct((m, n), BF16),
        grid=(m // tm, n // tn),
        in_specs=[pl.BlockSpec((tm, d), lambda i, j: (i, 0)),
                  pl.BlockSpec((d, tn), lambda i, j: (0, j))],
        out_specs=pl.BlockSpec((tm, tn), lambda i, j: (i, j)),
        compiler_params=_params("parallel", "parallel"),
        name="gates",
    )(h, w)


def _attn_kernel(q_ref, kt_ref, v_ref, o_ref, *, tq, nk):
    q = jnp.concatenate(
        [q_ref[:, r * HEAD_DIM:(r + 1) * HEAD_DIM] for r in range(Q_PER_KV)], axis=0)
    rows = Q_PER_KV * tq

    def body(j, carry):
        m, l, acc = carry
        off = pl.multiple_of(j * KV_BLOCK, KV_BLOCK)
        s = _dot(q, kt_ref[j])
        m_new = jnp.maximum(m, jnp.max(s, axis=-1, keepdims=True))
        alpha = jnp.exp2(m - m_new)
        p = jnp.exp2(s - m_new)
        l = alpha * l + jnp.sum(p, axis=-1, keepdims=True)
        acc = alpha * acc + _dot(p.astype(BF16), v_ref[pl.ds(off, KV_BLOCK), :])
        return m_new, l, acc

    m0 = jnp.full((rows, 1), -jnp.inf, F32)
    l0 = jnp.zeros((rows, 1), F32)
    acc0 = jnp.zeros((rows, HEAD_DIM), F32)
    _, l, acc = lax.fori_loop(0, nk, body, (m0, l0, acc0))
    out = acc / l
    for r in range(Q_PER_KV):
        o_ref[:, r * HEAD_DIM:(r + 1) * HEAD_DIM] = out[r * tq:(r + 1) * tq].astype(o_ref.dtype)


def _attention(q, k_t, v, tq):
    s = q.shape[0]
    nk = s // KV_BLOCK
    gw = Q_PER_KV * HEAD_DIM
    return pl.pallas_call(
        functools.partial(_attn_kernel, tq=tq, nk=nk),
        out_shape=jax.ShapeDtypeStruct((s, ATTN_Q_W), BF16),
        grid=(N_KV_HEADS, s // tq),
        in_specs=[pl.BlockSpec((tq, gw), lambda g, i: (i, g)),
                  pl.BlockSpec((nk, HEAD_DIM, KV_BLOCK), lambda g, i: (0, g, 0)),
                  pl.BlockSpec((s, HEAD_DIM), lambda g, i: (0, g))],
        out_specs=pl.BlockSpec((tq, gw), lambda g, i: (i, g)),
        compiler_params=_params("parallel", "parallel"),
        name="flash_attn",
    )(q, k_t, v)


def _merge_kernel(a_ref, gm_ref, ym_ref, ga_ref, gg_ref, gmm_ref, wa_ref, wg_ref, wm_ref, o_ref):
    y = ga_ref[...].astype(F32) * _dot(a_ref[...], wa_ref[...])
    y += gg_ref[...].astype(F32) * _dot(gm_ref[...], wg_ref[...])
    y += gmm_ref[...].astype(F32) * _dot(ym_ref[...], wm_ref[...])
    o_ref[...] = y.astype(o_ref.dtype)


def _merge(attn, gm, ym, gates, w_a, w_g, w_m, tm, tn):
    m = attn.shape[0]
    n = w_a.shape[1]
    nb = n // tn
    row = lambda i, j: (i, 0)
    col = lambda i, j: (0, j)
    return pl.pallas_call(
        _merge_kernel,
        out_shape=jax.ShapeDtypeStruct((m, n), BF16),
        grid=(m // tm, nb),
        in_specs=[pl.BlockSpec((tm, attn.shape[1]), row),
                  pl.BlockSpec((tm, gm.shape[1]), row),
                  pl.BlockSpec((tm, ym.shape[1]), row),
                  pl.BlockSpec((tm, tn), lambda i, j: (i, j)),
                  pl.BlockSpec((tm, tn), lambda i, j: (i, j + nb)),
                  pl.BlockSpec((tm, tn), lambda i, j: (i, j + 2 * nb)),
                  pl.BlockSpec((w_a.shape[0], tn), col),
                  pl.BlockSpec((w_g.shape[0], tn), col),
                  pl.BlockSpec((w_m.shape[0], tn), col)],
        out_specs=pl.BlockSpec((tm, tn), lambda i, j: (i, j)),
        compiler_params=_params("parallel", "parallel"),
        name="merge",
    )(attn, gm, ym, gates, gates, gates, w_a, w_g, w_m)


def _outproj_kernel(mg_ref, w_ref, x_ref, g_ref, x1_ref, h2_ref):
    x1 = x_ref[...] + _dot(mg_ref[...], w_ref[...])
    x1_ref[...] = x1
    h2_ref[...] = _head_rms(x1, g_ref[...]).astype(h2_ref.dtype)


def _outproj(merged, w, x, gain, tm):
    m, d = x.shape
    row = lambda i: (i, 0)
    fixed = lambda i: (0, 0)
    return pl.pallas_call(
        _outproj_kernel,
        out_shape=(jax.ShapeDtypeStruct((m, d), F32), jax.ShapeDtypeStruct((m, d), BF16)),
        grid=(m // tm,),
        in_specs=[pl.BlockSpec((tm, d), row), pl.BlockSpec((d, d), fixed),
                  pl.BlockSpec((tm, d), row), pl.BlockSpec((1, d), fixed)],
        out_specs=(pl.BlockSpec((tm, d), row), pl.BlockSpec((tm, d), row)),
        compiler_params=_params("parallel"),
        name="out_proj",
    )(merged, w, x, gain.reshape(1, d))


def _ffn_up_kernel(h_ref, w_ref, o_ref):
    z = jnp.maximum(_dot(h_ref[...], w_ref[...]), 0.0)
    o_ref[...] = (z * z).astype(o_ref.dtype)


def _ffn_up(h, w, tm, tn):
    m, d = h.shape
    n = w.shape[1]
    return pl.pallas_call(
        _ffn_up_kernel,
        out_shape=jax.ShapeDtypeStruct((m, n), BF16),
        grid=(m // tm, n // tn),
        in_specs=[pl.BlockSpec((tm, d), lambda i, j: (i, 0)),
                  pl.BlockSpec((d, tn), lambda i, j: (0, j))],
        out_specs=pl.BlockSpec((tm, tn), lambda i, j: (i, j)),
        compiler_params=_params("parallel", "parallel"),
        name="ffn_up",
    )(h, w)


def _ffn_down_kernel(a_ref, w_ref, x_ref, o_ref, acc_ref):
    k = pl.program_id(2)

    @pl.when(k == 0)
    def _():
        acc_ref[...] = x_ref[...]

    acc_ref[...] += _dot(a_ref[...], w_ref[...])

    @pl.when(k == pl.num_programs(2) - 1)
    def _():
        o_ref[...] = acc_ref[...]


def _ffn_down(a, w, x1, tm, tn, tk):
    m, kdim = a.shape
    n = w.shape[1]
    return pl.pallas_call(
        _ffn_down_kernel,
        out_shape=jax.ShapeDtypeStruct((m, n), F32),
        grid=(m // tm, n // tn, kdim // tk),
        in_specs=[pl.BlockSpec((tm, tk), lambda i, j, k: (i, k)),
                  pl.BlockSpec((tk, tn), lambda i, j, k: (k, j)),
                  pl.BlockSpec((tm, tn), lambda i, j, k: (i, j))],
        out_specs=pl.BlockSpec((tm, tn), lambda i, j, k: (i, j)),
        scratch_shapes=[pltpu.VMEM((tm, tn), F32)],
        compiler_params=_params("parallel", "parallel", "arbitrary"),
        name="ffn_down",
    )(a, w, x1)


def _rope_tables(s):
    rows = s // GRID_W
    t_row = jnp.repeat(jnp.arange(rows), GRID_W).astype(F32)
    t_col = jnp.tile(jnp.arange(GRID_W), rows).astype(F32)
    inv = ROPE_THETA ** (-jnp.arange(ROPE_PAIRS, dtype=F32) / ROPE_PAIRS)
    ar = t_row[:, None] * inv
    ac = t_col[:, None] * inv
    ang = jnp.concatenate([ar, ar, ac, ac], axis=-1)
    cos, sin = jnp.cos(ang), jnp.sin(ang)
    low_half = (jnp.arange(HEAD_DIM) % (2 * ROPE_PAIRS)) < ROPE_PAIRS
    return cos, jnp.where(low_half, -sin, 0.0), jnp.where(low_half, 0.0, sin)


def kernel(x, mem, norm_mix, w_in, q_norm, k_norm, sgu_norm, w_spatial, b_spatial, mem_norm,
           w_mem_kv, mq_norm, mk_norm, w_attn_o, w_gmlp_o, w_mem_o, w_out, norm_ffn,
           w_ffn_up, w_ffn_down):
    b, s, d = x.shape
    assert b == 1 and d == D_MODEL and norm_mix.shape[0] == 1
    xs = x[0]
    cos, slo, shi = _rope_tables(s)

    w_in_b = w_in[0].astype(BF16)
    o1 = ATTN_Q_W
    o2 = o1 + 2 * ATTN_KV_W
    o3 = o2 + 2 * GMLP_WIDTH
    o4 = o3 + MEM_WIDTH
    w_q, w_kv, w_zg, w_qm, w_gate = (w_in_b[:, :o1], w_in_b[:, o1:o2], w_in_b[:, o2:o3],
                                     w_in_b[:, o3:o4], w_in_b[:, o4:])
    bias_full = jnp.repeat(b_spatial[0].T, HEAD_DIM, axis=1)

    h = _rmsnorm(xs, norm_mix[0], 512)
    q = _qproj(h, w_q, q_norm[0], cos, slo, shi, 512)
    k_t, v = _kvproj(h, w_kv, k_norm[0], cos, slo, shi, 512)
    gm = _gmlp(h, w_zg, sgu_norm[0], w_spatial[0].astype(BF16), bias_full, 512)
    km_t, vm = _memkv(mem[0], mem_norm[0], w_mem_kv[0].astype(BF16), mk_norm[0])
    ym = _memattn(h, w_qm, mq_norm[0], km_t, vm, 512)
    gates = _gates(h, w_gate, 1024, 1024)
    attn = _attention(q, k_t, v, 128)
    merged = _merge(attn, gm, ym, gates, w_attn_o[0].astype(BF16), w_gmlp_o[0].astype(BF16),
                    w_mem_o[0].astype(BF16), 512, 1024)
    x1, h2 = _outproj(merged, w_out[0].astype(BF16), xs, norm_ffn[0], 512)
    a = _ffn_up(h2, w_ffn_up[0].astype(BF16), 1024, 1024)
    out = _ffn_down(a, w_ffn_down[0].astype(BF16), x1, 1024, 1024, 2048)
    return out[None]
```

```python
import functools
import math

import jax
import jax.numpy as jnp
from jax import lax
from jax.experimental import pallas as pl
from jax.experimental.pallas import tpu as pltpu

D_MODEL = 2048
HEAD_DIM = 128
N_Q_HEADS = 8
N_KV_HEADS = 2
Q_PER_KV = N_Q_HEADS // N_KV_HEADS
GRID_W = 64
ROPE_THETA = 10000.0
ROPE_PAIRS = HEAD_DIM // 4
GMLP_GROUPS = 4
GMLP_WIDTH = GMLP_GROUPS * HEAD_DIM
CHUNK = 128
N_MEM_HEADS = 4
MEM_WIDTH = N_MEM_HEADS * HEAD_DIM
D_FF = 4 * D_MODEL
EPS = 1e-6
ATTN_Q_W = N_Q_HEADS * HEAD_DIM
ATTN_KV_W = N_KV_HEADS * HEAD_DIM

VMEM_LIMIT_BYTES = 56 * 1024 * 1024
LOG2E = math.log2(math.e)
Q_SCALE = HEAD_DIM ** -0.5 * LOG2E
KV_BLOCK = 256
ATTN_TQ = 128
MAX_SAFE_SHIFT = 50.0

F32 = jnp.float32
BF16 = jnp.bfloat16


def _params(*sem):
    return pltpu.CompilerParams(dimension_semantics=sem, vmem_limit_bytes=VMEM_LIMIT_BYTES)


def _dot(a, b):
    return jnp.dot(a, b, preferred_element_type=F32)


def _head_rms(x, gain):
    ms = jnp.mean(x * x, axis=-1, keepdims=True)
    return x * lax.rsqrt(ms + EPS) * gain


def _rope(y, cos, sin_lo, sin_hi):
    return (y * cos + pltpu.roll(y, HEAD_DIM - ROPE_PAIRS, 1) * sin_lo
            + pltpu.roll(y, ROPE_PAIRS, 1) * sin_hi)


def _rmsnorm_kernel(x_ref, g_ref, o_ref):
    o_ref[...] = _head_rms(x_ref[...], g_ref[...]).astype(o_ref.dtype)


def _rmsnorm(x, gain, tm):
    m, d = x.shape
    return pl.pallas_call(
        _rmsnorm_kernel,
        out_shape=jax.ShapeDtypeStruct((m, d), BF16),
        grid=(m // tm,),
        in_specs=[pl.BlockSpec((tm, d), lambda i: (i, 0)),
                  pl.BlockSpec((1, d), lambda i: (0, 0))],
        out_specs=pl.BlockSpec((tm, d), lambda i: (i, 0)),
        compiler_params=_params("parallel"),
        name="rmsnorm",
    )(x, gain.reshape(1, d))


def _qproj_kernel(h_ref, w_ref, g_ref, cos_ref, slo_ref, shi_ref, kmax_ref, o_ref, bmax_ref,
                  *, tm, tq):
    acc = _dot(h_ref[...], w_ref[...])
    cos, slo, shi = cos_ref[...], slo_ref[...], shi_ref[...]
    g = g_ref[...]
    kmax = kmax_ref[...]
    first_lane = lax.broadcasted_iota(jnp.int32, (tm, HEAD_DIM), 1) == 0
    bmax = jnp.zeros((1, 1), F32)
    for hd in range(N_Q_HEADS):
        grp, r = divmod(hd, Q_PER_KV)
        sl = slice(hd * HEAD_DIM, (hd + 1) * HEAD_DIM)
        y = _rope(_head_rms(acc[:, sl], g), cos, slo, shi) * Q_SCALE
        bound = jnp.sqrt(jnp.sum(y * y, axis=-1, keepdims=True)) * kmax
        bmax = jnp.maximum(bmax, jnp.max(bound, axis=0, keepdims=True))
        yb = y.astype(o_ref.dtype)
        aug = jnp.where(first_lane, bound, 0.0).astype(o_ref.dtype)
        for t in range(tm // tq):
            o_ref[t, grp, r * tq:(r + 1) * tq, :HEAD_DIM] = yb[t * tq:(t + 1) * tq]
            o_ref[t, grp, r * tq:(r + 1) * tq, HEAD_DIM:] = aug[t * tq:(t + 1) * tq]
    bmax_ref[...] = jnp.broadcast_to(bmax.reshape(1, 1, 1), bmax_ref.shape)


def _qproj(h, w, gain, cos, slo, shi, kmax, tm, tq):
    m, d = h.shape
    n = w.shape[1]
    row = lambda i: (i, 0)
    fixed = lambda i: (0, 0)
    return pl.pallas_call(
        functools.partial(_qproj_kernel, tm=tm, tq=tq),
        out_shape=(jax.ShapeDtypeStruct((m // tq, N_KV_HEADS, Q_PER_KV * tq, 2 * HEAD_DIM), BF16),
                   jax.ShapeDtypeStruct((m // tm, 8, HEAD_DIM), F32)),
        grid=(m // tm,),
        in_specs=[pl.BlockSpec((tm, d), row), pl.BlockSpec((d, n), fixed),
                  pl.BlockSpec((1, HEAD_DIM), fixed),
                  pl.BlockSpec((tm, HEAD_DIM), row), pl.BlockSpec((tm, HEAD_DIM), row),
                  pl.BlockSpec((tm, HEAD_DIM), row), pl.BlockSpec((1, 1), fixed)],
        out_specs=(pl.BlockSpec((tm // tq, N_KV_HEADS, Q_PER_KV * tq, 2 * HEAD_DIM),
                                lambda i: (i, 0, 0, 0)),
                   pl.BlockSpec((1, 8, HEAD_DIM), lambda i: (i, 0, 0))),
        compiler_params=_params("parallel"),
        name="q_proj",
    )(h, w, gain.reshape(1, HEAD_DIM), cos, slo, shi, kmax)


def _kvproj_kernel(h_ref, w_ref, g_ref, cos_ref, slo_ref, shi_ref, kt_ref, v_ref, k2max_ref, *, tm):
    acc = _dot(h_ref[...], w_ref[...])
    cos, slo, shi = cos_ref[...], slo_ref[...], shi_ref[...]
    g = g_ref[...]
    first_row = lax.broadcasted_iota(jnp.int32, (HEAD_DIM, KV_BLOCK), 0) == 0
    aug = jnp.where(first_row, -1.0, 0.0).astype(kt_ref.dtype)
    k2max = jnp.zeros((1, 1), F32)
    for hd in range(N_KV_HEADS):
        sl = slice(hd * HEAD_DIM, (hd + 1) * HEAD_DIM)
        y = _rope(_head_rms(acc[:, sl], g), cos, slo, shi)
        k2 = jnp.sum(y * y, axis=-1, keepdims=True)
        k2max = jnp.maximum(k2max, jnp.max(k2, axis=0, keepdims=True))
        base = hd * 2 * HEAD_DIM
        for c in range(tm // KV_BLOCK):
            blk = y[c * KV_BLOCK:(c + 1) * KV_BLOCK, :]
            kt_ref[c, base:base + HEAD_DIM, :] = jnp.transpose(blk).astype(kt_ref.dtype)
            kt_ref[c, base + HEAD_DIM:base + 2 * HEAD_DIM, :] = aug
    v_ref[...] = acc[:, ATTN_KV_W:].astype(v_ref.dtype)
    k2max_ref[...] = jnp.broadcast_to(k2max.reshape(1, 1, 1), k2max_ref.shape)


def _kvproj(h, w, gain, cos, slo, shi, tm):
    m, d = h.shape
    n = w.shape[1]
    row = lambda i: (i, 0)
    fixed = lambda i: (0, 0)
    return pl.pallas_call(
        functools.partial(_kvproj_kernel, tm=tm),
        out_shape=(jax.ShapeDtypeStruct((m // KV_BLOCK, 2 * ATTN_KV_W, KV_BLOCK), BF16),
                   jax.ShapeDtypeStruct((m, ATTN_KV_W), BF16),
                   jax.ShapeDtypeStruct((m // tm, 8, HEAD_DIM), F32)),
        grid=(m // tm,),
        in_specs=[pl.BlockSpec((tm, d), row), pl.BlockSpec((d, n), fixed),
                  pl.BlockSpec((1, HEAD_DIM), fixed),
                  pl.BlockSpec((tm, HEAD_DIM), row), pl.BlockSpec((tm, HEAD_DIM), row),
                  pl.BlockSpec((tm, HEAD_DIM), row)],
        out_specs=(pl.BlockSpec((tm // KV_BLOCK, 2 * ATTN_KV_W, KV_BLOCK), lambda i: (i, 0, 0)),
                   pl.BlockSpec((tm, ATTN_KV_W), row),
                   pl.BlockSpec((1, 8, HEAD_DIM), lambda i: (i, 0, 0))),
        compiler_params=_params("parallel"),
        name="kv_proj",
    )(h, w, gain.reshape(1, HEAD_DIM), cos, slo, shi)


def _gmlp_kernel(h_ref, w_ref, g_ref, ws_ref, b_ref, o_ref, *, tm):
    z = jax.nn.gelu(_dot(h_ref[...], w_ref[...]))
    u = z[:, :GMLP_WIDTH]
    vn = _head_rms(z[:, GMLP_WIDTH:], g_ref[...]).astype(BF16)
    bias = b_ref[...]
    for c in range(tm // CHUNK):
        rows = slice(c * CHUNK, (c + 1) * CHUNK)
        for grp in range(GMLP_GROUPS):
            cols = slice(grp * HEAD_DIM, (grp + 1) * HEAD_DIM)
            mixed = _dot(ws_ref[grp], vn[rows, cols]) + bias[:, cols]
            o_ref[rows, cols] = (u[rows, cols] * mixed).astype(o_ref.dtype)


def _gmlp(h, w, sgu_gain, w_spatial, bias_full, tm):
    m, d = h.shape
    n = w.shape[1]
    row = lambda i: (i, 0)
    fixed = lambda i: (0, 0)
    return pl.pallas_call(
        functools.partial(_gmlp_kernel, tm=tm),
        out_shape=jax.ShapeDtypeStruct((m, GMLP_WIDTH), BF16),
        grid=(m // tm,),
        in_specs=[pl.BlockSpec((tm, d), row), pl.BlockSpec((d, n), fixed),
                  pl.BlockSpec((1, GMLP_WIDTH), fixed),
                  pl.BlockSpec((GMLP_GROUPS, CHUNK, CHUNK), lambda i: (0, 0, 0)),
                  pl.BlockSpec((CHUNK, GMLP_WIDTH), fixed)],
        out_specs=pl.BlockSpec((tm, GMLP_WIDTH), row),
        compiler_params=_params("parallel"),
        name="gmlp",
    )(h, w, sgu_gain.reshape(1, GMLP_WIDTH), w_spatial, bias_full)


def _memkv_kernel(mem_ref, g_ref, w_ref, kg_ref, kt_ref, v_ref):
    mem_n = _head_rms(mem_ref[...], g_ref[...]).astype(BF16)
    kv = _dot(mem_n, w_ref[...])
    kg = kg_ref[...]
    for hd in range(N_MEM_HEADS):
        sl = slice(hd * HEAD_DIM, (hd + 1) * HEAD_DIM)
        kt_ref[sl, :] = jnp.transpose(_head_rms(kv[:, sl], kg)).astype(kt_ref.dtype)
    v_ref[...] = kv[:, MEM_WIDTH:].astype(v_ref.dtype)


def _memkv(mem, mem_gain, w, mk_gain):
    mt, d = mem.shape
    return pl.pallas_call(
        _memkv_kernel,
        out_shape=(jax.ShapeDtypeStruct((MEM_WIDTH, mt), BF16),
                   jax.ShapeDtypeStruct((mt, MEM_WIDTH), BF16)),
        compiler_params=pltpu.CompilerParams(vmem_limit_bytes=VMEM_LIMIT_BYTES),
        name="mem_kv",
    )(mem, mem_gain.reshape(1, d), w, mk_gain.reshape(1, HEAD_DIM))


def _memattn_kernel(h_ref, w_ref, g_ref, kt_ref, v_ref, o_ref):
    acc = _dot(h_ref[...], w_ref[...])
    g = g_ref[...]
    for hd in range(N_MEM_HEADS):
        sl = slice(hd * HEAD_DIM, (hd + 1) * HEAD_DIM)
        qn = (_head_rms(acc[:, sl], g) * Q_SCALE).astype(BF16)
        s = _dot(qn, kt_ref[sl, :])
        p = jnp.exp2(s - jnp.max(s, axis=-1, keepdims=True))
        l = jnp.sum(p, axis=-1, keepdims=True)
        o = _dot(p.astype(BF16), v_ref[:, sl])
        o_ref[:, sl] = (o / l).astype(o_ref.dtype)


def _memattn(h, w, mq_gain, km_t, vm, tm):
    m, d = h.shape
    n = w.shape[1]
    mt = vm.shape[0]
    row = lambda i: (i, 0)
    fixed = lambda i: (0, 0)
    return pl.pallas_call(
        _memattn_kernel,
        out_shape=jax.ShapeDtypeStruct((m, MEM_WIDTH), BF16),
        grid=(m // tm,),
        in_specs=[pl.BlockSpec((tm, d), row), pl.BlockSpec((d, n), fixed),
                  pl.BlockSpec((1, HEAD_DIM), fixed),
                  pl.BlockSpec((MEM_WIDTH, mt), fixed), pl.BlockSpec((mt, MEM_WIDTH), fixed)],
        out_specs=pl.BlockSpec((tm, MEM_WIDTH), row),
        compiler_params=_params("parallel"),
        name="mem_attn",
    )(h, w, mq_gain.reshape(1, HEAD_DIM), km_t, vm)


def _gates_kernel(h_ref, w_ref, o_ref):
    z = _dot(h_ref[...], w_ref[...])
    o_ref[...] = (1.0 / (1.0 + jnp.exp(-z))).astype(o_ref.dtype)


def _gates(h, w, tm, tn):
    m, d = h.shape
    n = w.shape[1]
    return pl.pallas_call(
        _gates_kernel,
        out_shape=jax.ShapeDtypeStruct((m, n), BF16),
        grid=(m // tm, n // tn),
        in_specs=[pl.BlockSpec((tm, d), lambda i, j: (i, 0)),
                  pl.BlockSpec((d, tn), lambda i, j: (0, j))],
        out_specs=pl.BlockSpec((tm, tn), lambda i, j: (i, j)),
        compiler_params=_params("parallel", "parallel"),
        name="gates",
    )(h, w)


def _store_heads(o_ref, out, tq):
    for r in range(Q_PER_KV):
        o_ref[:, r * HEAD_DIM:(r + 1) * HEAD_DIM] = out[r * tq:(r + 1) * tq].astype(o_ref.dtype)


def _attn_bounded_kernel(q_ref, kt_ref, v_ref, o_ref, l_ref, acc_ref, *, tq, nk):
    l_ref[...] = jnp.zeros_like(l_ref)
    acc_ref[...] = jnp.zeros_like(acc_ref)

    def body(j, carry):
        off = pl.multiple_of(j * KV_BLOCK, KV_BLOCK)
        p = jnp.exp2(_dot(q_ref[0, 0], kt_ref[j]))
        l_ref[...] += p[:, :HEAD_DIM] + p[:, HEAD_DIM:]
        acc_ref[...] += _dot(p.astype(BF16), v_ref[pl.ds(off, KV_BLOCK), :])
        return carry

    lax.fori_loop(0, nk, body, 0, unroll=8)
    l = jnp.sum(l_ref[...], axis=-1, keepdims=True)
    _store_heads(o_ref, acc_ref[...] / l, tq)


def _attn_online_kernel(q_ref, kt_ref, v_ref, o_ref, *, tq, nk):
    q = q_ref[0, 0]
    rows = Q_PER_KV * tq

    def body(j, carry):
        m, l, acc = carry
        off = pl.multiple_of(j * KV_BLOCK, KV_BLOCK)
        s = _dot(q, kt_ref[j])
        m_new = jnp.maximum(m, jnp.max(s, axis=-1, keepdims=True))
        alpha = jnp.exp2(m - m_new)
        p = jnp.exp2(s - m_new)
        l = alpha * l + jnp.sum(p, axis=-1, keepdims=True)
        acc = alpha * acc + _dot(p.astype(BF16), v_ref[pl.ds(off, KV_BLOCK), :])
        return m_new, l, acc

    m0 = jnp.full((rows, 1), -jnp.inf, F32)
    l0 = jnp.zeros((rows, 1), F32)
    acc0 = jnp.zeros((rows, HEAD_DIM), F32)
    _, l, acc = lax.fori_loop(0, nk, body, (m0, l0, acc0))
    _store_heads(o_ref, acc / l, tq)


def _attention(q_aug, kt_aug, v, bounded):
    nq, _, rows, qw = q_aug.shape
    tq = rows // Q_PER_KV
    s = v.shape[0]
    nk = s // KV_BLOCK
    gw = Q_PER_KV * HEAD_DIM
    if bounded:
        body = functools.partial(_attn_bounded_kernel, tq=tq, nk=nk)
        width, kblk = qw, (lambda g, i: (0, g, 0))
        scratch = [pltpu.VMEM((rows, HEAD_DIM), F32), pltpu.VMEM((rows, HEAD_DIM), F32)]
        name = "attn_bounded"
    else:
        body = functools.partial(_attn_online_kernel, tq=tq, nk=nk)
        width, kblk = HEAD_DIM, (lambda g, i: (0, 2 * g, 0))
        scratch = []
        name = "attn_online"
    return pl.pallas_call(
        body,
        out_shape=jax.ShapeDtypeStruct((s, ATTN_Q_W), BF16),
        grid=(N_KV_HEADS, nq),
        in_specs=[pl.BlockSpec((1, 1, rows, width), lambda g, i: (i, g, 0, 0)),
                  pl.BlockSpec((nk, width, KV_BLOCK), kblk),
                  pl.BlockSpec((s, HEAD_DIM), lambda g, i: (0, g))],
        out_specs=pl.BlockSpec((tq, gw), lambda g, i: (i, g)),
        scratch_shapes=scratch,
        compiler_params=_params("parallel", "parallel"),
        name=name,
    )(q_aug, kt_aug, v)


def _merge_kernel(a_ref, gm_ref, ym_ref, ga_ref, gg_ref, gmm_ref, wa_ref, wg_ref, wm_ref, o_ref):
    y = ga_ref[...].astype(F32) * _dot(a_ref[...], wa_ref[...])
    y += gg_ref[...].astype(F32) * _dot(gm_ref[...], wg_ref[...])
    y += gmm_ref[...].astype(F32) * _dot(ym_ref[...], wm_ref[...])
    o_ref[...] = y.astype(o_ref.dtype)


def _merge(attn, gm, ym, gates, w_a, w_g, w_m, tm, tn):
    m = attn.shape[0]
    n = w_a.shape[1]
    nb = n // tn
    row = lambda i, j: (i, 0)
    col = lambda i, j: (0, j)
    return pl.pallas_call(
        _merge_kernel,
        out_shape=jax.ShapeDtypeStruct((m, n), BF16),
        grid=(m // tm, nb),
        in_specs=[pl.BlockSpec((tm, attn.shape[1]), row),
                  pl.BlockSpec((tm, gm.shape[1]), row),
                  pl.BlockSpec((tm, ym.shape[1]), row),
                  pl.BlockSpec((tm, tn), lambda i, j: (i, j)),
                  pl.BlockSpec((tm, tn), lambda i, j: (i, j + nb)),
                  pl.BlockSpec((tm, tn), lambda i, j: (i, j + 2 * nb)),
                  pl.BlockSpec((w_a.shape[0], tn), col),
                  pl.BlockSpec((w_g.shape[0], tn), col),
                  pl.BlockSpec((w_m.shape[0], tn), col)],
        out_specs=pl.BlockSpec((tm, tn), lambda i, j: (i, j)),
        compiler_params=_params("parallel", "parallel"),
        name="merge",
    )(attn, gm, ym, gates, gates, gates, w_a, w_g, w_m)


def _outproj_kernel(mg_ref, w_ref, x_ref, g_ref, x1_ref, h2_ref):
    x1 = x_ref[...] + _dot(mg_ref[...], w_ref[...])
    x1_ref[...] = x1
    h2_ref[...] = _head_rms(x1, g_ref[...]).astype(h2_ref.dtype)


def _outproj(merged, w, x, gain, tm):
    m, d = x.shape
    row = lambda i: (i, 0)
    fixed = lambda i: (0, 0)
    return pl.pallas_call(
        _outproj_kernel,
        out_shape=(jax.ShapeDtypeStruct((m, d), F32), jax.ShapeDtypeStruct((m, d), BF16)),
        grid=(m // tm,),
        in_specs=[pl.BlockSpec((tm, d), row), pl.BlockSpec((d, d), fixed),
                  pl.BlockSpec((tm, d), row), pl.BlockSpec((1, d), fixed)],
        out_specs=(pl.BlockSpec((tm, d), row), pl.BlockSpec((tm, d), row)),
        compiler_params=_params("parallel"),
        name="out_proj",
    )(merged, w, x, gain.reshape(1, d))


def _ffn_up_kernel(h_ref, w_ref, o_ref):
    z = jnp.maximum(_dot(h_ref[...], w_ref[...]), 0.0)
    o_ref[...] = (z * z).astype(o_ref.dtype)


def _ffn_up(h, w, tm, tn):
    m, d = h.shape
    n = w.shape[1]
    return pl.pallas_call(
        _ffn_up_kernel,
        out_shape=jax.ShapeDtypeStruct((m, n), BF16),
        grid=(m // tm, n // tn),
        in_specs=[pl.BlockSpec((tm, d), lambda i, j: (i, 0)),
                  pl.BlockSpec((d, tn), lambda i, j: (0, j))],
        out_specs=pl.BlockSpec((tm, tn), lambda i, j: (i, j)),
        compiler_params=_params("parallel", "parallel"),
        name="ffn_up",
    )(h, w)


def _ffn_down_kernel(a_ref, w_ref, x_ref, o_ref, acc_ref):
    k = pl.program_id(2)

    @pl.when(k == 0)
    def _():
        acc_ref[...] = x_ref[...]

    acc_ref[...] += _dot(a_ref[...], w_ref[...])

    @pl.when(k == pl.num_programs(2) - 1)
    def _():
        o_ref[...] = acc_ref[...]


def _ffn_down(a, w, x1, tm, tn, tk):
    m, kdim = a.shape
    n = w.shape[1]
    return pl.pallas_call(
        _ffn_down_kernel,
        out_shape=jax.ShapeDtypeStruct((m, n), F32),
        grid=(m // tm, n // tn, kdim // tk),
        in_specs=[pl.BlockSpec((tm, tk), lambda i, j, k: (i, k)),
                  pl.BlockSpec((tk, tn), lambda i, j, k: (k, j)),
                  pl.BlockSpec((tm, tn), lambda i, j, k: (i, j))],
        out_specs=pl.BlockSpec((tm, tn), lambda i, j, k: (i, j)),
        scratch_shapes=[pltpu.VMEM((tm, tn), F32)],
        compiler_params=_params("parallel", "parallel", "arbitrary"),
        name="ffn_down",
    )(a, w, x1)


def _rope_tables(s):
    rows = s // GRID_W
    t_row = jnp.repeat(jnp.arange(rows), GRID_W).astype(F32)
    t_col = jnp.tile(jnp.arange(GRID_W), rows).astype(F32)
    inv = ROPE_THETA ** (-jnp.arange(ROPE_PAIRS, dtype=F32) / ROPE_PAIRS)
    ar = t_row[:, None] * inv
    ac = t_col[:, None] * inv
    ang = jnp.concatenate([ar, ar, ac, ac], axis=-1)
    cos, sin = jnp.cos(ang), jnp.sin(ang)
    low_half = (jnp.arange(HEAD_DIM) % (2 * ROPE_PAIRS)) < ROPE_PAIRS
    return cos, jnp.where(low_half, -sin, 0.0), jnp.where(low_half, 0.0, sin)


def kernel(x, mem, norm_mix, w_in, q_norm, k_norm, sgu_norm, w_spatial, b_spatial, mem_norm,
           w_mem_kv, mq_norm, mk_norm, w_attn_o, w_gmlp_o, w_mem_o, w_out, norm_ffn,
           w_ffn_up, w_ffn_down):
    b, s, d = x.shape
    assert b == 1 and d == D_MODEL and norm_mix.shape[0] == 1
    xs = x[0]
    cos, slo, shi = _rope_tables(s)

    w_in_b = w_in[0].astype(BF16)
    o1 = ATTN_Q_W
    o2 = o1 + 2 * ATTN_KV_W
    o3 = o2 + 2 * GMLP_WIDTH
    o4 = o3 + MEM_WIDTH
    w_q, w_kv, w_zg, w_qm, w_gate = (w_in_b[:, :o1], w_in_b[:, o1:o2], w_in_b[:, o2:o3],
                                     w_in_b[:, o3:o4], w_in_b[:, o4:])
    bias_full = jnp.repeat(b_spatial[0].T, HEAD_DIM, axis=1)

    h = _rmsnorm(xs, norm_mix[0], 512)
    kt_aug, v, k2max = _kvproj(h, w_kv, k_norm[0], cos, slo, shi, 512)
    kmax = jnp.sqrt(jnp.max(k2max)).reshape(1, 1)
    q_aug, bmax = _qproj(h, w_q, q_norm[0], cos, slo, shi, kmax, 512, ATTN_TQ)
    gm = _gmlp(h, w_zg, sgu_norm[0], w_spatial[0].astype(BF16), bias_full, 512)
    km_t, vm = _memkv(mem[0], mem_norm[0], w_mem_kv[0].astype(BF16), mk_norm[0])
    ym = _memattn(h, w_qm, mq_norm[0], km_t, vm, 512)
    gates = _gates(h, w_gate, 1024, 1024)
    attn = lax.cond(jnp.max(bmax) <= MAX_SAFE_SHIFT,
                    functools.partial(_attention, bounded=True),
                    functools.partial(_attention, bounded=False),
                    q_aug, kt_aug, v)
    merged = _merge(attn, gm, ym, gates, w_attn_o[0].astype(BF16), w_gmlp_o[0].astype(BF16),
                    w_mem_o[0].astype(BF16), 512, 1024)
    x1, h2 = _outproj(merged, w_out[0].astype(BF16), xs, norm_ffn[0], 512)
    a = _ffn_up(h2, w_ffn_up[0].astype(BF16), 1024, 1024)
    out = _ffn_down(a, w_ffn_down[0].astype(BF16), x1, 1024, 1024, 2048)
    return out[None]
```

```python
import functools
import math

import jax
import jax.numpy as jnp
from jax import lax
from jax.experimental import pallas as pl
from jax.experimental.pallas import tpu as pltpu

D_MODEL = 2048
HEAD_DIM = 128
N_Q_HEADS = 8
N_KV_HEADS = 2
Q_PER_KV = N_Q_HEADS // N_KV_HEADS
GRID_W = 64
ROPE_THETA = 10000.0
ROPE_PAIRS = HEAD_DIM // 4
GMLP_GROUPS = 4
GMLP_WIDTH = GMLP_GROUPS * HEAD_DIM
CHUNK = 128
N_MEM_HEADS = 4
MEM_WIDTH = N_MEM_HEADS * HEAD_DIM
D_FF = 4 * D_MODEL
EPS = 1e-6
ATTN_Q_W = N_Q_HEADS * HEAD_DIM
ATTN_KV_W = N_KV_HEADS * HEAD_DIM

VMEM_LIMIT_BYTES = 56 * 1024 * 1024
LOG2E = math.log2(math.e)
Q_SCALE = HEAD_DIM ** -0.5 * LOG2E
KV_BLOCK = 256
ATTN_TQ = 256
MAX_SAFE_SHIFT = 50.0

F32 = jnp.float32
BF16 = jnp.bfloat16


def _params(*sem):
    return pltpu.CompilerParams(dimension_semantics=sem, vmem_limit_bytes=VMEM_LIMIT_BYTES)


def _dot(a, b):
    return jnp.dot(a, b, preferred_element_type=F32)


def _head_rms(x, gain):
    ms = jnp.mean(x * x, axis=-1, keepdims=True)
    return x * lax.rsqrt(ms + EPS) * gain


def _rope(y, cos, sin_lo, sin_hi):
    return (y * cos + pltpu.roll(y, HEAD_DIM - ROPE_PAIRS, 1) * sin_lo
            + pltpu.roll(y, ROPE_PAIRS, 1) * sin_hi)


def _rmsnorm_kernel(x_ref, g_ref, o_ref):
    o_ref[...] = _head_rms(x_ref[...], g_ref[...]).astype(o_ref.dtype)


def _rmsnorm(x, gain, tm):
    m, d = x.shape
    return pl.pallas_call(
        _rmsnorm_kernel,
        out_shape=jax.ShapeDtypeStruct((m, d), BF16),
        grid=(m // tm,),
        in_specs=[pl.BlockSpec((tm, d), lambda i: (i, 0)),
                  pl.BlockSpec((1, d), lambda i: (0, 0))],
        out_specs=pl.BlockSpec((tm, d), lambda i: (i, 0)),
        compiler_params=_params("parallel"),
        name="rmsnorm",
    )(x, gain.reshape(1, d))


def _qproj_kernel(h_ref, w_ref, g_ref, cos_ref, slo_ref, shi_ref, kmax_ref, o_ref, bmax_ref,
                  *, tm, tq):
    acc = _dot(h_ref[...], w_ref[...])
    cos, slo, shi = cos_ref[...], slo_ref[...], shi_ref[...]
    g = g_ref[...]
    kmax = kmax_ref[...]
    first_lane = lax.broadcasted_iota(jnp.int32, (tm, HEAD_DIM), 1) == 0
    bmax = jnp.zeros((1, 1), F32)
    for hd in range(N_Q_HEADS):
        grp, r = divmod(hd, Q_PER_KV)
        sl = slice(hd * HEAD_DIM, (hd + 1) * HEAD_DIM)
        y = _rope(_head_rms(acc[:, sl], g), cos, slo, shi) * Q_SCALE
        bound = jnp.sqrt(jnp.sum(y * y, axis=-1, keepdims=True)) * kmax
        bmax = jnp.maximum(bmax, jnp.max(bound, axis=0, keepdims=True))
        yb = y.astype(o_ref.dtype)
        aug = jnp.where(first_lane, bound, 0.0).astype(o_ref.dtype)
        for t in range(tm // tq):
            o_ref[t, grp, r * tq:(r + 1) * tq, :HEAD_DIM] = yb[t * tq:(t + 1) * tq]
            o_ref[t, grp, r * tq:(r + 1) * tq, HEAD_DIM:] = aug[t * tq:(t + 1) * tq]
    bmax_ref[...] = jnp.broadcast_to(bmax.reshape(1, 1, 1), bmax_ref.shape)


def _qproj(h, w, gain, cos, slo, shi, kmax, tm, tq):
    m, d = h.shape
    n = w.shape[1]
    row = lambda i: (i, 0)
    fixed = lambda i: (0, 0)
    return pl.pallas_call(
        functools.partial(_qproj_kernel, tm=tm, tq=tq),
        out_shape=(jax.ShapeDtypeStruct((m // tq, N_KV_HEADS, Q_PER_KV * tq, 2 * HEAD_DIM), BF16),
                   jax.ShapeDtypeStruct((m // tm, 8, HEAD_DIM), F32)),
        grid=(m // tm,),
        in_specs=[pl.BlockSpec((tm, d), row), pl.BlockSpec((d, n), fixed),
                  pl.BlockSpec((1, HEAD_DIM), fixed),
                  pl.BlockSpec((tm, HEAD_DIM), row), pl.BlockSpec((tm, HEAD_DIM), row),
                  pl.BlockSpec((tm, HEAD_DIM), row), pl.BlockSpec((1, 1), fixed)],
        out_specs=(pl.BlockSpec((tm // tq, N_KV_HEADS, Q_PER_KV * tq, 2 * HEAD_DIM),
                                lambda i: (i, 0, 0, 0)),
                   pl.BlockSpec((1, 8, HEAD_DIM), lambda i: (i, 0, 0))),
        compiler_params=_params("parallel"),
        name="q_proj",
    )(h, w, gain.reshape(1, HEAD_DIM), cos, slo, shi, kmax)


def _kvproj_kernel(h_ref, w_ref, g_ref, cos_ref, slo_ref, shi_ref, kt_ref, v_ref, k2max_ref, *, tm):
    acc = _dot(h_ref[...], w_ref[...])
    cos, slo, shi = cos_ref[...], slo_ref[...], shi_ref[...]
    g = g_ref[...]
    first_row = lax.broadcasted_iota(jnp.int32, (HEAD_DIM, KV_BLOCK), 0) == 0
    aug = jnp.where(first_row, -1.0, 0.0).astype(kt_ref.dtype)
    k2max = jnp.zeros((1, 1), F32)
    for hd in range(N_KV_HEADS):
        sl = slice(hd * HEAD_DIM, (hd + 1) * HEAD_DIM)
        y = _rope(_head_rms(acc[:, sl], g), cos, slo, shi)
        k2 = jnp.sum(y * y, axis=-1, keepdims=True)
        k2max = jnp.maximum(k2max, jnp.max(k2, axis=0, keepdims=True))
        base = hd * 2 * HEAD_DIM
        for c in range(tm // KV_BLOCK):
            blk = y[c * KV_BLOCK:(c + 1) * KV_BLOCK, :]
            kt_ref[c, base:base + HEAD_DIM, :] = jnp.transpose(blk).astype(kt_ref.dtype)
            kt_ref[c, base + HEAD_DIM:base + 2 * HEAD_DIM, :] = aug
    v_ref[...] = acc[:, ATTN_KV_W:].astype(v_ref.dtype)
    k2max_ref[...] = jnp.broadcast_to(k2max.reshape(1, 1, 1), k2max_ref.shape)


def _kvproj(h, w, gain, cos, slo, shi, tm):
    m, d = h.shape
    n = w.shape[1]
    row = lambda i: (i, 0)
    fixed = lambda i: (0, 0)
    return pl.pallas_call(
        functools.partial(_kvproj_kernel, tm=tm),
        out_shape=(jax.ShapeDtypeStruct((m // KV_BLOCK, 2 * ATTN_KV_W, KV_BLOCK), BF16),
                   jax.ShapeDtypeStruct((m, ATTN_KV_W), BF16),
                   jax.ShapeDtypeStruct((m // tm, 8, HEAD_DIM), F32)),
        grid=(m // tm,),
        in_specs=[pl.BlockSpec((tm, d), row), pl.BlockSpec((d, n), fixed),
                  pl.BlockSpec((1, HEAD_DIM), fixed),
                  pl.BlockSpec((tm, HEAD_DIM), row), pl.BlockSpec((tm, HEAD_DIM), row),
                  pl.BlockSpec((tm, HEAD_DIM), row)],
        out_specs=(pl.BlockSpec((tm // KV_BLOCK, 2 * ATTN_KV_W, KV_BLOCK), lambda i: (i, 0, 0)),
                   pl.BlockSpec((tm, ATTN_KV_W), row),
                   pl.BlockSpec((1, 8, HEAD_DIM), lambda i: (i, 0, 0))),
        compiler_params=_params("parallel"),
        name="kv_proj",
    )(h, w, gain.reshape(1, HEAD_DIM), cos, slo, shi)


def _gmlp_kernel(h_ref, w_ref, g_ref, ws_ref, b_ref, o_ref, *, tm):
    z = jax.nn.gelu(_dot(h_ref[...], w_ref[...]))
    u = z[:, :GMLP_WIDTH]
    vn = _head_rms(z[:, GMLP_WIDTH:], g_ref[...]).astype(BF16)
    bias = b_ref[...]
    for c in range(tm // CHUNK):
        rows = slice(c * CHUNK, (c + 1) * CHUNK)
        for grp in range(GMLP_GROUPS):
            cols = slice(grp * HEAD_DIM, (grp + 1) * HEAD_DIM)
            mixed = _dot(ws_ref[grp], vn[rows, cols]) + bias[:, cols]
            o_ref[rows, cols] = (u[rows, cols] * mixed).astype(o_ref.dtype)


def _gmlp(h, w, sgu_gain, w_spatial, bias_full, tm):
    m, d = h.shape
    n = w.shape[1]
    row = lambda i: (i, 0)
    fixed = lambda i: (0, 0)
    return pl.pallas_call(
        functools.partial(_gmlp_kernel, tm=tm),
        out_shape=jax.ShapeDtypeStruct((m, GMLP_WIDTH), BF16),
        grid=(m // tm,),
        in_specs=[pl.BlockSpec((tm, d), row), pl.BlockSpec((d, n), fixed),
                  pl.BlockSpec((1, GMLP_WIDTH), fixed),
                  pl.BlockSpec((GMLP_GROUPS, CHUNK, CHUNK), lambda i: (0, 0, 0)),
                  pl.BlockSpec((CHUNK, GMLP_WIDTH), fixed)],
        out_specs=pl.BlockSpec((tm, GMLP_WIDTH), row),
        compiler_params=_params("parallel"),
        name="gmlp",
    )(h, w, sgu_gain.reshape(1, GMLP_WIDTH), w_spatial, bias_full)


def _memkv_kernel(mem_ref, g_ref, w_ref, kg_ref, kt_ref, v_ref):
    mem_n = _head_rms(mem_ref[...], g_ref[...]).astype(BF16)
    kv = _dot(mem_n, w_ref[...])
    kg = kg_ref[...]
    for hd in range(N_MEM_HEADS):
        sl = slice(hd * HEAD_DIM, (hd + 1) * HEAD_DIM)
        kt_ref[sl, :] = jnp.transpose(_head_rms(kv[:, sl], kg)).astype(kt_ref.dtype)
    v_ref[...] = kv[:, MEM_WIDTH:].astype(v_ref.dtype)


def _memkv(mem, mem_gain, w, mk_gain):
    mt, d = mem.shape
    return pl.pallas_call(
        _memkv_kernel,
        out_shape=(jax.ShapeDtypeStruct((MEM_WIDTH, mt), BF16),
                   jax.ShapeDtypeStruct((mt, MEM_WIDTH), BF16)),
        compiler_params=pltpu.CompilerParams(vmem_limit_bytes=VMEM_LIMIT_BYTES),
        name="mem_kv",
    )(mem, mem_gain.reshape(1, d), w, mk_gain.reshape(1, HEAD_DIM))


def _memattn_kernel(h_ref, w_ref, g_ref, kt_ref, v_ref, o_ref):
    acc = _dot(h_ref[...], w_ref[...])
    g = g_ref[...]
    for hd in range(N_MEM_HEADS):
        sl = slice(hd * HEAD_DIM, (hd + 1) * HEAD_DIM)
        qn = (_head_rms(acc[:, sl], g) * Q_SCALE).astype(BF16)
        s = _dot(qn, kt_ref[sl, :])
        p = jnp.exp2(s - jnp.max(s, axis=-1, keepdims=True))
        l = jnp.sum(p, axis=-1, keepdims=True)
        o = _dot(p.astype(BF16), v_ref[:, sl])
        o_ref[:, sl] = (o / l).astype(o_ref.dtype)


def _memattn(h, w, mq_gain, km_t, vm, tm):
    m, d = h.shape
    n = w.shape[1]
    mt = vm.shape[0]
    row = lambda i: (i, 0)
    fixed = lambda i: (0, 0)
    return pl.pallas_call(
        _memattn_kernel,
        out_shape=jax.ShapeDtypeStruct((m, MEM_WIDTH), BF16),
        grid=(m // tm,),
        in_specs=[pl.BlockSpec((tm, d), row), pl.BlockSpec((d, n), fixed),
                  pl.BlockSpec((1, HEAD_DIM), fixed),
                  pl.BlockSpec((MEM_WIDTH, mt), fixed), pl.BlockSpec((mt, MEM_WIDTH), fixed)],
        out_specs=pl.BlockSpec((tm, MEM_WIDTH), row),
        compiler_params=_params("parallel"),
        name="mem_attn",
    )(h, w, mq_gain.reshape(1, HEAD_DIM), km_t, vm)


def _gates_kernel(h_ref, w_ref, o_ref):
    z = _dot(h_ref[...], w_ref[...])
    o_ref[...] = (1.0 / (1.0 + jnp.exp(-z))).astype(o_ref.dtype)


def _gates(h, w, tm, tn):
    m, d = h.shape
    n = w.shape[1]
    return pl.pallas_call(
        _gates_kernel,
        out_shape=jax.ShapeDtypeStruct((m, n), BF16),
        grid=(m // tm, n // tn),
        in_specs=[pl.BlockSpec((tm, d), lambda i, j: (i, 0)),
                  pl.BlockSpec((d, tn), lambda i, j: (0, j))],
        out_specs=pl.BlockSpec((tm, tn), lambda i, j: (i, j)),
        compiler_params=_params("parallel", "parallel"),
        name="gates",
    )(h, w)


def _store_heads(o_ref, out, tq):
    for r in range(Q_PER_KV):
        o_ref[:, r * HEAD_DIM:(r + 1) * HEAD_DIM] = out[r * tq:(r + 1) * tq].astype(o_ref.dtype)


def _attn_bounded_kernel(q_ref, kt_ref, v_ref, o_ref, l_ref, acc_ref, *, tq, nk):
    l_ref[...] = jnp.zeros_like(l_ref)
    acc_ref[...] = jnp.zeros_like(acc_ref)

    def body(j, carry):
        off = pl.multiple_of(j * KV_BLOCK, KV_BLOCK)
        p = jnp.exp2(_dot(q_ref[0, 0], kt_ref[j]))
        l_ref[...] += p[:, :HEAD_DIM] + p[:, HEAD_DIM:]
        acc_ref[...] += _dot(p.astype(BF16), v_ref[pl.ds(off, KV_BLOCK), :])
        return carry

    lax.fori_loop(0, nk, body, 0, unroll=True)
    l = jnp.sum(l_ref[...], axis=-1, keepdims=True)
    _store_heads(o_ref, acc_ref[...] / l, tq)


def _attn_online_kernel(q_ref, kt_ref, v_ref, o_ref, *, tq, nk):
    q = q_ref[0, 0]
    rows = Q_PER_KV * tq

    def body(j, carry):
        m, l, acc = carry
        off = pl.multiple_of(j * KV_BLOCK, KV_BLOCK)
        s = _dot(q, kt_ref[j])
        m_new = jnp.maximum(m, jnp.max(s, axis=-1, keepdims=True))
        alpha = jnp.exp2(m - m_new)
        p = jnp.exp2(s - m_new)
        l = alpha * l + jnp.sum(p, axis=-1, keepdims=True)
        acc = alpha * acc + _dot(p.astype(BF16), v_ref[pl.ds(off, KV_BLOCK), :])
        return m_new, l, acc

    m0 = jnp.full((rows, 1), -jnp.inf, F32)
    l0 = jnp.zeros((rows, 1), F32)
    acc0 = jnp.zeros((rows, HEAD_DIM), F32)
    _, l, acc = lax.fori_loop(0, nk, body, (m0, l0, acc0))
    _store_heads(o_ref, acc / l, tq)


def _attention(q_aug, kt_aug, v, bounded):
    nq, _, rows, qw = q_aug.shape
    tq = rows // Q_PER_KV
    s = v.shape[0]
    nk = s // KV_BLOCK
    gw = Q_PER_KV * HEAD_DIM
    if bounded:
        body = functools.partial(_attn_bounded_kernel, tq=tq, nk=nk)
        width, kblk = qw, (lambda g, i: (0, g, 0))
        scratch = [pltpu.VMEM((rows, HEAD_DIM), F32), pltpu.VMEM((rows, HEAD_DIM), F32)]
        name = "attn_bounded"
    else:
        body = functools.partial(_attn_online_kernel, tq=tq, nk=nk)
        width, kblk = HEAD_DIM, (lambda g, i: (0, 2 * g, 0))
        scratch = []
        name = "attn_online"
    return pl.pallas_call(
        body,
        out_shape=jax.ShapeDtypeStruct((s, ATTN_Q_W), BF16),
        grid=(N_KV_HEADS, nq),
        in_specs=[pl.BlockSpec((1, 1, rows, width), lambda g, i: (i, g, 0, 0)),
                  pl.BlockSpec((nk, width, KV_BLOCK), kblk),
                  pl.BlockSpec((s, HEAD_DIM), lambda g, i: (0, g))],
        out_specs=pl.BlockSpec((tq, gw), lambda g, i: (i, g)),
        scratch_shapes=scratch,
        compiler_params=_params("parallel", "parallel"),
        name=name,
    )(q_aug, kt_aug, v)


def _merge_kernel(a_ref, gm_ref, ym_ref, ga_ref, gg_ref, gmm_ref, wa_ref, wg_ref, wm_ref, o_ref):
    y = ga_ref[...].astype(F32) * _dot(a_ref[...], wa_ref[...])
    y += gg_ref[...].astype(F32) * _dot(gm_ref[...], wg_ref[...])
    y += gmm_ref[...].astype(F32) * _dot(ym_ref[...], wm_ref[...])
    o_ref[...] = y.astype(o_ref.dtype)


def _merge(attn, gm, ym, gates, w_a, w_g, w_m, tm, tn):
    m = attn.shape[0]
    n = w_a.shape[1]
    nb = n // tn
    row = lambda i, j: (i, 0)
    col = lambda i, j: (0, j)
    return pl.pallas_call(
        _merge_kernel,
        out_shape=jax.ShapeDtypeStruct((m, n), BF16),
        grid=(m // tm, nb),
        in_specs=[pl.BlockSpec((tm, attn.shape[1]), row),
                  pl.BlockSpec((tm, gm.shape[1]), row),
                  pl.BlockSpec((tm, ym.shape[1]), row),
                  pl.BlockSpec((tm, tn), lambda i, j: (i, j)),
                  pl.BlockSpec((tm, tn), lambda i, j: (i, j + nb)),
                  pl.BlockSpec((tm, tn), lambda i, j: (i, j + 2 * nb)),
                  pl.BlockSpec((w_a.shape[0], tn), col),
                  pl.BlockSpec((w_g.shape[0], tn), col),
                  pl.BlockSpec((w_m.shape[0], tn), col)],
        out_specs=pl.BlockSpec((tm, tn), lambda i, j: (i, j)),
        compiler_params=_params("parallel", "parallel"),
        name="merge",
    )(attn, gm, ym, gates, gates, gates, w_a, w_g, w_m)


def _outproj_kernel(mg_ref, w_ref, x_ref, g_ref, x1_ref, h2_ref):
    x1 = x_ref[...] + _dot(mg_ref[...], w_ref[...])
    x1_ref[...] = x1
    h2_ref[...] = _head_rms(x1, g_ref[...]).astype(h2_ref.dtype)


def _outproj(merged, w, x, gain, tm):
    m, d = x.shape
    row = lambda i: (i, 0)
    fixed = lambda i: (0, 0)
    return pl.pallas_call(
        _outproj_kernel,
        out_shape=(jax.ShapeDtypeStruct((m, d), F32), jax.ShapeDtypeStruct((m, d), BF16)),
        grid=(m // tm,),
        in_specs=[pl.BlockSpec((tm, d), row), pl.BlockSpec((d, d), fixed),
                  pl.BlockSpec((tm, d), row), pl.BlockSpec((1, d), fixed)],
        out_specs=(pl.BlockSpec((tm, d), row), pl.BlockSpec((tm, d), row)),
        compiler_params=_params("parallel"),
        name="out_proj",
    )(merged, w, x, gain.reshape(1, d))


def _ffn_up_kernel(h_ref, w_ref, o_ref):
    z = jnp.maximum(_dot(h_ref[...], w_ref[...]), 0.0)
    o_ref[...] = (z * z).astype(o_ref.dtype)


def _ffn_up(h, w, tm, tn):
    m, d = h.shape
    n = w.shape[1]
    return pl.pallas_call(
        _ffn_up_kernel,
        out_shape=jax.ShapeDtypeStruct((m, n), BF16),
        grid=(m // tm, n // tn),
        in_specs=[pl.BlockSpec((tm, d), lambda i, j: (i, 0)),
                  pl.BlockSpec((d, tn), lambda i, j: (0, j))],
        out_specs=pl.BlockSpec((tm, tn), lambda i, j: (i, j)),
        compiler_params=_params("parallel", "parallel"),
        name="ffn_up",
    )(h, w)


def _ffn_down_kernel(a_ref, w_ref, x_ref, o_ref, acc_ref):
    k = pl.program_id(2)

    @pl.when(k == 0)
    def _():
        acc_ref[...] = x_ref[...]

    acc_ref[...] += _dot(a_ref[...], w_ref[...])

    @pl.when(k == pl.num_programs(2) - 1)
    def _():
        o_ref[...] = acc_ref[...]


def _ffn_down(a, w, x1, tm, tn, tk):
    m, kdim = a.shape
    n = w.shape[1]
    return pl.pallas_call(
        _ffn_down_kernel,
        out_shape=jax.ShapeDtypeStruct((m, n), F32),
        grid=(m // tm, n // tn, kdim // tk),
        in_specs=[pl.BlockSpec((tm, tk), lambda i, j, k: (i, k)),
                  pl.BlockSpec((tk, tn), lambda i, j, k: (k, j)),
                  pl.BlockSpec((tm, tn), lambda i, j, k: (i, j))],
        out_specs=pl.BlockSpec((tm, tn), lambda i, j, k: (i, j)),
        scratch_shapes=[pltpu.VMEM((tm, tn), F32)],
        compiler_params=_params("parallel", "parallel", "arbitrary"),
        name="ffn_down",
    )(a, w, x1)


def _rope_tables(s):
    rows = s // GRID_W
    t_row = jnp.repeat(jnp.arange(rows), GRID_W).astype(F32)
    t_col = jnp.tile(jnp.arange(GRID_W), rows).astype(F32)
    inv = ROPE_THETA ** (-jnp.arange(ROPE_PAIRS, dtype=F32) / ROPE_PAIRS)
    ar = t_row[:, None] * inv
    ac = t_col[:, None] * inv
    ang = jnp.concatenate([ar, ar, ac, ac], axis=-1)
    cos, sin = jnp.cos(ang), jnp.sin(ang)
    low_half = (jnp.arange(HEAD_DIM) % (2 * ROPE_PAIRS)) < ROPE_PAIRS
    return cos, jnp.where(low_half, -sin, 0.0), jnp.where(low_half, 0.0, sin)


def kernel(x, mem, norm_mix, w_in, q_norm, k_norm, sgu_norm, w_spatial, b_spatial, mem_norm,
           w_mem_kv, mq_norm, mk_norm, w_attn_o, w_gmlp_o, w_mem_o, w_out, norm_ffn,
           w_ffn_up, w_ffn_down):
    b, s, d = x.shape
    assert b == 1 and d == D_MODEL and norm_mix.shape[0] == 1
    xs = x[0]
    cos, slo, shi = _rope_tables(s)

    w_in_b = w_in[0].astype(BF16)
    o1 = ATTN_Q_W
    o2 = o1 + 2 * ATTN_KV_W
    o3 = o2 + 2 * GMLP_WIDTH
    o4 = o3 + MEM_WIDTH
    w_q, w_kv, w_zg, w_qm, w_gate = (w_in_b[:, :o1], w_in_b[:, o1:o2], w_in_b[:, o2:o3],
                                     w_in_b[:, o3:o4], w_in_b[:, o4:])
    bias_full = jnp.repeat(b_spatial[0].T, HEAD_DIM, axis=1)

    h = _rmsnorm(xs, norm_mix[0], 512)
    kt_aug, v, k2max = _kvproj(h, w_kv, k_norm[0], cos, slo, shi, 512)
    kmax = jnp.sqrt(jnp.max(k2max)).reshape(1, 1)
    q_aug, bmax = _qproj(h, w_q, q_norm[0], cos, slo, shi, kmax, 512, ATTN_TQ)
    gm = _gmlp(h, w_zg, sgu_norm[0], w_spatial[0].astype(BF16), bias_full, 512)
    km_t, vm = _memkv(mem[0], mem_norm[0], w_mem_kv[0].astype(BF16), mk_norm[0])
    ym = _memattn(h, w_qm, mq_norm[0], km_t, vm, 512)
    gates = _gates(h, w_gate, 1024, 1024)
    attn = lax.cond(jnp.max(bmax) <= MAX_SAFE_SHIFT,
                    functools.partial(_attention, bounded=True),
                    functools.partial(_attention, bounded=False),
                    q_aug, kt_aug, v)
    merged = _merge(attn, gm, ym, gates, w_attn_o[0].astype(BF16), w_gmlp_o[0].astype(BF16),
                    w_mem_o[0].astype(BF16), 512, 1024)
    x1, h2 = _outproj(merged, w_out[0].astype(BF16), xs, norm_ffn[0], 512)
    a = _ffn_up(h2, w_ffn_up[0].astype(BF16), 1024, 1024)
    out = _ffn_down(a, w_ffn_down[0].astype(BF16), x1, 1024, 1024, 2048)
    return out[None]
```

```python
import functools
import math

import jax
import jax.numpy as jnp
from jax import lax
from jax.experimental import pallas as pl
from jax.experimental.pallas import tpu as pltpu

D_MODEL = 2048
HEAD_DIM = 128
N_Q_HEADS = 8
N_KV_HEADS = 2
Q_PER_KV = N_Q_HEADS // N_KV_HEADS
GRID_W = 64
ROPE_THETA = 10000.0
ROPE_PAIRS = HEAD_DIM // 4
GMLP_GROUPS = 4
GMLP_WIDTH = GMLP_GROUPS * HEAD_DIM
CHUNK = 128
N_MEM_HEADS = 4
MEM_WIDTH = N_MEM_HEADS * HEAD_DIM
D_FF = 4 * D_MODEL
EPS = 1e-6
ATTN_Q_W = N_Q_HEADS * HEAD_DIM
ATTN_KV_W = N_KV_HEADS * HEAD_DIM
IN_COL_BLOCK = 512
IN_Q_BLK, IN_KV_BLK, IN_U_BLK, IN_V_BLK, IN_QM_BLK, IN_GATE_BLK = 0, 2, 3, 4, 5, 6

VMEM_LIMIT_BYTES = 56 * 1024 * 1024
LOG2E = math.log2(math.e)
Q_SCALE = HEAD_DIM ** -0.5 * LOG2E
KV_BLOCK = 256
ATTN_TQ = 256
MAX_UNSHIFTED_LOGIT = 64.0
ROPE_PERM = tuple(list(range(0, 32)) + list(range(64, 96)) + list(range(32, 64))
                  + list(range(96, 128)))

TM_PROJ = 512
TM_GATES, TN_GATES = 1024, 1024
TM_MERGE, TN_MERGE = 512, 1024
TM_OUT = 512
TM_UP, TN_UP = 1024, 1024
TM_DOWN, TN_DOWN, TK_DOWN = 1024, 1024, 2048

F32 = jnp.float32
BF16 = jnp.bfloat16


def _params(*sem):
    return pltpu.CompilerParams(dimension_semantics=sem, vmem_limit_bytes=VMEM_LIMIT_BYTES)


def _dot(a, b):
    return jnp.dot(a, b, preferred_element_type=F32)


def _head_rms(x, gain):
    ms = jnp.mean(x * x, axis=-1, keepdims=True)
    return x * lax.rsqrt(ms + EPS) * gain


def _rope(y, cos, sin_signed):
    return y * cos + pltpu.roll(y, HEAD_DIM // 2, 1) * sin_signed


def _to_rope_order(w):
    grp = lax.broadcasted_iota(jnp.int32, w.shape, 1) // ROPE_PAIRS
    from_right = pltpu.roll(w, HEAD_DIM - ROPE_PAIRS, 1)
    from_left = pltpu.roll(w, ROPE_PAIRS, 1)
    return jnp.where(grp == 1, from_right, jnp.where(grp == 2, from_left, w))


def _rope_tile(row_ref, col_ref, i, tm):
    nrow = tm // GRID_W
    rows = row_ref[pl.ds(pl.multiple_of(i * nrow, nrow), nrow), :]
    col = col_ref[...]
    return jnp.concatenate([rows[r:r + 1, :] + col for r in range(nrow)], axis=0)


def _cast_side_job(pairs):
    for src, dst in pairs:
        dst[...] = src[...].astype(dst.dtype)


def _side_specs(arrays, nsteps, index):
    in_specs, out_specs, out_shapes = [], [], []
    for a in arrays:
        rows, cols = a.shape
        blk = (rows // nsteps, cols)
        in_specs.append(pl.BlockSpec(blk, index))
        out_specs.append(pl.BlockSpec(blk, index))
        out_shapes.append(jax.ShapeDtypeStruct(a.shape, BF16))
    return in_specs, out_specs, out_shapes


def _kvproj_kernel(x_ref, gx_ref, w_ref, gk_ref, crow_ref, srow_ref, ccol_ref, scol_ref,
                   h_ref, kt_ref, v_ref, wb_ref, *, tm):
    i = pl.program_id(0)

    @pl.when(i == 0)
    def _():
        for hd in range(N_KV_HEADS):
            sl = slice(hd * HEAD_DIM, (hd + 1) * HEAD_DIM)
            wb_ref[:, sl] = _to_rope_order(w_ref[:, sl]).astype(BF16)
        wb_ref[:, ATTN_KV_W:] = w_ref[:, ATTN_KV_W:].astype(BF16)

    h = _head_rms(x_ref[...], gx_ref[...]).astype(BF16)
    h_ref[...] = h
    acc = _dot(h, wb_ref[...])
    cos = _rope_tile(crow_ref, ccol_ref, i, tm)
    sin = _rope_tile(srow_ref, scol_ref, i, tm)
    g = gk_ref[...]
    for hd in range(N_KV_HEADS):
        sl = slice(hd * HEAD_DIM, (hd + 1) * HEAD_DIM)
        y = _rope(_head_rms(acc[:, sl], g), cos, sin)
        for c in range(tm // KV_BLOCK):
            blk = y[c * KV_BLOCK:(c + 1) * KV_BLOCK, :]
            kt_ref[c, sl, :] = jnp.transpose(blk).astype(kt_ref.dtype)
    v_ref[...] = acc[:, ATTN_KV_W:].astype(v_ref.dtype)


def _kvproj(x, gx, w_in, gk, tables, tm):
    m, d = x.shape
    row = lambda i: (i, 0)
    fixed = lambda i: (0, 0)
    n = 2 * ATTN_KV_W
    tab_specs = [pl.BlockSpec(t.shape, fixed) for t in tables]
    return pl.pallas_call(
        functools.partial(_kvproj_kernel, tm=tm),
        out_shape=(jax.ShapeDtypeStruct((m, d), BF16),
                   jax.ShapeDtypeStruct((m // KV_BLOCK, ATTN_KV_W, KV_BLOCK), BF16),
                   jax.ShapeDtypeStruct((m, ATTN_KV_W), BF16)),
        grid=(m // tm,),
        in_specs=[pl.BlockSpec((tm, d), row), pl.BlockSpec((1, d), fixed),
                  pl.BlockSpec((d, n), lambda i: (0, IN_KV_BLK)),
                  pl.BlockSpec((1, HEAD_DIM), fixed)] + tab_specs,
        out_specs=(pl.BlockSpec((tm, d), row),
                   pl.BlockSpec((tm // KV_BLOCK, ATTN_KV_W, KV_BLOCK), lambda i: (i, 0, 0)),
                   pl.BlockSpec((tm, ATTN_KV_W), row)),
        scratch_shapes=[pltpu.VMEM((d, n), BF16)],
        compiler_params=_params("arbitrary"),
        name="kv_proj",
    )(x, gx.reshape(1, d), w_in, gk.reshape(1, HEAD_DIM), *tables)


def _qproj_kernel(h_ref, w_ref, g_ref, crow_ref, srow_ref, ccol_ref, scol_ref, o_ref, wb_ref,
                  *, tm, tq):
    i = pl.program_id(0)

    @pl.when(i == 0)
    def _():
        for hd in range(N_Q_HEADS):
            sl = slice(hd * HEAD_DIM, (hd + 1) * HEAD_DIM)
            wb_ref[:, sl] = _to_rope_order(w_ref[:, sl]).astype(BF16)

    cos = _rope_tile(crow_ref, ccol_ref, i, tm)
    sin = _rope_tile(srow_ref, scol_ref, i, tm)
    g = g_ref[...]
    h = h_ref[...]
    pair_w = 2 * HEAD_DIM
    for pair in range(N_Q_HEADS // 2):
        acc = _dot(h, wb_ref[:, pair * pair_w:(pair + 1) * pair_w])
        for k in range(2):
            grp, r = divmod(2 * pair + k, Q_PER_KV)
            y = _rope(_head_rms(acc[:, k * HEAD_DIM:(k + 1) * HEAD_DIM], g), cos, sin)
            y = y.astype(o_ref.dtype)
            for t in range(tm // tq):
                o_ref[t, grp, r * tq:(r + 1) * tq, :] = y[t * tq:(t + 1) * tq]


def _qproj(h, w_in, gain, tables, tm, tq):
    m, d = h.shape
    row = lambda i: (i, 0)
    fixed = lambda i: (0, 0)
    tab_specs = [pl.BlockSpec(t.shape, fixed) for t in tables]
    oshape = (m // tq, N_KV_HEADS, Q_PER_KV * tq, HEAD_DIM)
    return pl.pallas_call(
        functools.partial(_qproj_kernel, tm=tm, tq=tq),
        out_shape=jax.ShapeDtypeStruct(oshape, BF16),
        grid=(m // tm,),
        in_specs=[pl.BlockSpec((tm, d), row),
                  pl.BlockSpec((d, ATTN_Q_W), lambda i: (0, IN_Q_BLK)),
                  pl.BlockSpec((1, HEAD_DIM), fixed)] + tab_specs,
        out_specs=pl.BlockSpec((tm // tq,) + oshape[1:], lambda i: (i, 0, 0, 0)),
        scratch_shapes=[pltpu.VMEM((d, ATTN_Q_W), BF16)],
        compiler_params=_params("arbitrary"),
        name="q_proj",
    )(h, w_in, gain.reshape(1, HEAD_DIM), *tables)


def _gmlp_kernel(h_ref, wu_ref, wv_ref, g_ref, ws_ref, b_ref, wo_ref, o_ref, wob_ref, wb_ref,
                 *, tm):
    @pl.when(pl.program_id(0) == 0)
    def _():
        wb_ref[:, :GMLP_WIDTH] = wu_ref[...].astype(BF16)
        wb_ref[:, GMLP_WIDTH:] = wv_ref[...].astype(BF16)

    _cast_side_job([(wo_ref, wob_ref)])
    z = jax.nn.gelu(_dot(h_ref[...], wb_ref[...]))
    u = z[:, :GMLP_WIDTH]
    vn = _head_rms(z[:, GMLP_WIDTH:], g_ref[...]).astype(BF16)
    bias = b_ref[...]
    for c in range(tm // CHUNK):
        rows = slice(c * CHUNK, (c + 1) * CHUNK)
        for grp in range(GMLP_GROUPS):
            cols = slice(grp * HEAD_DIM, (grp + 1) * HEAD_DIM)
            mixed = _dot(ws_ref[grp].astype(BF16), vn[rows, cols]) + bias[:, cols]
            o_ref[rows, cols] = (u[rows, cols] * mixed).astype(o_ref.dtype)


def _gmlp(h, w_in, sgu_gain, w_spatial, bias_full, w_o, tm):
    m, d = h.shape
    row = lambda i: (i, 0)
    fixed = lambda i: (0, 0)
    side_in, side_out, side_shapes = _side_specs([w_o], m // tm, row)
    return pl.pallas_call(
        functools.partial(_gmlp_kernel, tm=tm),
        out_shape=[jax.ShapeDtypeStruct((m, GMLP_WIDTH), BF16)] + side_shapes,
        grid=(m // tm,),
        in_specs=[pl.BlockSpec((tm, d), row),
                  pl.BlockSpec((d, GMLP_WIDTH), lambda i: (0, IN_U_BLK)),
                  pl.BlockSpec((d, GMLP_WIDTH), lambda i: (0, IN_V_BLK)),
                  pl.BlockSpec((1, GMLP_WIDTH), fixed),
                  pl.BlockSpec((GMLP_GROUPS, CHUNK, CHUNK), lambda i: (0, 0, 0)),
                  pl.BlockSpec((CHUNK, GMLP_WIDTH), fixed)] + side_in,
        out_specs=[pl.BlockSpec((tm, GMLP_WIDTH), row)] + side_out,
        scratch_shapes=[pltpu.VMEM((d, 2 * GMLP_WIDTH), BF16)],
        compiler_params=_params("arbitrary"),
        name="gmlp",
    )(h, w_in, w_in, sgu_gain.reshape(1, GMLP_WIDTH), w_spatial, bias_full, w_o)


def _memkv_kernel(mem_ref, g_ref, w_ref, kg_ref, kt_ref, v_ref):
    mem_n = _head_rms(mem_ref[...], g_ref[...]).astype(BF16)
    kv = _dot(mem_n, w_ref[...].astype(BF16))
    kg = kg_ref[...]
    for hd in range(N_MEM_HEADS):
        sl = slice(hd * HEAD_DIM, (hd + 1) * HEAD_DIM)
        kt_ref[sl, :] = jnp.transpose(_head_rms(kv[:, sl], kg)).astype(kt_ref.dtype)
    v_ref[...] = kv[:, MEM_WIDTH:].astype(v_ref.dtype)


def _memkv(mem, mem_gain, w, mk_gain):
    mt, d = mem.shape
    return pl.pallas_call(
        _memkv_kernel,
        out_shape=(jax.ShapeDtypeStruct((MEM_WIDTH, mt), BF16),
                   jax.ShapeDtypeStruct((mt, MEM_WIDTH), BF16)),
        compiler_params=pltpu.CompilerParams(vmem_limit_bytes=VMEM_LIMIT_BYTES),
        name="mem_kv",
    )(mem, mem_gain.reshape(1, d), w, mk_gain.reshape(1, HEAD_DIM))


def _memattn_kernel(h_ref, w_ref, g_ref, kt_ref, v_ref, wo_ref, o_ref, wob_ref, wb_ref):
    @pl.when(pl.program_id(0) == 0)
    def _():
        wb_ref[...] = w_ref[...].astype(BF16)

    _cast_side_job([(wo_ref, wob_ref)])
    acc = _dot(h_ref[...], wb_ref[...])
    g = g_ref[...]
    for hd in range(N_MEM_HEADS):
        sl = slice(hd * HEAD_DIM, (hd + 1) * HEAD_DIM)
        qn = _head_rms(acc[:, sl], g).astype(BF16)
        s = _dot(qn, kt_ref[sl, :])
        p = jnp.exp2(s - jnp.max(s, axis=-1, keepdims=True))
        l = jnp.sum(p, axis=-1, keepdims=True)
        o = _dot(p.astype(BF16), v_ref[:, sl])
        o_ref[:, sl] = (o / l).astype(o_ref.dtype)


def _memattn(h, w_in, mq_gain, km_t, vm, w_o, tm):
    m, d = h.shape
    mt = vm.shape[0]
    row = lambda i: (i, 0)
    fixed = lambda i: (0, 0)
    side_in, side_out, side_shapes = _side_specs([w_o], m // tm, row)
    return pl.pallas_call(
        _memattn_kernel,
        out_shape=[jax.ShapeDtypeStruct((m, MEM_WIDTH), BF16)] + side_shapes,
        grid=(m // tm,),
        in_specs=[pl.BlockSpec((tm, d), row),
                  pl.BlockSpec((d, MEM_WIDTH), lambda i: (0, IN_QM_BLK)),
                  pl.BlockSpec((1, HEAD_DIM), fixed),
                  pl.BlockSpec((MEM_WIDTH, mt), fixed),
                  pl.BlockSpec((mt, MEM_WIDTH), fixed)] + side_in,
        out_specs=[pl.BlockSpec((tm, MEM_WIDTH), row)] + side_out,
        scratch_shapes=[pltpu.VMEM((d, MEM_WIDTH), BF16)],
        compiler_params=_params("arbitrary"),
        name="mem_attn",
    )(h, w_in, mq_gain.reshape(1, HEAD_DIM), km_t, vm, w_o)


def _gates_kernel(h_ref, w_ref, o_ref, wb_ref):
    @pl.when(pl.program_id(1) == 0)
    def _():
        wb_ref[...] = w_ref[...].astype(BF16)

    z = _dot(h_ref[...], wb_ref[...])
    o_ref[...] = (1.0 / (1.0 + jnp.exp(-z))).astype(o_ref.dtype)


def _gates(h, w_in, tm, tn):
    m, d = h.shape
    n = w_in.shape[1] - IN_GATE_BLK * IN_COL_BLOCK
    first = IN_GATE_BLK * IN_COL_BLOCK // tn
    return pl.pallas_call(
        _gates_kernel,
        out_shape=jax.ShapeDtypeStruct((m, n), BF16),
        grid=(n // tn, m // tm),
        in_specs=[pl.BlockSpec((tm, d), lambda j, i: (i, 0)),
                  pl.BlockSpec((d, tn), lambda j, i: (0, j + first))],
        out_specs=pl.BlockSpec((tm, tn), lambda j, i: (i, j)),
        scratch_shapes=[pltpu.VMEM((d, tn), BF16)],
        compiler_params=_params("arbitrary", "arbitrary"),
        name="gates",
    )(h, w_in)


def _store_heads(o_ref, out, tq):
    for r in range(Q_PER_KV):
        o_ref[:, r * HEAD_DIM:(r + 1) * HEAD_DIM] = out[r * tq:(r + 1) * tq].astype(o_ref.dtype)


def _attn_unshifted_kernel(q_ref, kt_ref, v_ref, *rest, tq, nk, nside):
    side = rest[:nside]
    o_ref = rest[nside]
    side_out = rest[nside + 1:2 * nside + 1]
    l_ref, acc_ref = rest[2 * nside + 1:]
    _cast_side_job(zip(side, side_out))
    l_ref[...] = jnp.zeros_like(l_ref)
    acc_ref[...] = jnp.zeros_like(acc_ref)

    def body(j, carry):
        off = pl.multiple_of(j * KV_BLOCK, KV_BLOCK)
        p = jnp.exp2(_dot(q_ref[0, 0], kt_ref[j]))
        l_ref[...] += p[:, :HEAD_DIM] + p[:, HEAD_DIM:]
        acc_ref[...] += _dot(p.astype(BF16), v_ref[pl.ds(off, KV_BLOCK), :])
        return carry

    lax.fori_loop(0, nk, body, 0, unroll=True)
    l = jnp.sum(l_ref[...], axis=-1, keepdims=True)
    _store_heads(o_ref, acc_ref[...] / l, tq)


def _attn_online_kernel(q_ref, kt_ref, v_ref, *rest, tq, nk, nside):
    side = rest[:nside]
    o_ref = rest[nside]
    side_out = rest[nside + 1:2 * nside + 1]
    _cast_side_job(zip(side, side_out))
    q = q_ref[0, 0]
    rows = Q_PER_KV * tq

    def body(j, carry):
        m, l, acc = carry
        off = pl.multiple_of(j * KV_BLOCK, KV_BLOCK)
        s = _dot(q, kt_ref[j])
        m_new = jnp.maximum(m, jnp.max(s, axis=-1, keepdims=True))
        alpha = jnp.exp2(m - m_new)
        p = jnp.exp2(s - m_new)
        l = alpha * l + jnp.sum(p, axis=-1, keepdims=True)
        acc = alpha * acc + _dot(p.astype(BF16), v_ref[pl.ds(off, KV_BLOCK), :])
        return m_new, l, acc

    m0 = jnp.full((rows, 1), -jnp.inf, F32)
    l0 = jnp.zeros((rows, 1), F32)
    acc0 = jnp.zeros((rows, HEAD_DIM), F32)
    _, l, acc = lax.fori_loop(0, nk, body, (m0, l0, acc0))
    _store_heads(o_ref, acc / l, tq)


def _attention(q, kt, v, *side_weights, unshifted):
    nq, _, rows, _ = q.shape
    tq = rows // Q_PER_KV
    s = v.shape[0]
    nk = s // KV_BLOCK
    gw = Q_PER_KV * HEAD_DIM
    nside = len(side_weights)
    side_in, side_out, side_shapes = _side_specs(side_weights, N_KV_HEADS * nq,
                                                 lambda g, i: (g * nq + i, 0))
    if unshifted:
        body = functools.partial(_attn_unshifted_kernel, tq=tq, nk=nk, nside=nside)
        scratch = [pltpu.VMEM((rows, HEAD_DIM), F32), pltpu.VMEM((rows, HEAD_DIM), F32)]
        name = "attn_unshifted"
    else:
        body = functools.partial(_attn_online_kernel, tq=tq, nk=nk, nside=nside)
        scratch = []
        name = "attn_online"
    return pl.pallas_call(
        body,
        out_shape=[jax.ShapeDtypeStruct((s, ATTN_Q_W), BF16)] + side_shapes,
        grid=(N_KV_HEADS, nq),
        in_specs=[pl.BlockSpec((1, 1, rows, HEAD_DIM), lambda g, i: (i, g, 0, 0)),
                  pl.BlockSpec((nk, HEAD_DIM, KV_BLOCK), lambda g, i: (0, g, 0)),
                  pl.BlockSpec((s, HEAD_DIM), lambda g, i: (0, g))] + side_in,
        out_specs=[pl.BlockSpec((tq, gw), lambda g, i: (i, g))] + side_out,
        scratch_shapes=scratch,
        compiler_params=_params("arbitrary", "arbitrary"),
        name=name,
    )(q, kt, v, *side_weights)


def _merge_kernel(a_ref, gm_ref, ym_ref, ga_ref, gg_ref, gmm_ref, wa_ref, wg_ref, wm_ref, o_ref):
    y = ga_ref[...].astype(F32) * _dot(a_ref[...], wa_ref[...])
    y += gg_ref[...].astype(F32) * _dot(gm_ref[...], wg_ref[...])
    y += gmm_ref[...].astype(F32) * _dot(ym_ref[...], wm_ref[...])
    o_ref[...] = y.astype(o_ref.dtype)


def _merge(attn, gm, ym, gates, w_a, w_g, w_m, tm, tn):
    m = attn.shape[0]
    n = w_a.shape[1]
    nb = n // tn
    row = lambda i, j: (i, 0)
    col = lambda i, j: (0, j)
    return pl.pallas_call(
        _merge_kernel,
        out_shape=jax.ShapeDtypeStruct((m, n), BF16),
        grid=(m // tm, nb),
        in_specs=[pl.BlockSpec((tm, attn.shape[1]), row),
                  pl.BlockSpec((tm, gm.shape[1]), row),
                  pl.BlockSpec((tm, ym.shape[1]), row),
                  pl.BlockSpec((tm, tn), lambda i, j: (i, j)),
                  pl.BlockSpec((tm, tn), lambda i, j: (i, j + nb)),
                  pl.BlockSpec((tm, tn), lambda i, j: (i, j + 2 * nb)),
                  pl.BlockSpec((w_a.shape[0], tn), col),
                  pl.BlockSpec((w_g.shape[0], tn), col),
                  pl.BlockSpec((w_m.shape[0], tn), col)],
        out_specs=pl.BlockSpec((tm, tn), lambda i, j: (i, j)),
        compiler_params=_params("parallel", "parallel"),
        name="merge",
    )(attn, gm, ym, gates, gates, gates, w_a, w_g, w_m)


def _outproj_kernel(mg_ref, w_ref, x_ref, g_ref, x1_ref, h2_ref):
    x1 = x_ref[...] + _dot(mg_ref[...], w_ref[...])
    x1_ref[...] = x1
    h2_ref[...] = _head_rms(x1, g_ref[...]).astype(h2_ref.dtype)


def _outproj(merged, w, x, gain, tm):
    m, d = x.shape
    row = lambda i: (i, 0)
    fixed = lambda i: (0, 0)
    return pl.pallas_call(
        _outproj_kernel,
        out_shape=(jax.ShapeDtypeStruct((m, d), F32), jax.ShapeDtypeStruct((m, d), BF16)),
        grid=(m // tm,),
        in_specs=[pl.BlockSpec((tm, d), row), pl.BlockSpec((d, d), fixed),
                  pl.BlockSpec((tm, d), row), pl.BlockSpec((1, d), fixed)],
        out_specs=(pl.BlockSpec((tm, d), row), pl.BlockSpec((tm, d), row)),
        compiler_params=_params("parallel"),
        name="out_proj",
    )(merged, w, x, gain.reshape(1, d))


def _ffn_up_kernel(h_ref, w_ref, o_ref):
    z = jnp.maximum(_dot(h_ref[...], w_ref[...]), 0.0)
    o_ref[...] = (z * z).astype(o_ref.dtype)


def _ffn_up(h, w, tm, tn):
    m, d = h.shape
    n = w.shape[1]
    return pl.pallas_call(
        _ffn_up_kernel,
        out_shape=jax.ShapeDtypeStruct((m, n), BF16),
        grid=(m // tm, n // tn),
        in_specs=[pl.BlockSpec((tm, d), lambda i, j: (i, 0)),
                  pl.BlockSpec((d, tn), lambda i, j: (0, j))],
        out_specs=pl.BlockSpec((tm, tn), lambda i, j: (i, j)),
        compiler_params=_params("parallel", "parallel"),
        name="ffn_up",
    )(h, w)


def _ffn_down_kernel(a_ref, w_ref, x_ref, o_ref, acc_ref):
    k = pl.program_id(2)

    @pl.when(k == 0)
    def _():
        acc_ref[...] = x_ref[...]

    acc_ref[...] += _dot(a_ref[...], w_ref[...])

    @pl.when(k == pl.num_programs(2) - 1)
    def _():
        o_ref[...] = acc_ref[...]


def _ffn_down(a, w, x1, tm, tn, tk):
    m, kdim = a.shape
    n = w.shape[1]
    return pl.pallas_call(
        _ffn_down_kernel,
        out_shape=jax.ShapeDtypeStruct((m, n), F32),
        grid=(m // tm, n // tn, kdim // tk),
        in_specs=[pl.BlockSpec((tm, tk), lambda i, j, k: (i, k)),
                  pl.BlockSpec((tk, tn), lambda i, j, k: (k, j)),
                  pl.BlockSpec((tm, tn), lambda i, j, k: (i, j))],
        out_specs=pl.BlockSpec((tm, tn), lambda i, j, k: (i, j)),
        scratch_shapes=[pltpu.VMEM((tm, tn), F32)],
        compiler_params=_params("parallel", "parallel", "arbitrary"),
        name="ffn_down",
    )(a, w, x1)


def _rope_tables(s):
    inv = ROPE_THETA ** (-jnp.arange(ROPE_PAIRS, dtype=F32) / ROPE_PAIRS)
    inv_lane = jnp.tile(inv, HEAD_DIM // ROPE_PAIRS)
    lane_grp = jnp.arange(HEAD_DIM) // ROPE_PAIRS
    is_row = (lane_grp % 2) == 0
    sign = jnp.where(lane_grp < 2, -1.0, 1.0).astype(F32)

    def tables(n):
        ang = jnp.arange(n, dtype=F32)[:, None] * inv_lane
        return jnp.cos(ang), jnp.sin(ang) * sign

    cr, sr = tables(s // GRID_W)
    cc, sc = tables(GRID_W)
    return (jnp.where(is_row, cr, 0.0), jnp.where(is_row, sr, 0.0),
            jnp.where(is_row, 0.0, cc), jnp.where(is_row, 0.0, sc))


def kernel(x, mem, norm_mix, w_in, q_norm, k_norm, sgu_norm, w_spatial, b_spatial, mem_norm,
           w_mem_kv, mq_norm, mk_norm, w_attn_o, w_gmlp_o, w_mem_o, w_out, norm_ffn,
           w_ffn_up, w_ffn_down):
    b, s, d = x.shape
    assert b == 1 and d == D_MODEL and norm_mix.shape[0] == 1
    xs = x[0]
    w_in0 = w_in[0]
    tables = _rope_tables(s)
    perm = jnp.asarray(ROPE_PERM)
    gq = q_norm[0][perm] * Q_SCALE
    gk = k_norm[0][perm]
    bias_full = jnp.repeat(b_spatial[0].T, HEAD_DIM, axis=1)

    h, kt, v = _kvproj(xs, norm_mix[0], w_in0, gk, tables, TM_PROJ)
    q = _qproj(h, w_in0, gq, tables, TM_PROJ, ATTN_TQ)
    gm, w_gmlp_o_b = _gmlp(h, w_in0, sgu_norm[0], w_spatial[0], bias_full, w_gmlp_o[0], TM_PROJ)
    km_t, vm = _memkv(mem[0], mem_norm[0], w_mem_kv[0], mk_norm[0])
    ym, w_mem_o_b = _memattn(h, w_in0, mq_norm[0] * Q_SCALE, km_t, vm, w_mem_o[0], TM_PROJ)
    gates = _gates(h, w_in0, TM_GATES, TN_GATES)

    logit_bound = (1.02 * LOG2E * math.sqrt(HEAD_DIM)
                   * jnp.max(jnp.abs(q_norm[0])) * jnp.max(jnp.abs(k_norm[0])))
    attn, w_attn_o_b, w_out_b, w_up_b, w_down_b = lax.cond(
        logit_bound <= MAX_UNSHIFTED_LOGIT,
        functools.partial(_attention, unshifted=True),
        functools.partial(_attention, unshifted=False),
        q, kt, v, w_attn_o[0], w_out[0], w_ffn_up[0], w_ffn_down[0])

    merged = _merge(attn, gm, ym, gates, w_attn_o_b, w_gmlp_o_b, w_mem_o_b, TM_MERGE, TN_MERGE)
    x1, h2 = _outproj(merged, w_out_b, xs, norm_ffn[0], TM_OUT)
    a = _ffn_up(h2, w_up_b, TM_UP, TN_UP)
    out = _ffn_down(a, w_down_b, x1, TM_DOWN, TN_DOWN, TK_DOWN)
    return out[None]
```

```python
import functools
import math

import jax
import jax.numpy as jnp
from jax import lax
from jax.experimental import pallas as pl
from jax.experimental.pallas import tpu as pltpu

D_MODEL = 2048
HEAD_DIM = 128
N_Q_HEADS = 8
N_KV_HEADS = 2
Q_PER_KV = N_Q_HEADS // N_KV_HEADS
GRID_W = 64
ROPE_THETA = 10000.0
ROPE_PAIRS = HEAD_DIM // 4
GMLP_GROUPS = 4
GMLP_WIDTH = GMLP_GROUPS * HEAD_DIM
CHUNK = 128
N_MEM_HEADS = 4
MEM_WIDTH = N_MEM_HEADS * HEAD_DIM
D_FF = 4 * D_MODEL
EPS = 1e-6
ATTN_Q_W = N_Q_HEADS * HEAD_DIM
ATTN_KV_W = N_KV_HEADS * HEAD_DIM
IN_COL_BLOCK = 512
IN_Q_BLK, IN_KV_BLK, IN_U_BLK, IN_V_BLK, IN_QM_BLK, IN_GATE_BLK = 0, 2, 3, 4, 5, 6

VMEM_LIMIT_BYTES = 56 * 1024 * 1024
LOG2E = math.log2(math.e)
Q_SCALE = HEAD_DIM ** -0.5 * LOG2E
KV_BLOCK = 256
ATTN_TQ = 256
MAX_UNSHIFTED_LOGIT = 64.0
ROPE_PERM = tuple(list(range(0, 32)) + list(range(64, 96)) + list(range(32, 64))
                  + list(range(96, 128)))

TM_PROJ = 512
TM_MERGE, TN_MERGE = 512, 512
TM_OUT = 512
TM_UP, TN_UP = 2048, 1024
TM_DOWN, TN_DOWN, TK_DOWN = 1024, 1024, 2048

F32 = jnp.float32
BF16 = jnp.bfloat16


def _params(*sem):
    return pltpu.CompilerParams(dimension_semantics=sem, vmem_limit_bytes=VMEM_LIMIT_BYTES)


def _dot(a, b):
    return jnp.dot(a, b, preferred_element_type=F32)


def _head_rms(x, gain):
    ms = jnp.mean(x * x, axis=-1, keepdims=True)
    return x * lax.rsqrt(ms + EPS) * gain


def _rope(y, cos, sin_signed):
    return y * cos + pltpu.roll(y, HEAD_DIM // 2, 1) * sin_signed


def _to_rope_order(w):
    grp = lax.broadcasted_iota(jnp.int32, w.shape, 1) // ROPE_PAIRS
    from_right = pltpu.roll(w, HEAD_DIM - ROPE_PAIRS, 1)
    from_left = pltpu.roll(w, ROPE_PAIRS, 1)
    return jnp.where(grp == 1, from_right, jnp.where(grp == 2, from_left, w))


def _rope_tile(row_ref, col_ref, i, tm):
    nrow = tm // GRID_W
    rows = row_ref[pl.ds(pl.multiple_of(i * nrow, nrow), nrow), :]
    col = col_ref[...]
    return jnp.concatenate([rows[r:r + 1, :] + col for r in range(nrow)], axis=0)


def _cast_side_job(pairs):
    for src, dst in pairs:
        dst[...] = src[...].astype(dst.dtype)


def _side_specs(arrays, nsteps, index):
    in_specs, out_specs, out_shapes = [], [], []
    for a in arrays:
        rows, cols = a.shape
        blk = (rows // nsteps, cols)
        in_specs.append(pl.BlockSpec(blk, index))
        out_specs.append(pl.BlockSpec(blk, index))
        out_shapes.append(jax.ShapeDtypeStruct(a.shape, BF16))
    return in_specs, out_specs, out_shapes


def _kvproj_kernel(x_ref, gx_ref, w_ref, gk_ref, crow_ref, srow_ref, ccol_ref, scol_ref,
                   h_ref, kt_ref, v_ref, wb_ref, *, tm):
    i = pl.program_id(0)

    @pl.when(i == 0)
    def _():
        for hd in range(N_KV_HEADS):
            sl = slice(hd * HEAD_DIM, (hd + 1) * HEAD_DIM)
            wb_ref[:, sl] = _to_rope_order(w_ref[:, sl]).astype(BF16)
        wb_ref[:, ATTN_KV_W:] = w_ref[:, ATTN_KV_W:].astype(BF16)

    h = _head_rms(x_ref[...], gx_ref[...]).astype(BF16)
    h_ref[...] = h
    acc = _dot(h, wb_ref[...])
    cos = _rope_tile(crow_ref, ccol_ref, i, tm)
    sin = _rope_tile(srow_ref, scol_ref, i, tm)
    g = gk_ref[...]
    for hd in range(N_KV_HEADS):
        sl = slice(hd * HEAD_DIM, (hd + 1) * HEAD_DIM)
        y = _rope(_head_rms(acc[:, sl], g), cos, sin)
        for c in range(tm // KV_BLOCK):
            blk = y[c * KV_BLOCK:(c + 1) * KV_BLOCK, :]
            kt_ref[c, sl, :] = jnp.transpose(blk).astype(kt_ref.dtype)
    v_ref[...] = acc[:, ATTN_KV_W:].astype(v_ref.dtype)


def _kvproj(x, gx, w_in, gk, tables, tm):
    m, d = x.shape
    row = lambda i: (i, 0)
    fixed = lambda i: (0, 0)
    n = 2 * ATTN_KV_W
    tab_specs = [pl.BlockSpec(t.shape, fixed) for t in tables]
    return pl.pallas_call(
        functools.partial(_kvproj_kernel, tm=tm),
        out_shape=(jax.ShapeDtypeStruct((m, d), BF16),
                   jax.ShapeDtypeStruct((m // KV_BLOCK, ATTN_KV_W, KV_BLOCK), BF16),
                   jax.ShapeDtypeStruct((m, ATTN_KV_W), BF16)),
        grid=(m // tm,),
        in_specs=[pl.BlockSpec((tm, d), row), pl.BlockSpec((1, d), fixed),
                  pl.BlockSpec((d, n), lambda i: (0, IN_KV_BLK)),
                  pl.BlockSpec((1, HEAD_DIM), fixed)] + tab_specs,
        out_specs=(pl.BlockSpec((tm, d), row),
                   pl.BlockSpec((tm // KV_BLOCK, ATTN_KV_W, KV_BLOCK), lambda i: (i, 0, 0)),
                   pl.BlockSpec((tm, ATTN_KV_W), row)),
        scratch_shapes=[pltpu.VMEM((d, n), BF16)],
        compiler_params=_params("arbitrary"),
        name="kv_proj",
    )(x, gx.reshape(1, d), w_in, gk.reshape(1, HEAD_DIM), *tables)


def _qproj_kernel(h_ref, w_ref, g_ref, crow_ref, srow_ref, ccol_ref, scol_ref, o_ref, wb_ref,
                  *, tm, tq):
    i = pl.program_id(0)

    @pl.when(i == 0)
    def _():
        for hd in range(N_Q_HEADS):
            sl = slice(hd * HEAD_DIM, (hd + 1) * HEAD_DIM)
            wb_ref[:, sl] = _to_rope_order(w_ref[:, sl]).astype(BF16)

    cos = _rope_tile(crow_ref, ccol_ref, i, tm)
    sin = _rope_tile(srow_ref, scol_ref, i, tm)
    g = g_ref[...]
    h = h_ref[...]
    pair_w = 2 * HEAD_DIM
    for pair in range(N_Q_HEADS // 2):
        acc = _dot(h, wb_ref[:, pair * pair_w:(pair + 1) * pair_w])
        for k in range(2):
            grp, r = divmod(2 * pair + k, Q_PER_KV)
            y = _rope(_head_rms(acc[:, k * HEAD_DIM:(k + 1) * HEAD_DIM], g), cos, sin)
            y = y.astype(o_ref.dtype)
            for t in range(tm // tq):
                o_ref[t, grp, r * tq:(r + 1) * tq, :] = y[t * tq:(t + 1) * tq]


def _qproj(h, w_in, gain, tables, tm, tq):
    m, d = h.shape
    row = lambda i: (i, 0)
    fixed = lambda i: (0, 0)
    tab_specs = [pl.BlockSpec(t.shape, fixed) for t in tables]
    oshape = (m // tq, N_KV_HEADS, Q_PER_KV * tq, HEAD_DIM)
    return pl.pallas_call(
        functools.partial(_qproj_kernel, tm=tm, tq=tq),
        out_shape=jax.ShapeDtypeStruct(oshape, BF16),
        grid=(m // tm,),
        in_specs=[pl.BlockSpec((tm, d), row),
                  pl.BlockSpec((d, ATTN_Q_W), lambda i: (0, IN_Q_BLK)),
                  pl.BlockSpec((1, HEAD_DIM), fixed)] + tab_specs,
        out_specs=pl.BlockSpec((tm // tq,) + oshape[1:], lambda i: (i, 0, 0, 0)),
        scratch_shapes=[pltpu.VMEM((d, ATTN_Q_W), BF16)],
        compiler_params=_params("arbitrary"),
        name="q_proj",
    )(h, w_in, gain.reshape(1, HEAD_DIM), *tables)


def _gmlp_kernel(h_ref, wu_ref, wv_ref, g_ref, ws_ref, b_ref, wo_ref, o_ref, wob_ref, wb_ref,
                 *, tm):
    @pl.when(pl.program_id(0) == 0)
    def _():
        wb_ref[:, :GMLP_WIDTH] = wu_ref[...].astype(BF16)
        wb_ref[:, GMLP_WIDTH:] = wv_ref[...].astype(BF16)

    _cast_side_job([(wo_ref, wob_ref)])
    z = jax.nn.gelu(_dot(h_ref[...], wb_ref[...]))
    u = z[:, :GMLP_WIDTH]
    vn = _head_rms(z[:, GMLP_WIDTH:], g_ref[...]).astype(BF16)
    bias = b_ref[...]
    for c in range(tm // CHUNK):
        rows = slice(c * CHUNK, (c + 1) * CHUNK)
        for grp in range(GMLP_GROUPS):
            cols = slice(grp * HEAD_DIM, (grp + 1) * HEAD_DIM)
            mixed = _dot(ws_ref[grp].astype(BF16), vn[rows, cols]) + bias[:, cols]
            o_ref[rows, cols] = (u[rows, cols] * mixed).astype(o_ref.dtype)


def _gmlp(h, w_in, sgu_gain, w_spatial, bias_full, w_o, tm):
    m, d = h.shape
    row = lambda i: (i, 0)
    fixed = lambda i: (0, 0)
    side_in, side_out, side_shapes = _side_specs([w_o], m // tm, row)
    return pl.pallas_call(
        functools.partial(_gmlp_kernel, tm=tm),
        out_shape=[jax.ShapeDtypeStruct((m, GMLP_WIDTH), BF16)] + side_shapes,
        grid=(m // tm,),
        in_specs=[pl.BlockSpec((tm, d), row),
                  pl.BlockSpec((d, GMLP_WIDTH), lambda i: (0, IN_U_BLK)),
                  pl.BlockSpec((d, GMLP_WIDTH), lambda i: (0, IN_V_BLK)),
                  pl.BlockSpec((1, GMLP_WIDTH), fixed),
                  pl.BlockSpec((GMLP_GROUPS, CHUNK, CHUNK), lambda i: (0, 0, 0)),
                  pl.BlockSpec((CHUNK, GMLP_WIDTH), fixed)] + side_in,
        out_specs=[pl.BlockSpec((tm, GMLP_WIDTH), row)] + side_out,
        scratch_shapes=[pltpu.VMEM((d, 2 * GMLP_WIDTH), BF16)],
        compiler_params=_params("arbitrary"),
        name="gmlp",
    )(h, w_in, w_in, sgu_gain.reshape(1, GMLP_WIDTH), w_spatial, bias_full, w_o)


def _memkv_kernel(mem_ref, g_ref, w_ref, kg_ref, kt_ref, v_ref):
    mem_n = _head_rms(mem_ref[...], g_ref[...]).astype(BF16)
    kv = _dot(mem_n, w_ref[...].astype(BF16))
    kg = kg_ref[...]
    for hd in range(N_MEM_HEADS):
        sl = slice(hd * HEAD_DIM, (hd + 1) * HEAD_DIM)
        kt_ref[sl, :] = jnp.transpose(_head_rms(kv[:, sl], kg)).astype(kt_ref.dtype)
    v_ref[...] = kv[:, MEM_WIDTH:].astype(v_ref.dtype)


def _memkv(mem, mem_gain, w, mk_gain):
    mt, d = mem.shape
    return pl.pallas_call(
        _memkv_kernel,
        out_shape=(jax.ShapeDtypeStruct((MEM_WIDTH, mt), BF16),
                   jax.ShapeDtypeStruct((mt, MEM_WIDTH), BF16)),
        compiler_params=pltpu.CompilerParams(vmem_limit_bytes=VMEM_LIMIT_BYTES),
        name="mem_kv",
    )(mem, mem_gain.reshape(1, d), w, mk_gain.reshape(1, HEAD_DIM))


def _memattn_kernel(h_ref, w_ref, g_ref, kt_ref, v_ref, wo_ref, o_ref, wob_ref, wb_ref):
    @pl.when(pl.program_id(0) == 0)
    def _():
        wb_ref[...] = w_ref[...].astype(BF16)

    _cast_side_job([(wo_ref, wob_ref)])
    acc = _dot(h_ref[...], wb_ref[...])
    g = g_ref[...]
    for hd in range(N_MEM_HEADS):
        sl = slice(hd * HEAD_DIM, (hd + 1) * HEAD_DIM)
        qn = _head_rms(acc[:, sl], g).astype(BF16)
        s = _dot(qn, kt_ref[sl, :])
        p = jnp.exp2(s - jnp.max(s, axis=-1, keepdims=True))
        l = jnp.sum(p, axis=-1, keepdims=True)
        o = _dot(p.astype(BF16), v_ref[:, sl])
        o_ref[:, sl] = (o / l).astype(o_ref.dtype)


def _memattn(h, w_in, mq_gain, km_t, vm, w_o, tm):
    m, d = h.shape
    mt = vm.shape[0]
    row = lambda i: (i, 0)
    fixed = lambda i: (0, 0)
    side_in, side_out, side_shapes = _side_specs([w_o], m // tm, row)
    return pl.pallas_call(
        _memattn_kernel,
        out_shape=[jax.ShapeDtypeStruct((m, MEM_WIDTH), BF16)] + side_shapes,
        grid=(m // tm,),
        in_specs=[pl.BlockSpec((tm, d), row),
                  pl.BlockSpec((d, MEM_WIDTH), lambda i: (0, IN_QM_BLK)),
                  pl.BlockSpec((1, HEAD_DIM), fixed),
                  pl.BlockSpec((MEM_WIDTH, mt), fixed),
                  pl.BlockSpec((mt, MEM_WIDTH), fixed)] + side_in,
        out_specs=[pl.BlockSpec((tm, MEM_WIDTH), row)] + side_out,
        scratch_shapes=[pltpu.VMEM((d, MEM_WIDTH), BF16)],
        compiler_params=_params("arbitrary"),
        name="mem_attn",
    )(h, w_in, mq_gain.reshape(1, HEAD_DIM), km_t, vm, w_o)


def _store_heads(o_ref, out, tq):
    for r in range(Q_PER_KV):
        o_ref[:, r * HEAD_DIM:(r + 1) * HEAD_DIM] = out[r * tq:(r + 1) * tq].astype(o_ref.dtype)


def _attn_unshifted_kernel(q_ref, kt_ref, v_ref, *rest, tq, nk, nside):
    side = rest[:nside]
    o_ref = rest[nside]
    side_out = rest[nside + 1:2 * nside + 1]
    l_ref, acc_ref = rest[2 * nside + 1:]
    _cast_side_job(zip(side, side_out))
    l_ref[...] = jnp.zeros_like(l_ref)
    acc_ref[...] = jnp.zeros_like(acc_ref)

    def body(j, carry):
        off = pl.multiple_of(j * KV_BLOCK, KV_BLOCK)
        p = jnp.exp2(_dot(q_ref[0, 0], kt_ref[j]))
        l_ref[...] += p[:, :HEAD_DIM] + p[:, HEAD_DIM:]
        acc_ref[...] += _dot(p.astype(BF16), v_ref[pl.ds(off, KV_BLOCK), :])
        return carry

    lax.fori_loop(0, nk, body, 0, unroll=True)
    l = jnp.sum(l_ref[...], axis=-1, keepdims=True)
    _store_heads(o_ref, acc_ref[...] / l, tq)


def _attn_online_kernel(q_ref, kt_ref, v_ref, *rest, tq, nk, nside):
    side = rest[:nside]
    o_ref = rest[nside]
    side_out = rest[nside + 1:2 * nside + 1]
    _cast_side_job(zip(side, side_out))
    q = q_ref[0, 0]
    rows = Q_PER_KV * tq

    def body(j, carry):
        m, l, acc = carry
        off = pl.multiple_of(j * KV_BLOCK, KV_BLOCK)
        s = _dot(q, kt_ref[j])
        m_new = jnp.maximum(m, jnp.max(s, axis=-1, keepdims=True))
        alpha = jnp.exp2(m - m_new)
        p = jnp.exp2(s - m_new)
        l = alpha * l + jnp.sum(p, axis=-1, keepdims=True)
        acc = alpha * acc + _dot(p.astype(BF16), v_ref[pl.ds(off, KV_BLOCK), :])
        return m_new, l, acc

    m0 = jnp.full((rows, 1), -jnp.inf, F32)
    l0 = jnp.zeros((rows, 1), F32)
    acc0 = jnp.zeros((rows, HEAD_DIM), F32)
    _, l, acc = lax.fori_loop(0, nk, body, (m0, l0, acc0))
    _store_heads(o_ref, acc / l, tq)


def _attention(q, kt, v, *side_weights, unshifted):
    nq, _, rows, _ = q.shape
    tq = rows // Q_PER_KV
    s = v.shape[0]
    nk = s // KV_BLOCK
    gw = Q_PER_KV * HEAD_DIM
    nside = len(side_weights)
    side_in, side_out, side_shapes = _side_specs(side_weights, N_KV_HEADS * nq,
                                                 lambda g, i: (g * nq + i, 0))
    if unshifted:
        body = functools.partial(_attn_unshifted_kernel, tq=tq, nk=nk, nside=nside)
        scratch = [pltpu.VMEM((rows, HEAD_DIM), F32), pltpu.VMEM((rows, HEAD_DIM), F32)]
        name = "attn_unshifted"
    else:
        body = functools.partial(_attn_online_kernel, tq=tq, nk=nk, nside=nside)
        scratch = []
        name = "attn_online"
    return pl.pallas_call(
        body,
        out_shape=[jax.ShapeDtypeStruct((s, ATTN_Q_W), BF16)] + side_shapes,
        grid=(N_KV_HEADS, nq),
        in_specs=[pl.BlockSpec((1, 1, rows, HEAD_DIM), lambda g, i: (i, g, 0, 0)),
                  pl.BlockSpec((nk, HEAD_DIM, KV_BLOCK), lambda g, i: (0, g, 0)),
                  pl.BlockSpec((s, HEAD_DIM), lambda g, i: (0, g))] + side_in,
        out_specs=[pl.BlockSpec((tq, gw), lambda g, i: (i, g))] + side_out,
        scratch_shapes=scratch,
        compiler_params=_params("arbitrary", "arbitrary"),
        name=name,
    )(q, kt, v, *side_weights)


def _gated_merge_kernel(h_ref, a_ref, gm_ref, ym_ref, wga_ref, wgg_ref, wgm_ref,
                        wa_ref, wg_ref, wm_ref, o_ref, wb_ref):
    @pl.when(pl.program_id(1) == 0)
    def _():
        for b, w_ref in enumerate((wga_ref, wgg_ref, wgm_ref)):
            wb_ref[b] = w_ref[...].astype(BF16)

    h = h_ref[...]
    y = None
    for b, (y_ref, w_ref) in enumerate(((a_ref, wa_ref), (gm_ref, wg_ref), (ym_ref, wm_ref))):
        gate = 1.0 / (1.0 + jnp.exp(-_dot(h, wb_ref[b])))
        term = gate * _dot(y_ref[...], w_ref[...])
        y = term if y is None else y + term
    o_ref[...] = y.astype(o_ref.dtype)


def _gated_merge(h, attn, gm, ym, w_in, w_a, w_g, w_m, tm, tn):
    m, d = h.shape
    n = w_a.shape[1]
    nb = n // tn
    first = IN_GATE_BLK * IN_COL_BLOCK // tn
    row = lambda j, i: (i, 0)
    col = lambda j, i: (0, j)
    return pl.pallas_call(
        _gated_merge_kernel,
        out_shape=jax.ShapeDtypeStruct((m, n), BF16),
        grid=(nb, m // tm),
        in_specs=[pl.BlockSpec((tm, d), row),
                  pl.BlockSpec((tm, attn.shape[1]), row),
                  pl.BlockSpec((tm, gm.shape[1]), row),
                  pl.BlockSpec((tm, ym.shape[1]), row),
                  pl.BlockSpec((d, tn), lambda j, i: (0, first + j)),
                  pl.BlockSpec((d, tn), lambda j, i: (0, first + nb + j)),
                  pl.BlockSpec((d, tn), lambda j, i: (0, first + 2 * nb + j)),
                  pl.BlockSpec((w_a.shape[0], tn), col),
                  pl.BlockSpec((w_g.shape[0], tn), col),
                  pl.BlockSpec((w_m.shape[0], tn), col)],
        out_specs=pl.BlockSpec((tm, tn), lambda j, i: (i, j)),
        scratch_shapes=[pltpu.VMEM((3, d, tn), BF16)],
        compiler_params=_params("arbitrary", "arbitrary"),
        name="gated_merge",
    )(h, attn, gm, ym, w_in, w_in, w_in, w_a, w_g, w_m)


def _outproj_kernel(mg_ref, w_ref, x_ref, g_ref, x1_ref, h2_ref):
    x1 = x_ref[...] + _dot(mg_ref[...], w_ref[...])
    x1_ref[...] = x1
    h2_ref[...] = _head_rms(x1, g_ref[...]).astype(h2_ref.dtype)


def _outproj(merged, w, x, gain, tm):
    m, d = x.shape
    row = lambda i: (i, 0)
    fixed = lambda i: (0, 0)
    return pl.pallas_call(
        _outproj_kernel,
        out_shape=(jax.ShapeDtypeStruct((m, d), F32), jax.ShapeDtypeStruct((m, d), BF16)),
        grid=(m // tm,),
        in_specs=[pl.BlockSpec((tm, d), row), pl.BlockSpec((d, d), fixed),
                  pl.BlockSpec((tm, d), row), pl.BlockSpec((1, d), fixed)],
        out_specs=(pl.BlockSpec((tm, d), row), pl.BlockSpec((tm, d), row)),
        compiler_params=_params("parallel"),
        name="out_proj",
    )(merged, w, x, gain.reshape(1, d))


def _ffn_up_kernel(h_ref, w_ref, o_ref):
    z = jnp.maximum(_dot(h_ref[...], w_ref[...]), 0.0)
    o_ref[...] = (z * z).astype(o_ref.dtype)


def _ffn_up(h, w, tm, tn):
    m, d = h.shape
    n = w.shape[1]
    return pl.pallas_call(
        _ffn_up_kernel,
        out_shape=jax.ShapeDtypeStruct((m, n), BF16),
        grid=(m // tm, n // tn),
        in_specs=[pl.BlockSpec((tm, d), lambda i, j: (i, 0)),
                  pl.BlockSpec((d, tn), lambda i, j: (0, j))],
        out_specs=pl.BlockSpec((tm, tn), lambda i, j: (i, j)),
        compiler_params=_params("parallel", "parallel"),
        name="ffn_up",
    )(h, w)


def _ffn_down_kernel(a_ref, w_ref, x_ref, o_ref):
    k = pl.program_id(2)

    @pl.when(k == 0)
    def _():
        o_ref[...] = x_ref[...] + _dot(a_ref[...], w_ref[...])

    @pl.when(k != 0)
    def _():
        o_ref[...] += _dot(a_ref[...], w_ref[...])


def _ffn_down(a, w, x1, tm, tn, tk):
    m, kdim = a.shape
    n = w.shape[1]
    return pl.pallas_call(
        _ffn_down_kernel,
        out_shape=jax.ShapeDtypeStruct((m, n), F32),
        grid=(m // tm, n // tn, kdim // tk),
        in_specs=[pl.BlockSpec((tm, tk), lambda i, j, k: (i, k)),
                  pl.BlockSpec((tk, tn), lambda i, j, k: (k, j)),
                  pl.BlockSpec((tm, tn), lambda i, j, k: (i, j))],
        out_specs=pl.BlockSpec((tm, tn), lambda i, j, k: (i, j)),
        compiler_params=_params("parallel", "parallel", "arbitrary"),
        name="ffn_down",
    )(a, w, x1)


def _rope_tables(s):
    inv = ROPE_THETA ** (-jnp.arange(ROPE_PAIRS, dtype=F32) / ROPE_PAIRS)
    inv_lane = jnp.tile(inv, HEAD_DIM // ROPE_PAIRS)
    lane_grp = jnp.arange(HEAD_DIM) // ROPE_PAIRS
    is_row = (lane_grp % 2) == 0
    sign = jnp.where(lane_grp < 2, -1.0, 1.0).astype(F32)

    def tables(n):
        ang = jnp.arange(n, dtype=F32)[:, None] * inv_lane
        return jnp.cos(ang), jnp.sin(ang) * sign

    cr, sr = tables(s // GRID_W)
    cc, sc = tables(GRID_W)
    return (jnp.where(is_row, cr, 0.0), jnp.where(is_row, sr, 0.0),
            jnp.where(is_row, 0.0, cc), jnp.where(is_row, 0.0, sc))


def kernel(x, mem, norm_mix, w_in, q_norm, k_norm, sgu_norm, w_spatial, b_spatial, mem_norm,
           w_mem_kv, mq_norm, mk_norm, w_attn_o, w_gmlp_o, w_mem_o, w_out, norm_ffn,
           w_ffn_up, w_ffn_down):
    b, s, d = x.shape
    assert b == 1 and d == D_MODEL and norm_mix.shape[0] == 1
    xs = x[0]
    w_in0 = w_in[0]
    tables = _rope_tables(s)
    perm = jnp.asarray(ROPE_PERM)
    gq = q_norm[0][perm] * Q_SCALE
    gk = k_norm[0][perm]
    bias_full = jnp.repeat(b_spatial[0].T, HEAD_DIM, axis=1)

    h, kt, v = _kvproj(xs, norm_mix[0], w_in0, gk, tables, TM_PROJ)
    q = _qproj(h, w_in0, gq, tables, TM_PROJ, ATTN_TQ)
    gm, w_gmlp_o_b = _gmlp(h, w_in0, sgu_norm[0], w_spatial[0], bias_full, w_gmlp_o[0], TM_PROJ)
    km_t, vm = _memkv(mem[0], mem_norm[0], w_mem_kv[0], mk_norm[0])
    ym, w_mem_o_b = _memattn(h, w_in0, mq_norm[0] * Q_SCALE, km_t, vm, w_mem_o[0], TM_PROJ)

    logit_bound = (1.02 * LOG2E * math.sqrt(HEAD_DIM)
                   * jnp.max(jnp.abs(q_norm[0])) * jnp.max(jnp.abs(k_norm[0])))
    attn, w_attn_o_b, w_out_b, w_up_b, w_down_b = lax.cond(
        logit_bound <= MAX_UNSHIFTED_LOGIT,
        functools.partial(_attention, unshifted=True),
        functools.partial(_attention, unshifted=False),
        q, kt, v, w_attn_o[0], w_out[0], w_ffn_up[0], w_ffn_down[0])

    merged = _gated_merge(h, attn, gm, ym, w_in0, w_attn_o_b, w_gmlp_o_b, w_mem_o_b,
                          TM_MERGE, TN_MERGE)
    x1, h2 = _outproj(merged, w_out_b, xs, norm_ffn[0], TM_OUT)
    a = _ffn_up(h2, w_up_b, TM_UP, TN_UP)
    out = _ffn_down(a, w_down_b, x1, TM_DOWN, TN_DOWN, TK_DOWN)
    return out[None]
```

```python
import functools
import math

import jax
import jax.numpy as jnp
from jax import lax
from jax.experimental import pallas as pl
from jax.experimental.pallas import tpu as pltpu

D_MODEL = 2048
HEAD_DIM = 128
N_Q_HEADS = 8
N_KV_HEADS = 2
Q_PER_KV = N_Q_HEADS // N_KV_HEADS
GRID_W = 64
ROPE_THETA = 10000.0
ROPE_PAIRS = HEAD_DIM // 4
GMLP_GROUPS = 4
GMLP_WIDTH = GMLP_GROUPS * HEAD_DIM
CHUNK = 128
N_MEM_HEADS = 4
MEM_WIDTH = N_MEM_HEADS * HEAD_DIM
D_FF = 4 * D_MODEL
EPS = 1e-6
ATTN_Q_W = N_Q_HEADS * HEAD_DIM
ATTN_KV_W = N_KV_HEADS * HEAD_DIM
IN_COL_BLOCK = 512
IN_Q_BLK, IN_KV_BLK, IN_U_BLK, IN_V_BLK, IN_QM_BLK, IN_GATE_BLK = 0, 2, 3, 4, 5, 6

VMEM_LIMIT_BYTES = 56 * 1024 * 1024
LOG2E = math.log2(math.e)
Q_SCALE = HEAD_DIM ** -0.5 * LOG2E
KV_BLOCK = 256
ATTN_TQ = 256
MAX_UNSHIFTED_LOGIT = 64.0
ROPE_PERM = tuple(list(range(0, 32)) + list(range(64, 96)) + list(range(32, 64))
                  + list(range(96, 128)))

TM_PROJ = 512
TM_MERGE, TN_MERGE = 512, 512
TM_OUT = 512
TM_UP, TN_UP = 2048, 1024
TM_DOWN, TN_DOWN, TK_DOWN = 1024, 1024, 2048

F32 = jnp.float32
BF16 = jnp.bfloat16


def _params(*sem):
    return pltpu.CompilerParams(dimension_semantics=sem, vmem_limit_bytes=VMEM_LIMIT_BYTES)


def _dot(a, b):
    return jnp.dot(a, b, preferred_element_type=F32)


def _head_rms(x, gain):
    ms = jnp.mean(x * x, axis=-1, keepdims=True)
    return x * lax.rsqrt(ms + EPS) * gain


def _rope(y, cos, sin_signed):
    return y * cos + pltpu.roll(y, HEAD_DIM // 2, 1) * sin_signed


def _to_rope_order(w):
    grp = lax.broadcasted_iota(jnp.int32, w.shape, 1) // ROPE_PAIRS
    from_right = pltpu.roll(w, HEAD_DIM - ROPE_PAIRS, 1)
    from_left = pltpu.roll(w, ROPE_PAIRS, 1)
    return jnp.where(grp == 1, from_right, jnp.where(grp == 2, from_left, w))


def _rope_tile(row_ref, col_ref, i, tm):
    nrow = tm // GRID_W
    rows = row_ref[pl.ds(pl.multiple_of(i * nrow, nrow), nrow), :]
    col = col_ref[...]
    return jnp.concatenate([rows[r:r + 1, :] + col for r in range(nrow)], axis=0)


def _lagged_step(i, project, finish, buf_a, buf_b):
    @pl.when(i % 2 == 0)
    def _():
        project(buf_a)
        finish(buf_b)

    @pl.when(i % 2 == 1)
    def _():
        project(buf_b)
        finish(buf_a)


def _lag_index_maps(n_tiles):
    return (lambda i: jnp.minimum(i, n_tiles - 1)), (lambda i: jnp.maximum(i - 1, 0))


def _cast_side_job(pairs):
    for src, dst in pairs:
        dst[...] = src[...].astype(dst.dtype)


def _side_specs(arrays, nsteps, index):
    in_specs, out_specs, out_shapes = [], [], []
    for a in arrays:
        rows, cols = a.shape
        blk = (rows // nsteps, cols)
        in_specs.append(pl.BlockSpec(blk, index))
        out_specs.append(pl.BlockSpec(blk, index))
        out_shapes.append(jax.ShapeDtypeStruct(a.shape, BF16))
    return in_specs, out_specs, out_shapes


def _kvproj_kernel(x_ref, gx_ref, w_ref, gk_ref, crow_ref, srow_ref, ccol_ref, scol_ref,
                   h_ref, kt_ref, v_ref, wb_ref, acc_a, acc_b, *, tm):
    i = pl.program_id(0)

    @pl.when(i == 0)
    def _():
        for hd in range(N_KV_HEADS):
            sl = slice(hd * HEAD_DIM, (hd + 1) * HEAD_DIM)
            wb_ref[:, sl] = _to_rope_order(w_ref[:, sl]).astype(BF16)
        wb_ref[:, ATTN_KV_W:] = w_ref[:, ATTN_KV_W:].astype(BF16)
        acc_b[...] = jnp.zeros_like(acc_b)

    def project(acc_ref):
        h = _head_rms(x_ref[...], gx_ref[...]).astype(BF16)
        h_ref[...] = h
        acc_ref[...] = _dot(h, wb_ref[...])

    def finish(acc_ref):
        prev = jnp.maximum(i - 1, 0)
        cos = _rope_tile(crow_ref, ccol_ref, prev, tm)
        sin = _rope_tile(srow_ref, scol_ref, prev, tm)
        g = gk_ref[...]
        for hd in range(N_KV_HEADS):
            sl = slice(hd * HEAD_DIM, (hd + 1) * HEAD_DIM)
            y = _rope(_head_rms(acc_ref[:, sl], g), cos, sin)
            for c in range(tm // KV_BLOCK):
                blk = y[c * KV_BLOCK:(c + 1) * KV_BLOCK, :]
                kt_ref[c, sl, :] = jnp.transpose(blk).astype(kt_ref.dtype)
        v_ref[...] = acc_ref[:, ATTN_KV_W:].astype(v_ref.dtype)

    _lagged_step(i, project, finish, acc_a, acc_b)


def _kvproj(x, gx, w_in, gk, tables, tm):
    m, d = x.shape
    n_tiles = m // tm
    cur, prev = _lag_index_maps(n_tiles)
    fixed = lambda i: (0, 0)
    n = 2 * ATTN_KV_W
    tab_specs = [pl.BlockSpec(t.shape, fixed) for t in tables]
    return pl.pallas_call(
        functools.partial(_kvproj_kernel, tm=tm),
        out_shape=(jax.ShapeDtypeStruct((m, d), BF16),
                   jax.ShapeDtypeStruct((m // KV_BLOCK, ATTN_KV_W, KV_BLOCK), BF16),
                   jax.ShapeDtypeStruct((m, ATTN_KV_W), BF16)),
        grid=(n_tiles + 1,),
        in_specs=[pl.BlockSpec((tm, d), lambda i: (cur(i), 0)), pl.BlockSpec((1, d), fixed),
                  pl.BlockSpec((d, n), lambda i: (0, IN_KV_BLK)),
                  pl.BlockSpec((1, HEAD_DIM), fixed)] + tab_specs,
        out_specs=(pl.BlockSpec((tm, d), lambda i: (cur(i), 0)),
                   pl.BlockSpec((tm // KV_BLOCK, ATTN_KV_W, KV_BLOCK),
                                lambda i: (prev(i), 0, 0)),
                   pl.BlockSpec((tm, ATTN_KV_W), lambda i: (prev(i), 0))),
        scratch_shapes=[pltpu.VMEM((d, n), BF16), pltpu.VMEM((tm, n), F32),
                        pltpu.VMEM((tm, n), F32)],
        compiler_params=_params("arbitrary"),
        name="kv_proj",
    )(x, gx.reshape(1, d), w_in, gk.reshape(1, HEAD_DIM), *tables)


def _qproj_kernel(h_ref, w_ref, g_ref, crow_ref, srow_ref, ccol_ref, scol_ref, o_ref, wb_ref,
                  acc_a, acc_b, *, tm, tq):
    i = pl.program_id(0)

    @pl.when(i == 0)
    def _():
        for hd in range(N_Q_HEADS):
            sl = slice(hd * HEAD_DIM, (hd + 1) * HEAD_DIM)
            wb_ref[:, sl] = _to_rope_order(w_ref[:, sl]).astype(BF16)
        acc_b[...] = jnp.zeros_like(acc_b)

    def project(acc_ref):
        acc_ref[...] = _dot(h_ref[...], wb_ref[...])

    def finish(acc_ref):
        prev = jnp.maximum(i - 1, 0)
        cos = _rope_tile(crow_ref, ccol_ref, prev, tm)
        sin = _rope_tile(srow_ref, scol_ref, prev, tm)
        g = g_ref[...]
        for hd in range(N_Q_HEADS):
            grp, r = divmod(hd, Q_PER_KV)
            x = acc_ref[:, hd * HEAD_DIM:(hd + 1) * HEAD_DIM]
            y = _rope(_head_rms(x, g), cos, sin).astype(o_ref.dtype)
            for t in range(tm // tq):
                o_ref[t, grp, r * tq:(r + 1) * tq, :] = y[t * tq:(t + 1) * tq]

    _lagged_step(i, project, finish, acc_a, acc_b)


def _qproj(h, w_in, gain, tables, tm, tq):
    m, d = h.shape
    n_tiles = m // tm
    cur, prev = _lag_index_maps(n_tiles)
    fixed = lambda i: (0, 0)
    tab_specs = [pl.BlockSpec(t.shape, fixed) for t in tables]
    oshape = (m // tq, N_KV_HEADS, Q_PER_KV * tq, HEAD_DIM)
    return pl.pallas_call(
        functools.partial(_qproj_kernel, tm=tm, tq=tq),
        out_shape=jax.ShapeDtypeStruct(oshape, BF16),
        grid=(n_tiles + 1,),
        in_specs=[pl.BlockSpec((tm, d), lambda i: (cur(i), 0)),
                  pl.BlockSpec((d, ATTN_Q_W), lambda i: (0, IN_Q_BLK)),
                  pl.BlockSpec((1, HEAD_DIM), fixed)] + tab_specs,
        out_specs=pl.BlockSpec((tm // tq,) + oshape[1:], lambda i: (prev(i), 0, 0, 0)),
        scratch_shapes=[pltpu.VMEM((d, ATTN_Q_W), BF16), pltpu.VMEM((tm, ATTN_Q_W), F32),
                        pltpu.VMEM((tm, ATTN_Q_W), F32)],
        compiler_params=_params("arbitrary"),
        name="q_proj",
    )(h, w_in, gain.reshape(1, HEAD_DIM), *tables)


def _gmlp_kernel(h_ref, wu_ref, wv_ref, g_ref, ws_ref, b_ref, wo_ref, o_ref, wob_ref, wb_ref,
                 acc_a, acc_b, *, tm):
    i = pl.program_id(0)

    @pl.when(i == 0)
    def _():
        wb_ref[:, :GMLP_WIDTH] = wu_ref[...].astype(BF16)
        wb_ref[:, GMLP_WIDTH:] = wv_ref[...].astype(BF16)
        acc_b[...] = jnp.zeros_like(acc_b)

    _cast_side_job([(wo_ref, wob_ref)])

    def project(acc_ref):
        acc_ref[...] = _dot(h_ref[...], wb_ref[...])

    def finish(acc_ref):
        z = jax.nn.gelu(acc_ref[...])
        u = z[:, :GMLP_WIDTH]
        vn = _head_rms(z[:, GMLP_WIDTH:], g_ref[...]).astype(BF16)
        bias = b_ref[...]
        for c in range(tm // CHUNK):
            rows = slice(c * CHUNK, (c + 1) * CHUNK)
            for grp in range(GMLP_GROUPS):
                cols = slice(grp * HEAD_DIM, (grp + 1) * HEAD_DIM)
                mixed = _dot(ws_ref[grp].astype(BF16), vn[rows, cols]) + bias[:, cols]
                o_ref[rows, cols] = (u[rows, cols] * mixed).astype(o_ref.dtype)

    _lagged_step(i, project, finish, acc_a, acc_b)


def _gmlp(h, w_in, sgu_gain, w_spatial, bias_full, w_o, tm):
    m, d = h.shape
    n_tiles = m // tm
    cur, prev = _lag_index_maps(n_tiles)
    fixed = lambda i: (0, 0)
    side_in, side_out, side_shapes = _side_specs([w_o], n_tiles, lambda i: (cur(i), 0))
    return pl.pallas_call(
        functools.partial(_gmlp_kernel, tm=tm),
        out_shape=[jax.ShapeDtypeStruct((m, GMLP_WIDTH), BF16)] + side_shapes,
        grid=(n_tiles + 1,),
        in_specs=[pl.BlockSpec((tm, d), lambda i: (cur(i), 0)),
                  pl.BlockSpec((d, GMLP_WIDTH), lambda i: (0, IN_U_BLK)),
                  pl.BlockSpec((d, GMLP_WIDTH), lambda i: (0, IN_V_BLK)),
                  pl.BlockSpec((1, GMLP_WIDTH), fixed),
                  pl.BlockSpec((GMLP_GROUPS, CHUNK, CHUNK), lambda i: (0, 0, 0)),
                  pl.BlockSpec((CHUNK, GMLP_WIDTH), fixed)] + side_in,
        out_specs=[pl.BlockSpec((tm, GMLP_WIDTH), lambda i: (prev(i), 0))] + side_out,
        scratch_shapes=[pltpu.VMEM((d, 2 * GMLP_WIDTH), BF16),
                        pltpu.VMEM((tm, 2 * GMLP_WIDTH), F32),
                        pltpu.VMEM((tm, 2 * GMLP_WIDTH), F32)],
        compiler_params=_params("arbitrary"),
        name="gmlp",
    )(h, w_in, w_in, sgu_gain.reshape(1, GMLP_WIDTH), w_spatial, bias_full, w_o)


def _memkv_kernel(mem_ref, g_ref, w_ref, kg_ref, kt_ref, v_ref):
    mem_n = _head_rms(mem_ref[...], g_ref[...]).astype(BF16)
    kv = _dot(mem_n, w_ref[...].astype(BF16))
    kg = kg_ref[...]
    for hd in range(N_MEM_HEADS):
        sl = slice(hd * HEAD_DIM, (hd + 1) * HEAD_DIM)
        kt_ref[sl, :] = jnp.transpose(_head_rms(kv[:, sl], kg)).astype(kt_ref.dtype)
    v_ref[...] = kv[:, MEM_WIDTH:].astype(v_ref.dtype)


def _memkv(mem, mem_gain, w, mk_gain):
    mt, d = mem.shape
    return pl.pallas_call(
        _memkv_kernel,
        out_shape=(jax.ShapeDtypeStruct((MEM_WIDTH, mt), BF16),
                   jax.ShapeDtypeStruct((mt, MEM_WIDTH), BF16)),
        compiler_params=pltpu.CompilerParams(vmem_limit_bytes=VMEM_LIMIT_BYTES),
        name="mem_kv",
    )(mem, mem_gain.reshape(1, d), w, mk_gain.reshape(1, HEAD_DIM))


def _memattn_kernel(h_ref, w_ref, g_ref, kt_ref, v_ref, wo_ref, o_ref, wob_ref, wb_ref,
                    acc_a, acc_b):
    i = pl.program_id(0)

    @pl.when(i == 0)
    def _():
        wb_ref[...] = w_ref[...].astype(BF16)
        acc_b[...] = jnp.zeros_like(acc_b)

    _cast_side_job([(wo_ref, wob_ref)])

    def project(acc_ref):
        acc_ref[...] = _dot(h_ref[...], wb_ref[...])

    def finish(acc_ref):
        g = g_ref[...]
        for hd in range(N_MEM_HEADS):
            sl = slice(hd * HEAD_DIM, (hd + 1) * HEAD_DIM)
            qn = _head_rms(acc_ref[:, sl], g).astype(BF16)
            s = _dot(qn, kt_ref[sl, :])
            p = jnp.exp2(s - jnp.max(s, axis=-1, keepdims=True))
            l = jnp.sum(p, axis=-1, keepdims=True)
            o = _dot(p.astype(BF16), v_ref[:, sl])
            o_ref[:, sl] = (o / l).astype(o_ref.dtype)

    _lagged_step(i, project, finish, acc_a, acc_b)


def _memattn(h, w_in, mq_gain, km_t, vm, w_o, tm):
    m, d = h.shape
    mt = vm.shape[0]
    n_tiles = m // tm
    cur, prev = _lag_index_maps(n_tiles)
    fixed = lambda i: (0, 0)
    side_in, side_out, side_shapes = _side_specs([w_o], n_tiles, lambda i: (cur(i), 0))
    return pl.pallas_call(
        _memattn_kernel,
        out_shape=[jax.ShapeDtypeStruct((m, MEM_WIDTH), BF16)] + side_shapes,
        grid=(n_tiles + 1,),
        in_specs=[pl.BlockSpec((tm, d), lambda i: (cur(i), 0)),
                  pl.BlockSpec((d, MEM_WIDTH), lambda i: (0, IN_QM_BLK)),
                  pl.BlockSpec((1, HEAD_DIM), fixed),
                  pl.BlockSpec((MEM_WIDTH, mt), fixed),
                  pl.BlockSpec((mt, MEM_WIDTH), fixed)] + side_in,
        out_specs=[pl.BlockSpec((tm, MEM_WIDTH), lambda i: (prev(i), 0))] + side_out,
        scratch_shapes=[pltpu.VMEM((d, MEM_WIDTH), BF16), pltpu.VMEM((tm, MEM_WIDTH), F32),
                        pltpu.VMEM((tm, MEM_WIDTH), F32)],
        compiler_params=_params("arbitrary"),
        name="mem_attn",
    )(h, w_in, mq_gain.reshape(1, HEAD_DIM), km_t, vm, w_o)


def _store_heads(o_ref, out, tq):
    for r in range(Q_PER_KV):
        o_ref[:, r * HEAD_DIM:(r + 1) * HEAD_DIM] = out[r * tq:(r + 1) * tq].astype(o_ref.dtype)


def _attn_unshifted_kernel(q_ref, kt_ref, v_ref, *rest, tq, nk, nside):
    side = rest[:nside]
    o_ref = rest[nside]
    side_out = rest[nside + 1:2 * nside + 1]
    l_ref, acc_ref = rest[2 * nside + 1:]
    _cast_side_job(zip(side, side_out))
    l_ref[...] = jnp.zeros_like(l_ref)
    acc_ref[...] = jnp.zeros_like(acc_ref)

    def body(j, carry):
        off = pl.multiple_of(j * KV_BLOCK, KV_BLOCK)
        p = jnp.exp2(_dot(q_ref[0, 0], kt_ref[j]))
        l_ref[...] += p[:, :HEAD_DIM] + p[:, HEAD_DIM:]
        acc_ref[...] += _dot(p.astype(BF16), v_ref[pl.ds(off, KV_BLOCK), :])
        return carry

    lax.fori_loop(0, nk, body, 0, unroll=True)
    l = jnp.sum(l_ref[...], axis=-1, keepdims=True)
    _store_heads(o_ref, acc_ref[...] / l, tq)


def _attn_online_kernel(q_ref, kt_ref, v_ref, *rest, tq, nk, nside):
    side = rest[:nside]
    o_ref = rest[nside]
    side_out = rest[nside + 1:2 * nside + 1]
    _cast_side_job(zip(side, side_out))
    q = q_ref[0, 0]
    rows = Q_PER_KV * tq

    def body(j, carry):
        m, l, acc = carry
        off = pl.multiple_of(j * KV_BLOCK, KV_BLOCK)
        s = _dot(q, kt_ref[j])
        m_new = jnp.maximum(m, jnp.max(s, axis=-1, keepdims=True))
        alpha = jnp.exp2(m - m_new)
        p = jnp.exp2(s - m_new)
        l = alpha * l + jnp.sum(p, axis=-1, keepdims=True)
        acc = alpha * acc + _dot(p.astype(BF16), v_ref[pl.ds(off, KV_BLOCK), :])
        return m_new, l, acc

    m0 = jnp.full((rows, 1), -jnp.inf, F32)
    l0 = jnp.zeros((rows, 1), F32)
    acc0 = jnp.zeros((rows, HEAD_DIM), F32)
    _, l, acc = lax.fori_loop(0, nk, body, (m0, l0, acc0))
    _store_heads(o_ref, acc / l, tq)


def _attention(q, kt, v, *side_weights, unshifted):
    nq, _, rows, _ = q.shape
    tq = rows // Q_PER_KV
    s = v.shape[0]
    nk = s // KV_BLOCK
    gw = Q_PER_KV * HEAD_DIM
    nside = len(side_weights)
    side_in, side_out, side_shapes = _side_specs(side_weights, N_KV_HEADS * nq,
                                                 lambda g, i: (g * nq + i, 0))
    if unshifted:
        body = functools.partial(_attn_unshifted_kernel, tq=tq, nk=nk, nside=nside)
        scratch = [pltpu.VMEM((rows, HEAD_DIM), F32), pltpu.VMEM((rows, HEAD_DIM), F32)]
        name = "attn_unshifted"
    else:
        body = functools.partial(_attn_online_kernel, tq=tq, nk=nk, nside=nside)
        scratch = []
        name = "attn_online"
    return pl.pallas_call(
        body,
        out_shape=[jax.ShapeDtypeStruct((s, ATTN_Q_W), BF16)] + side_shapes,
        grid=(N_KV_HEADS, nq),
        in_specs=[pl.BlockSpec((1, 1, rows, HEAD_DIM), lambda g, i: (i, g, 0, 0)),
                  pl.BlockSpec((nk, HEAD_DIM, KV_BLOCK), lambda g, i: (0, g, 0)),
                  pl.BlockSpec((s, HEAD_DIM), lambda g, i: (0, g))] + side_in,
        out_specs=[pl.BlockSpec((tq, gw), lambda g, i: (i, g))] + side_out,
        scratch_shapes=scratch,
        compiler_params=_params("arbitrary", "arbitrary"),
        name=name,
    )(q, kt, v, *side_weights)


def _gated_merge_kernel(h_ref, a_ref, gm_ref, ym_ref, wga_ref, wgg_ref, wgm_ref,
                        wa_ref, wg_ref, wm_ref, o_ref, wb_ref):
    @pl.when(pl.program_id(1) == 0)
    def _():
        for b, w_ref in enumerate((wga_ref, wgg_ref, wgm_ref)):
            wb_ref[b] = w_ref[...].astype(BF16)

    h = h_ref[...]
    y = None
    for b, (y_ref, w_ref) in enumerate(((a_ref, wa_ref), (gm_ref, wg_ref), (ym_ref, wm_ref))):
        gate = 1.0 / (1.0 + jnp.exp(-_dot(h, wb_ref[b])))
        term = gate * _dot(y_ref[...], w_ref[...])
        y = term if y is None else y + term
    o_ref[...] = y.astype(o_ref.dtype)


def _gated_merge(h, attn, gm, ym, w_in, w_a, w_g, w_m, tm, tn):
    m, d = h.shape
    n = w_a.shape[1]
    nb = n // tn
    first = IN_GATE_BLK * IN_COL_BLOCK // tn
    row = lambda j, i: (i, 0)
    col = lambda j, i: (0, j)
    return pl.pallas_call(
        _gated_merge_kernel,
        out_shape=jax.ShapeDtypeStruct((m, n), BF16),
        grid=(nb, m // tm),
        in_specs=[pl.BlockSpec((tm, d), row),
                  pl.BlockSpec((tm, attn.shape[1]), row),
                  pl.BlockSpec((tm, gm.shape[1]), row),
                  pl.BlockSpec((tm, ym.shape[1]), row),
                  pl.BlockSpec((d, tn), lambda j, i: (0, first + j)),
                  pl.BlockSpec((d, tn), lambda j, i: (0, first + nb + j)),
                  pl.BlockSpec((d, tn), lambda j, i: (0, first + 2 * nb + j)),
                  pl.BlockSpec((w_a.shape[0], tn), col),
                  pl.BlockSpec((w_g.shape[0], tn), col),
                  pl.BlockSpec((w_m.shape[0], tn), col)],
        out_specs=pl.BlockSpec((tm, tn), lambda j, i: (i, j)),
        scratch_shapes=[pltpu.VMEM((3, d, tn), BF16)],
        compiler_params=_params("arbitrary", "arbitrary"),
        name="gated_merge",
    )(h, attn, gm, ym, w_in, w_in, w_in, w_a, w_g, w_m)


def _outproj_kernel(mg_ref, w_ref, x_ref, g_ref, x1_ref, h2_ref):
    x1 = x_ref[...] + _dot(mg_ref[...], w_ref[...])
    x1_ref[...] = x1
    h2_ref[...] = _head_rms(x1, g_ref[...]).astype(h2_ref.dtype)


def _outproj(merged, w, x, gain, tm):
    m, d = x.shape
    row = lambda i: (i, 0)
    fixed = lambda i: (0, 0)
    return pl.pallas_call(
        _outproj_kernel,
        out_shape=(jax.ShapeDtypeStruct((m, d), F32), jax.ShapeDtypeStruct((m, d), BF16)),
        grid=(m // tm,),
        in_specs=[pl.BlockSpec((tm, d), row), pl.BlockSpec((d, d), fixed),
                  pl.BlockSpec((tm, d), row), pl.BlockSpec((1, d), fixed)],
        out_specs=(pl.BlockSpec((tm, d), row), pl.BlockSpec((tm, d), row)),
        compiler_params=_params("parallel"),
        name="out_proj",
    )(merged, w, x, gain.reshape(1, d))


def _ffn_up_kernel(h_ref, w_ref, o_ref):
    z = jnp.maximum(_dot(h_ref[...], w_ref[...]), 0.0)
    o_ref[...] = (z * z).astype(o_ref.dtype)


def _ffn_up(h, w, tm, tn):
    m, d = h.shape
    n = w.shape[1]
    return pl.pallas_call(
        _ffn_up_kernel,
        out_shape=jax.ShapeDtypeStruct((m, n), BF16),
        grid=(m // tm, n // tn),
        in_specs=[pl.BlockSpec((tm, d), lambda i, j: (i, 0)),
                  pl.BlockSpec((d, tn), lambda i, j: (0, j))],
        out_specs=pl.BlockSpec((tm, tn), lambda i, j: (i, j)),
        compiler_params=_params("parallel", "parallel"),
        name="ffn_up",
    )(h, w)


def _ffn_down_kernel(a_ref, w_ref, x_ref, o_ref):
    k = pl.program_id(2)

    @pl.when(k == 0)
    def _():
        o_ref[...] = x_ref[...] + _dot(a_ref[...], w_ref[...])

    @pl.when(k != 0)
    def _():
        o_ref[...] += _dot(a_ref[...], w_ref[...])


def _ffn_down(a, w, x1, tm, tn, tk):
    m, kdim = a.shape
    n = w.shape[1]
    return pl.pallas_call(
        _ffn_down_kernel,
        out_shape=jax.ShapeDtypeStruct((m, n), F32),
        grid=(m // tm, n // tn, kdim // tk),
        in_specs=[pl.BlockSpec((tm, tk), lambda i, j, k: (i, k)),
                  pl.BlockSpec((tk, tn), lambda i, j, k: (k, j)),
                  pl.BlockSpec((tm, tn), lambda i, j, k: (i, j))],
        out_specs=pl.BlockSpec((tm, tn), lambda i, j, k: (i, j)),
        compiler_params=_params("parallel", "parallel", "arbitrary"),
        name="ffn_down",
    )(a, w, x1)


def _rope_tables(s):
    inv = ROPE_THETA ** (-jnp.arange(ROPE_PAIRS, dtype=F32) / ROPE_PAIRS)
    inv_lane = jnp.tile(inv, HEAD_DIM // ROPE_PAIRS)
    lane_grp = jnp.arange(HEAD_DIM) // ROPE_PAIRS
    is_row = (lane_grp % 2) == 0
    sign = jnp.where(lane_grp < 2, -1.0, 1.0).astype(F32)

    def tables(n):
        ang = jnp.arange(n, dtype=F32)[:, None] * inv_lane
        return jnp.cos(ang), jnp.sin(ang) * sign

    cr, sr = tables(s // GRID_W)
    cc, sc = tables(GRID_W)
    return (jnp.where(is_row, cr, 0.0), jnp.where(is_row, sr, 0.0),
            jnp.where(is_row, 0.0, cc), jnp.where(is_row, 0.0, sc))


def kernel(x, mem, norm_mix, w_in, q_norm, k_norm, sgu_norm, w_spatial, b_spatial, mem_norm,
           w_mem_kv, mq_norm, mk_norm, w_attn_o, w_gmlp_o, w_mem_o, w_out, norm_ffn,
           w_ffn_up, w_ffn_down):
    b, s, d = x.shape
    assert b == 1 and d == D_MODEL and norm_mix.shape[0] == 1
    xs = x[0]
    w_in0 = w_in[0]
    tables = _rope_tables(s)
    perm = jnp.asarray(ROPE_PERM)
    gq = q_norm[0][perm] * Q_SCALE
    gk = k_norm[0][perm]
    bias_full = jnp.repeat(b_spatial[0].T, HEAD_DIM, axis=1)

    h, kt, v = _kvproj(xs, norm_mix[0], w_in0, gk, tables, TM_PROJ)
    q = _qproj(h, w_in0, gq, tables, TM_PROJ, ATTN_TQ)
    gm, w_gmlp_o_b = _gmlp(h, w_in0, sgu_norm[0], w_spatial[0], bias_full, w_gmlp_o[0], TM_PROJ)
    km_t, vm = _memkv(mem[0], mem_norm[0], w_mem_kv[0], mk_norm[0])
    ym, w_mem_o_b = _memattn(h, w_in0, mq_norm[0] * Q_SCALE, km_t, vm, w_mem_o[0], TM_PROJ)

    logit_bound = (1.02 * LOG2E * math.sqrt(HEAD_DIM)
                   * jnp.max(jnp.abs(q_norm[0])) * jnp.max(jnp.abs(k_norm[0])))
    attn, w_attn_o_b, w_out_b, w_up_b, w_down_b = lax.cond(
        logit_bound <= MAX_UNSHIFTED_LOGIT,
        functools.partial(_attention, unshifted=True),
        functools.partial(_attention, unshifted=False),
        q, kt, v, w_attn_o[0], w_out[0], w_ffn_up[0], w_ffn_down[0])

    merged = _gated_merge(h, attn, gm, ym, w_in0, w_attn_o_b, w_gmlp_o_b, w_mem_o_b,
                          TM_MERGE, TN_MERGE)
    x1, h2 = _outproj(merged, w_out_b, xs, norm_ffn[0], TM_OUT)
    a = _ffn_up(h2, w_up_b, TM_UP, TN_UP)
    out = _ffn_down(a, w_down_b, x1, TM_DOWN, TN_DOWN, TK_DOWN)
    return out[None]
```

```python
import functools
import math

import jax
import jax.numpy as jnp
from jax import lax
from jax.experimental import pallas as pl
from jax.experimental.pallas import tpu as pltpu

D_MODEL = 2048
HEAD_DIM = 128
N_Q_HEADS = 8
N_KV_HEADS = 2
Q_PER_KV = N_Q_HEADS // N_KV_HEADS
GRID_W = 64
ROPE_THETA = 10000.0
ROPE_PAIRS = HEAD_DIM // 4
GMLP_GROUPS = 4
GMLP_WIDTH = GMLP_GROUPS * HEAD_DIM
CHUNK = 128
N_MEM_HEADS = 4
MEM_WIDTH = N_MEM_HEADS * HEAD_DIM
D_FF = 4 * D_MODEL
EPS = 1e-6
ATTN_Q_W = N_Q_HEADS * HEAD_DIM
ATTN_KV_W = N_KV_HEADS * HEAD_DIM
IN_COL_BLOCK = 512
IN_Q_BLK, IN_KV_BLK, IN_U_BLK, IN_V_BLK, IN_QM_BLK, IN_GATE_BLK = 0, 2, 3, 4, 5, 6

VMEM_LIMIT_BYTES = 56 * 1024 * 1024
LOG2E = math.log2(math.e)
Q_SCALE = HEAD_DIM ** -0.5 * LOG2E
KV_BLOCK = 256
ATTN_TQ = 256
MAX_UNSHIFTED_LOGIT = 64.0
ROPE_PERM = tuple(list(range(0, 32)) + list(range(64, 96)) + list(range(32, 64))
                  + list(range(96, 128)))

TM_PROJ = 512
TM_MERGE, TN_MERGE = 512, 512
TM_OUT = 512
TM_UP, TN_UP = 2048, 1024
TM_DOWN, TN_DOWN, TK_DOWN = 1024, 1024, 2048

F32 = jnp.float32
BF16 = jnp.bfloat16


def _params(*sem):
    return pltpu.CompilerParams(dimension_semantics=sem, vmem_limit_bytes=VMEM_LIMIT_BYTES)


def _dot(a, b):
    return jnp.dot(a, b, preferred_element_type=F32)


def _head_rms(x, gain):
    ms = jnp.mean(x * x, axis=-1, keepdims=True)
    return x * lax.rsqrt(ms + EPS) * gain


def _rope(y, cos, sin_signed):
    return y * cos + pltpu.roll(y, HEAD_DIM // 2, 1) * sin_signed


def _to_rope_order(w):
    grp = lax.broadcasted_iota(jnp.int32, w.shape, 1) // ROPE_PAIRS
    from_right = pltpu.roll(w, HEAD_DIM - ROPE_PAIRS, 1)
    from_left = pltpu.roll(w, ROPE_PAIRS, 1)
    return jnp.where(grp == 1, from_right, jnp.where(grp == 2, from_left, w))


def _rope_tile(row_ref, col_ref, i, tm):
    nrow = tm // GRID_W
    rows = row_ref[pl.ds(pl.multiple_of(i * nrow, nrow), nrow), :]
    col = col_ref[...]
    return jnp.concatenate([rows[r:r + 1, :] + col for r in range(nrow)], axis=0)


def _lagged_step(i, project, finish, buf_a, buf_b):
    @pl.when(i % 2 == 0)
    def _():
        project(buf_a)
        finish(buf_b)

    @pl.when(i % 2 == 1)
    def _():
        project(buf_b)
        finish(buf_a)


def _lag_index_maps(n_tiles):
    return (lambda i: jnp.minimum(i, n_tiles - 1)), (lambda i: jnp.maximum(i - 1, 0))


def _cast_side_job(pairs):
    for src, dst in pairs:
        dst[...] = src[...].astype(dst.dtype)


def _side_specs(arrays, nsteps, index):
    in_specs, out_specs, out_shapes = [], [], []
    for a in arrays:
        rows, cols = a.shape
        blk = (rows // nsteps, cols)
        in_specs.append(pl.BlockSpec(blk, index))
        out_specs.append(pl.BlockSpec(blk, index))
        out_shapes.append(jax.ShapeDtypeStruct(a.shape, BF16))
    return in_specs, out_specs, out_shapes


def _kvproj_kernel(x_ref, gx_ref, w_ref, gk_ref, crow_ref, srow_ref, ccol_ref, scol_ref,
                   h_ref, kt_ref, v_ref, wb_ref, acc_a, acc_b, *, tm):
    i = pl.program_id(0)

    @pl.when(i == 0)
    def _():
        for hd in range(N_KV_HEADS):
            sl = slice(hd * HEAD_DIM, (hd + 1) * HEAD_DIM)
            wb_ref[:, sl] = _to_rope_order(w_ref[:, sl]).astype(BF16)
        wb_ref[:, ATTN_KV_W:] = w_ref[:, ATTN_KV_W:].astype(BF16)
        acc_b[...] = jnp.zeros_like(acc_b)

    def project(acc_ref):
        h = _head_rms(x_ref[...], gx_ref[...]).astype(BF16)
        h_ref[...] = h
        acc_ref[...] = _dot(h, wb_ref[...])

    def finish(acc_ref):
        prev = jnp.maximum(i - 1, 0)
        cos = _rope_tile(crow_ref, ccol_ref, prev, tm)
        sin = _rope_tile(srow_ref, scol_ref, prev, tm)
        g = gk_ref[...]
        for hd in range(N_KV_HEADS):
            sl = slice(hd * HEAD_DIM, (hd + 1) * HEAD_DIM)
            y = _rope(_head_rms(acc_ref[:, sl], g), cos, sin)
            for c in range(tm // KV_BLOCK):
                blk = y[c * KV_BLOCK:(c + 1) * KV_BLOCK, :]
                kt_ref[c, sl, :] = jnp.transpose(blk).astype(kt_ref.dtype)
        v_ref[...] = acc_ref[:, ATTN_KV_W:].astype(v_ref.dtype)

    _lagged_step(i, project, finish, acc_a, acc_b)


def _kvproj(x, gx, w_in, gk, tables, tm):
    m, d = x.shape
    n_tiles = m // tm
    cur, prev = _lag_index_maps(n_tiles)
    fixed = lambda i: (0, 0)
    n = 2 * ATTN_KV_W
    tab_specs = [pl.BlockSpec(t.shape, fixed) for t in tables]
    return pl.pallas_call(
        functools.partial(_kvproj_kernel, tm=tm),
        out_shape=(jax.ShapeDtypeStruct((m, d), BF16),
                   jax.ShapeDtypeStruct((m // KV_BLOCK, ATTN_KV_W, KV_BLOCK), BF16),
                   jax.ShapeDtypeStruct((m, ATTN_KV_W), BF16)),
        grid=(n_tiles + 1,),
        in_specs=[pl.BlockSpec((tm, d), lambda i: (cur(i), 0)), pl.BlockSpec((1, d), fixed),
                  pl.BlockSpec((d, n), lambda i: (0, IN_KV_BLK)),
                  pl.BlockSpec((1, HEAD_DIM), fixed)] + tab_specs,
        out_specs=(pl.BlockSpec((tm, d), lambda i: (cur(i), 0)),
                   pl.BlockSpec((tm // KV_BLOCK, ATTN_KV_W, KV_BLOCK),
                                lambda i: (prev(i), 0, 0)),
                   pl.BlockSpec((tm, ATTN_KV_W), lambda i: (prev(i), 0))),
        scratch_shapes=[pltpu.VMEM((d, n), BF16), pltpu.VMEM((tm, n), F32),
                        pltpu.VMEM((tm, n), F32)],
        compiler_params=_params("arbitrary"),
        name="kv_proj",
    )(x, gx.reshape(1, d), w_in, gk.reshape(1, HEAD_DIM), *tables)


def _qproj_kernel(h_ref, w_ref, g_ref, crow_ref, srow_ref, ccol_ref, scol_ref, o_ref, wb_ref,
                  acc_a, acc_b, *, tm, tq):
    i = pl.program_id(0)

    @pl.when(i == 0)
    def _():
        for hd in range(N_Q_HEADS):
            sl = slice(hd * HEAD_DIM, (hd + 1) * HEAD_DIM)
            wb_ref[:, sl] = _to_rope_order(w_ref[:, sl]).astype(BF16)
        acc_b[...] = jnp.zeros_like(acc_b)

    def project(acc_ref):
        acc_ref[...] = _dot(h_ref[...], wb_ref[...])

    def finish(acc_ref):
        prev = jnp.maximum(i - 1, 0)
        cos = _rope_tile(crow_ref, ccol_ref, prev, tm)
        sin = _rope_tile(srow_ref, scol_ref, prev, tm)
        g = g_ref[...]
        for hd in range(N_Q_HEADS):
            grp, r = divmod(hd, Q_PER_KV)
            x = acc_ref[:, hd * HEAD_DIM:(hd + 1) * HEAD_DIM]
            y = _rope(_head_rms(x, g), cos, sin).astype(o_ref.dtype)
            for t in range(tm // tq):
                o_ref[t, grp, r * tq:(r + 1) * tq, :] = y[t * tq:(t + 1) * tq]

    _lagged_step(i, project, finish, acc_a, acc_b)


def _qproj(h, w_in, gain, tables, tm, tq):
    m, d = h.shape
    n_tiles = m // tm
    cur, prev = _lag_index_maps(n_tiles)
    fixed = lambda i: (0, 0)
    tab_specs = [pl.BlockSpec(t.shape, fixed) for t in tables]
    oshape = (m // tq, N_KV_HEADS, Q_PER_KV * tq, HEAD_DIM)
    return pl.pallas_call(
        functools.partial(_qproj_kernel, tm=tm, tq=tq),
        out_shape=jax.ShapeDtypeStruct(oshape, BF16),
        grid=(n_tiles + 1,),
        in_specs=[pl.BlockSpec((tm, d), lambda i: (cur(i), 0)),
                  pl.BlockSpec((d, ATTN_Q_W), lambda i: (0, IN_Q_BLK)),
                  pl.BlockSpec((1, HEAD_DIM), fixed)] + tab_specs,
        out_specs=pl.BlockSpec((tm // tq,) + oshape[1:], lambda i: (prev(i), 0, 0, 0)),
        scratch_shapes=[pltpu.VMEM((d, ATTN_Q_W), BF16), pltpu.VMEM((tm, ATTN_Q_W), F32),
                        pltpu.VMEM((tm, ATTN_Q_W), F32)],
        compiler_params=_params("arbitrary"),
        name="q_proj",
    )(h, w_in, gain.reshape(1, HEAD_DIM), *tables)


def _gmlp_kernel(h_ref, wu_ref, wv_ref, g_ref, ws_ref, b_ref, wo_ref, o_ref, wob_ref, wb_ref,
                 *, tm):
    @pl.when(pl.program_id(0) == 0)
    def _():
        wb_ref[:, :GMLP_WIDTH] = wu_ref[...].astype(BF16)
        wb_ref[:, GMLP_WIDTH:] = wv_ref[...].astype(BF16)

    _cast_side_job([(wo_ref, wob_ref)])
    z = jax.nn.gelu(_dot(h_ref[...], wb_ref[...]))
    u = z[:, :GMLP_WIDTH]
    vn = _head_rms(z[:, GMLP_WIDTH:], g_ref[...]).astype(BF16)
    bias = b_ref[...]
    for c in range(tm // CHUNK):
        rows = slice(c * CHUNK, (c + 1) * CHUNK)
        for grp in range(GMLP_GROUPS):
            cols = slice(grp * HEAD_DIM, (grp + 1) * HEAD_DIM)
            mixed = _dot(ws_ref[grp].astype(BF16), vn[rows, cols]) + bias[:, cols]
            o_ref[rows, cols] = (u[rows, cols] * mixed).astype(o_ref.dtype)


def _gmlp(h, w_in, sgu_gain, w_spatial, bias_full, w_o, tm):
    m, d = h.shape
    row = lambda i: (i, 0)
    fixed = lambda i: (0, 0)
    side_in, side_out, side_shapes = _side_specs([w_o], m // tm, row)
    return pl.pallas_call(
        functools.partial(_gmlp_kernel, tm=tm),
        out_shape=[jax.ShapeDtypeStruct((m, GMLP_WIDTH), BF16)] + side_shapes,
        grid=(m // tm,),
        in_specs=[pl.BlockSpec((tm, d), row),
                  pl.BlockSpec((d, GMLP_WIDTH), lambda i: (0, IN_U_BLK)),
                  pl.BlockSpec((d, GMLP_WIDTH), lambda i: (0, IN_V_BLK)),
                  pl.BlockSpec((1, GMLP_WIDTH), fixed),
                  pl.BlockSpec((GMLP_GROUPS, CHUNK, CHUNK), lambda i: (0, 0, 0)),
                  pl.BlockSpec((CHUNK, GMLP_WIDTH), fixed)] + side_in,
        out_specs=[pl.BlockSpec((tm, GMLP_WIDTH), row)] + side_out,
        scratch_shapes=[pltpu.VMEM((d, 2 * GMLP_WIDTH), BF16)],
        compiler_params=_params("arbitrary"),
        name="gmlp",
    )(h, w_in, w_in, sgu_gain.reshape(1, GMLP_WIDTH), w_spatial, bias_full, w_o)


def _memkv_kernel(mem_ref, g_ref, w_ref, kg_ref, kt_ref, v_ref):
    mem_n = _head_rms(mem_ref[...], g_ref[...]).astype(BF16)
    kv = _dot(mem_n, w_ref[...].astype(BF16))
    kg = kg_ref[...]
    for hd in range(N_MEM_HEADS):
        sl = slice(hd * HEAD_DIM, (hd + 1) * HEAD_DIM)
        kt_ref[sl, :] = jnp.transpose(_head_rms(kv[:, sl], kg)).astype(kt_ref.dtype)
    v_ref[...] = kv[:, MEM_WIDTH:].astype(v_ref.dtype)


def _memkv(mem, mem_gain, w, mk_gain):
    mt, d = mem.shape
    return pl.pallas_call(
        _memkv_kernel,
        out_shape=(jax.ShapeDtypeStruct((MEM_WIDTH, mt), BF16),
                   jax.ShapeDtypeStruct((mt, MEM_WIDTH), BF16)),
        compiler_params=pltpu.CompilerParams(vmem_limit_bytes=VMEM_LIMIT_BYTES),
        name="mem_kv",
    )(mem, mem_gain.reshape(1, d), w, mk_gain.reshape(1, HEAD_DIM))


def _memattn_kernel(h_ref, w_ref, g_ref, kt_ref, v_ref, wo_ref, o_ref, wob_ref, wb_ref):
    @pl.when(pl.program_id(0) == 0)
    def _():
        wb_ref[...] = w_ref[...].astype(BF16)

    _cast_side_job([(wo_ref, wob_ref)])
    acc = _dot(h_ref[...], wb_ref[...])
    g = g_ref[...]
    for hd in range(N_MEM_HEADS):
        sl = slice(hd * HEAD_DIM, (hd + 1) * HEAD_DIM)
        qn = _head_rms(acc[:, sl], g).astype(BF16)
        s = _dot(qn, kt_ref[sl, :])
        p = jnp.exp2(s - jnp.max(s, axis=-1, keepdims=True))
        l = jnp.sum(p, axis=-1, keepdims=True)
        o = _dot(p.astype(BF16), v_ref[:, sl])
        o_ref[:, sl] = (o / l).astype(o_ref.dtype)


def _memattn(h, w_in, mq_gain, km_t, vm, w_o, tm):
    m, d = h.shape
    mt = vm.shape[0]
    row = lambda i: (i, 0)
    fixed = lambda i: (0, 0)
    side_in, side_out, side_shapes = _side_specs([w_o], m // tm, row)
    return pl.pallas_call(
        _memattn_kernel,
        out_shape=[jax.ShapeDtypeStruct((m, MEM_WIDTH), BF16)] + side_shapes,
        grid=(m // tm,),
        in_specs=[pl.BlockSpec((tm, d), row),
                  pl.BlockSpec((d, MEM_WIDTH), lambda i: (0, IN_QM_BLK)),
                  pl.BlockSpec((1, HEAD_DIM), fixed),
                  pl.BlockSpec((MEM_WIDTH, mt), fixed),
                  pl.BlockSpec((mt, MEM_WIDTH), fixed)] + side_in,
        out_specs=[pl.BlockSpec((tm, MEM_WIDTH), row)] + side_out,
        scratch_shapes=[pltpu.VMEM((d, MEM_WIDTH), BF16)],
        compiler_params=_params("arbitrary"),
        name="mem_attn",
    )(h, w_in, mq_gain.reshape(1, HEAD_DIM), km_t, vm, w_o)


def _store_heads(o_ref, out, tq):
    for r in range(Q_PER_KV):
        o_ref[:, r * HEAD_DIM:(r + 1) * HEAD_DIM] = out[r * tq:(r + 1) * tq].astype(o_ref.dtype)


def _attn_unshifted_kernel(q_ref, kt_ref, v_ref, *rest, tq, nk, nside):
    side = rest[:nside]
    o_ref = rest[nside]
    side_out = rest[nside + 1:2 * nside + 1]
    l_ref, acc_ref = rest[2 * nside + 1:]
    _cast_side_job(zip(side, side_out))
    l_ref[...] = jnp.zeros_like(l_ref)
    acc_ref[...] = jnp.zeros_like(acc_ref)

    def body(j, carry):
        off = pl.multiple_of(j * KV_BLOCK, KV_BLOCK)
        p = jnp.exp2(_dot(q_ref[0, 0], kt_ref[j]))
        l_ref[...] += p[:, :HEAD_DIM] + p[:, HEAD_DIM:]
        acc_ref[...] += _dot(p.astype(BF16), v_ref[pl.ds(off, KV_BLOCK), :])
        return carry

    lax.fori_loop(0, nk, body, 0, unroll=True)
    l = jnp.sum(l_ref[...], axis=-1, keepdims=True)
    _store_heads(o_ref, acc_ref[...] / l, tq)


def _attn_online_kernel(q_ref, kt_ref, v_ref, *rest, tq, nk, nside):
    side = rest[:nside]
    o_ref = rest[nside]
    side_out = rest[nside + 1:2 * nside + 1]
    _cast_side_job(zip(side, side_out))
    q = q_ref[0, 0]
    rows = Q_PER_KV * tq

    def body(j, carry):
        m, l, acc = carry
        off = pl.multiple_of(j * KV_BLOCK, KV_BLOCK)
        s = _dot(q, kt_ref[j])
        m_new = jnp.maximum(m, jnp.max(s, axis=-1, keepdims=True))
        alpha = jnp.exp2(m - m_new)
        p = jnp.exp2(s - m_new)
        l = alpha * l + jnp.sum(p, axis=-1, keepdims=True)
        acc = alpha * acc + _dot(p.astype(BF16), v_ref[pl.ds(off, KV_BLOCK), :])
        return m_new, l, acc

    m0 = jnp.full((rows, 1), -jnp.inf, F32)
    l0 = jnp.zeros((rows, 1), F32)
    acc0 = jnp.zeros((rows, HEAD_DIM), F32)
    _, l, acc = lax.fori_loop(0, nk, body, (m0, l0, acc0))
    _store_heads(o_ref, acc / l, tq)


def _attention(q, kt, v, *side_weights, unshifted):
    nq, _, rows, _ = q.shape
    tq = rows // Q_PER_KV
    s = v.shape[0]
    nk = s // KV_BLOCK
    gw = Q_PER_KV * HEAD_DIM
    nside = len(side_weights)
    side_in, side_out, side_shapes = _side_specs(side_weights, N_KV_HEADS * nq,
                                                 lambda g, i: (g * nq + i, 0))
    if unshifted:
        body = functools.partial(_attn_unshifted_kernel, tq=tq, nk=nk, nside=nside)
        scratch = [pltpu.VMEM((rows, HEAD_DIM), F32), pltpu.VMEM((rows, HEAD_DIM), F32)]
        name = "attn_unshifted"
    else:
        body = functools.partial(_attn_online_kernel, tq=tq, nk=nk, nside=nside)
        scratch = []
        name = "attn_online"
    return pl.pallas_call(
        body,
        out_shape=[jax.ShapeDtypeStruct((s, ATTN_Q_W), BF16)] + side_shapes,
        grid=(N_KV_HEADS, nq),
        in_specs=[pl.BlockSpec((1, 1, rows, HEAD_DIM), lambda g, i: (i, g, 0, 0)),
                  pl.BlockSpec((nk, HEAD_DIM, KV_BLOCK), lambda g, i: (0, g, 0)),
                  pl.BlockSpec((s, HEAD_DIM), lambda g, i: (0, g))] + side_in,
        out_specs=[pl.BlockSpec((tq, gw), lambda g, i: (i, g))] + side_out,
        scratch_shapes=scratch,
        compiler_params=_params("arbitrary", "arbitrary"),
        name=name,
    )(q, kt, v, *side_weights)


def _gated_merge_kernel(h_ref, a_ref, gm_ref, ym_ref, wga_ref, wgg_ref, wgm_ref,
                        wa_ref, wg_ref, wm_ref, o_ref, wb_ref):
    @pl.when(pl.program_id(1) == 0)
    def _():
        for b, w_ref in enumerate((wga_ref, wgg_ref, wgm_ref)):
            wb_ref[b] = w_ref[...].astype(BF16)

    h = h_ref[...]
    y = None
    for b, (y_ref, w_ref) in enumerate(((a_ref, wa_ref), (gm_ref, wg_ref), (ym_ref, wm_ref))):
        gate = 1.0 / (1.0 + jnp.exp(-_dot(h, wb_ref[b])))
        term = gate * _dot(y_ref[...], w_ref[...])
        y = term if y is None else y + term
    o_ref[...] = y.astype(o_ref.dtype)


def _gated_merge(h, attn, gm, ym, w_in, w_a, w_g, w_m, tm, tn):
    m, d = h.shape
    n = w_a.shape[1]
    nb = n // tn
    first = IN_GATE_BLK * IN_COL_BLOCK // tn
    row = lambda j, i: (i, 0)
    col = lambda j, i: (0, j)
    return pl.pallas_call(
        _gated_merge_kernel,
        out_shape=jax.ShapeDtypeStruct((m, n), BF16),
        grid=(nb, m // tm),
        in_specs=[pl.BlockSpec((tm, d), row),
                  pl.BlockSpec((tm, attn.shape[1]), row),
                  pl.BlockSpec((tm, gm.shape[1]), row),
                  pl.BlockSpec((tm, ym.shape[1]), row),
                  pl.BlockSpec((d, tn), lambda j, i: (0, first + j)),
                  pl.BlockSpec((d, tn), lambda j, i: (0, first + nb + j)),
                  pl.BlockSpec((d, tn), lambda j, i: (0, first + 2 * nb + j)),
                  pl.BlockSpec((w_a.shape[0], tn), col),
                  pl.BlockSpec((w_g.shape[0], tn), col),
                  pl.BlockSpec((w_m.shape[0], tn), col)],
        out_specs=pl.BlockSpec((tm, tn), lambda j, i: (i, j)),
        scratch_shapes=[pltpu.VMEM((3, d, tn), BF16)],
        compiler_params=_params("arbitrary", "arbitrary"),
        name="gated_merge",
    )(h, attn, gm, ym, w_in, w_in, w_in, w_a, w_g, w_m)


def _outproj_kernel(mg_ref, w_ref, x_ref, g_ref, x1_ref, h2_ref):
    x1 = x_ref[...] + _dot(mg_ref[...], w_ref[...])
    x1_ref[...] = x1
    h2_ref[...] = _head_rms(x1, g_ref[...]).astype(h2_ref.dtype)


def _outproj(merged, w, x, gain, tm):
    m, d = x.shape
    row = lambda i: (i, 0)
    fixed = lambda i: (0, 0)
    return pl.pallas_call(
        _outproj_kernel,
        out_shape=(jax.ShapeDtypeStruct((m, d), F32), jax.ShapeDtypeStruct((m, d), BF16)),
        grid=(m // tm,),
        in_specs=[pl.BlockSpec((tm, d), row), pl.BlockSpec((d, d), fixed),
                  pl.BlockSpec((tm, d), row), pl.BlockSpec((1, d), fixed)],
        out_specs=(pl.BlockSpec((tm, d), row), pl.BlockSpec((tm, d), row)),
        compiler_params=_params("parallel"),
        name="out_proj",
    )(merged, w, x, gain.reshape(1, d))


def _ffn_up_kernel(h_ref, w_ref, o_ref):
    z = jnp.maximum(_dot(h_ref[...], w_ref[...]), 0.0)
    o_ref[...] = (z * z).astype(o_ref.dtype)


def _ffn_up(h, w, tm, tn):
    m, d = h.shape
    n = w.shape[1]
    return pl.pallas_call(
        _ffn_up_kernel,
        out_shape=jax.ShapeDtypeStruct((m, n), BF16),
        grid=(m // tm, n // tn),
        in_specs=[pl.BlockSpec((tm, d), lambda i, j: (i, 0)),
                  pl.BlockSpec((d, tn), lambda i, j: (0, j))],
        out_specs=pl.BlockSpec((tm, tn), lambda i, j: (i, j)),
        compiler_params=_params("parallel", "parallel"),
        name="ffn_up",
    )(h, w)


def _ffn_down_kernel(a_ref, w_ref, x_ref, o_ref):
    k = pl.program_id(2)

    @pl.when(k == 0)
    def _():
        o_ref[...] = x_ref[...] + _dot(a_ref[...], w_ref[...])

    @pl.when(k != 0)
    def _():
        o_ref[...] += _dot(a_ref[...], w_ref[...])


def _ffn_down(a, w, x1, tm, tn, tk):
    m, kdim = a.shape
    n = w.shape[1]
    return pl.pallas_call(
        _ffn_down_kernel,
        out_shape=jax.ShapeDtypeStruct((m, n), F32),
        grid=(m // tm, n // tn, kdim // tk),
        in_specs=[pl.BlockSpec((tm, tk), lambda i, j, k: (i, k)),
                  pl.BlockSpec((tk, tn), lambda i, j, k: (k, j)),
                  pl.BlockSpec((tm, tn), lambda i, j, k: (i, j))],
        out_specs=pl.BlockSpec((tm, tn), lambda i, j, k: (i, j)),
        compiler_params=_params("parallel", "parallel", "arbitrary"),
        name="ffn_down",
    )(a, w, x1)


def _rope_tables(s):
    inv = ROPE_THETA ** (-jnp.arange(ROPE_PAIRS, dtype=F32) / ROPE_PAIRS)
    inv_lane = jnp.tile(inv, HEAD_DIM // ROPE_PAIRS)
    lane_grp = jnp.arange(HEAD_DIM) // ROPE_PAIRS
    is_row = (lane_grp % 2) == 0
    sign = jnp.where(lane_grp < 2, -1.0, 1.0).astype(F32)

    def tables(n):
        ang = jnp.arange(n, dtype=F32)[:, None] * inv_lane
        return jnp.cos(ang), jnp.sin(ang) * sign

    cr, sr = tables(s // GRID_W)
    cc, sc = tables(GRID_W)
    return (jnp.where(is_row, cr, 0.0), jnp.where(is_row, sr, 0.0),
            jnp.where(is_row, 0.0, cc), jnp.where(is_row, 0.0, sc))


def kernel(x, mem, norm_mix, w_in, q_norm, k_norm, sgu_norm, w_spatial, b_spatial, mem_norm,
           w_mem_kv, mq_norm, mk_norm, w_attn_o, w_gmlp_o, w_mem_o, w_out, norm_ffn,
           w_ffn_up, w_ffn_down):
    b, s, d = x.shape
    assert b == 1 and d == D_MODEL and norm_mix.shape[0] == 1
    xs = x[0]
    w_in0 = w_in[0]
    tables = _rope_tables(s)
    perm = jnp.asarray(ROPE_PERM)
    gq = q_norm[0][perm] * Q_SCALE
    gk = k_norm[0][perm]
    bias_full = jnp.repeat(b_spatial[0].T, HEAD_DIM, axis=1)

    h, kt, v = _kvproj(xs, norm_mix[0], w_in0, gk, tables, TM_PROJ)
    q = _qproj(h, w_in0, gq, tables, TM_PROJ, ATTN_TQ)
    gm, w_gmlp_o_b = _gmlp(h, w_in0, sgu_norm[0], w_spatial[0], bias_full, w_gmlp_o[0], TM_PROJ)
    km_t, vm = _memkv(mem[0], mem_norm[0], w_mem_kv[0], mk_norm[0])
    ym, w_mem_o_b = _memattn(h, w_in0, mq_norm[0] * Q_SCALE, km_t, vm, w_mem_o[0], TM_PROJ)

    logit_bound = (1.02 * LOG2E * math.sqrt(HEAD_DIM)
                   * jnp.max(jnp.abs(q_norm[0])) * jnp.max(jnp.abs(k_norm[0])))
    attn, w_attn_o_b, w_out_b, w_up_b, w_down_b = lax.cond(
        logit_bound <= MAX_UNSHIFTED_LOGIT,
        functools.partial(_attention, unshifted=True),
        functools.partial(_attention, unshifted=False),
        q, kt, v, w_attn_o[0], w_out[0], w_ffn_up[0], w_ffn_down[0])

    merged = _gated_merge(h, attn, gm, ym, w_in0, w_attn_o_b, w_gmlp_o_b, w_mem_o_b,
                          TM_MERGE, TN_MERGE)
    x1, h2 = _outproj(merged, w_out_b, xs, norm_ffn[0], TM_OUT)
    a = _ffn_up(h2, w_up_b, TM_UP, TN_UP)
    out = _ffn_down(a, w_down_b, x1, TM_DOWN, TN_DOWN, TK_DOWN)
    return out[None]
```

```python
import functools
import math

import jax
import jax.numpy as jnp
from jax import lax
from jax.experimental import pallas as pl
from jax.experimental.pallas import tpu as pltpu

D_MODEL = 2048
HEAD_DIM = 128
N_Q_HEADS = 8
N_KV_HEADS = 2
Q_PER_KV = N_Q_HEADS // N_KV_HEADS
GRID_W = 64
ROPE_THETA = 10000.0
ROPE_PAIRS = HEAD_DIM // 4
GMLP_GROUPS = 4
GMLP_WIDTH = GMLP_GROUPS * HEAD_DIM
CHUNK = 128
N_MEM_HEADS = 4
MEM_WIDTH = N_MEM_HEADS * HEAD_DIM
D_FF = 4 * D_MODEL
EPS = 1e-6
ATTN_Q_W = N_Q_HEADS * HEAD_DIM
ATTN_KV_W = N_KV_HEADS * HEAD_DIM
IN_COL_BLOCK = 512
IN_Q_BLK, IN_KV_BLK, IN_U_BLK, IN_V_BLK, IN_QM_BLK, IN_GATE_BLK = 0, 2, 3, 4, 5, 6

VMEM_LIMIT_BYTES = 56 * 1024 * 1024
LOG2E = math.log2(math.e)
Q_SCALE = HEAD_DIM ** -0.5 * LOG2E
KV_BLOCK = 256
ATTN_TQ = 512
MAX_UNSHIFTED_LOGIT = 64.0
ROPE_PERM = tuple(list(range(0, 32)) + list(range(64, 96)) + list(range(32, 64))
                  + list(range(96, 128)))

TM_PROJ = 512
TM_MERGE, TN_MERGE = 512, 512
TM_OUT = 512
TM_UP, TN_UP = 2048, 1024
TM_DOWN, TN_DOWN, TK_DOWN = 1024, 1024, 4096

F32 = jnp.float32
BF16 = jnp.bfloat16


def _params(*sem):
    return pltpu.CompilerParams(dimension_semantics=sem, vmem_limit_bytes=VMEM_LIMIT_BYTES)


def _dot(a, b):
    return jnp.dot(a, b, preferred_element_type=F32)


def _head_rms(x, gain):
    ms = jnp.mean(x * x, axis=-1, keepdims=True)
    return x * lax.rsqrt(ms + EPS) * gain


def _rope(y, cos, sin_signed):
    return y * cos + pltpu.roll(y, HEAD_DIM // 2, 1) * sin_signed


def _to_rope_order(w):
    grp = lax.broadcasted_iota(jnp.int32, w.shape, 1) // ROPE_PAIRS
    from_right = pltpu.roll(w, HEAD_DIM - ROPE_PAIRS, 1)
    from_left = pltpu.roll(w, ROPE_PAIRS, 1)
    return jnp.where(grp == 1, from_right, jnp.where(grp == 2, from_left, w))


def _rope_tile(row_ref, col_ref, i, tm):
    nrow = tm // GRID_W
    rows = row_ref[pl.ds(pl.multiple_of(i * nrow, nrow), nrow), :]
    col = col_ref[...]
    return jnp.concatenate([rows[r:r + 1, :] + col for r in range(nrow)], axis=0)


def _lagged_step(i, project, finish, buf_a, buf_b):
    @pl.when(i % 2 == 0)
    def _():
        project(buf_a)
        finish(buf_b)

    @pl.when(i % 2 == 1)
    def _():
        project(buf_b)
        finish(buf_a)


def _lag_index_maps(n_tiles):
    return (lambda i: jnp.minimum(i, n_tiles - 1)), (lambda i: jnp.maximum(i - 1, 0))


def _cast_side_job(pairs):
    for src, dst in pairs:
        dst[...] = src[...].astype(dst.dtype)


def _side_specs(arrays, nsteps, index):
    in_specs, out_specs, out_shapes = [], [], []
    for a in arrays:
        rows, cols = a.shape
        blk = (rows // nsteps, cols)
        in_specs.append(pl.BlockSpec(blk, index))
        out_specs.append(pl.BlockSpec(blk, index))
        out_shapes.append(jax.ShapeDtypeStruct(a.shape, BF16))
    return in_specs, out_specs, out_shapes


def _kvproj_kernel(x_ref, gx_ref, w_ref, gk_ref, crow_ref, srow_ref, ccol_ref, scol_ref,
                   h_ref, kt_ref, v_ref, wb_ref, acc_a, acc_b, *, tm):
    i = pl.program_id(0)

    @pl.when(i == 0)
    def _():
        for hd in range(N_KV_HEADS):
            sl = slice(hd * HEAD_DIM, (hd + 1) * HEAD_DIM)
            wb_ref[:, sl] = _to_rope_order(w_ref[:, sl]).astype(BF16)
        wb_ref[:, ATTN_KV_W:] = w_ref[:, ATTN_KV_W:].astype(BF16)
        acc_b[...] = jnp.zeros_like(acc_b)

    def project(acc_ref):
        h = _head_rms(x_ref[...], gx_ref[...]).astype(BF16)
        h_ref[...] = h
        acc_ref[...] = _dot(h, wb_ref[...])

    def finish(acc_ref):
        prev = jnp.maximum(i - 1, 0)
        cos = _rope_tile(crow_ref, ccol_ref, prev, tm)
        sin = _rope_tile(srow_ref, scol_ref, prev, tm)
        g = gk_ref[...]
        for hd in range(N_KV_HEADS):
            sl = slice(hd * HEAD_DIM, (hd + 1) * HEAD_DIM)
            y = _rope(_head_rms(acc_ref[:, sl], g), cos, sin)
            for c in range(tm // KV_BLOCK):
                blk = y[c * KV_BLOCK:(c + 1) * KV_BLOCK, :]
                kt_ref[c, sl, :] = jnp.transpose(blk).astype(kt_ref.dtype)
        v_ref[...] = acc_ref[:, ATTN_KV_W:].astype(v_ref.dtype)

    _lagged_step(i, project, finish, acc_a, acc_b)


def _kvproj(x, gx, w_in, gk, tables, tm):
    m, d = x.shape
    n_tiles = m // tm
    cur, prev = _lag_index_maps(n_tiles)
    fixed = lambda i: (0, 0)
    n = 2 * ATTN_KV_W
    tab_specs = [pl.BlockSpec(t.shape, fixed) for t in tables]
    return pl.pallas_call(
        functools.partial(_kvproj_kernel, tm=tm),
        out_shape=(jax.ShapeDtypeStruct((m, d), BF16),
                   jax.ShapeDtypeStruct((m // KV_BLOCK, ATTN_KV_W, KV_BLOCK), BF16),
                   jax.ShapeDtypeStruct((m, ATTN_KV_W), BF16)),
        grid=(n_tiles + 1,),
        in_specs=[pl.BlockSpec((tm, d), lambda i: (cur(i), 0)), pl.BlockSpec((1, d), fixed),
                  pl.BlockSpec((d, n), lambda i: (0, IN_KV_BLK)),
                  pl.BlockSpec((1, HEAD_DIM), fixed)] + tab_specs,
        out_specs=(pl.BlockSpec((tm, d), lambda i: (cur(i), 0)),
                   pl.BlockSpec((tm // KV_BLOCK, ATTN_KV_W, KV_BLOCK),
                                lambda i: (prev(i), 0, 0)),
                   pl.BlockSpec((tm, ATTN_KV_W), lambda i: (prev(i), 0))),
        scratch_shapes=[pltpu.VMEM((d, n), BF16), pltpu.VMEM((tm, n), F32),
                        pltpu.VMEM((tm, n), F32)],
        compiler_params=_params("arbitrary"),
        name="kv_proj",
    )(x, gx.reshape(1, d), w_in, gk.reshape(1, HEAD_DIM), *tables)


def _qproj_kernel(h_ref, w_ref, g_ref, crow_ref, srow_ref, ccol_ref, scol_ref, o_ref, wb_ref,
                  acc_a, acc_b, *, tm, tq):
    i = pl.program_id(0)

    @pl.when(i == 0)
    def _():
        for hd in range(N_Q_HEADS):
            sl = slice(hd * HEAD_DIM, (hd + 1) * HEAD_DIM)
            wb_ref[:, sl] = _to_rope_order(w_ref[:, sl]).astype(BF16)
        acc_b[...] = jnp.zeros_like(acc_b)

    def project(acc_ref):
        acc_ref[...] = _dot(h_ref[...], wb_ref[...])

    def finish(acc_ref):
        prev = jnp.maximum(i - 1, 0)
        cos = _rope_tile(crow_ref, ccol_ref, prev, tm)
        sin = _rope_tile(srow_ref, scol_ref, prev, tm)
        g = g_ref[...]
        for hd in range(N_Q_HEADS):
            grp, r = divmod(hd, Q_PER_KV)
            x = acc_ref[:, hd * HEAD_DIM:(hd + 1) * HEAD_DIM]
            y = _rope(_head_rms(x, g), cos, sin).astype(o_ref.dtype)
            for t in range(tm // tq):
                o_ref[t, grp, r * tq:(r + 1) * tq, :] = y[t * tq:(t + 1) * tq]

    _lagged_step(i, project, finish, acc_a, acc_b)


def _qproj(h, w_in, gain, tables, tm, tq):
    m, d = h.shape
    n_tiles = m // tm
    cur, prev = _lag_index_maps(n_tiles)
    fixed = lambda i: (0, 0)
    tab_specs = [pl.BlockSpec(t.shape, fixed) for t in tables]
    oshape = (m // tq, N_KV_HEADS, Q_PER_KV * tq, HEAD_DIM)
    return pl.pallas_call(
        functools.partial(_qproj_kernel, tm=tm, tq=tq),
        out_shape=jax.ShapeDtypeStruct(oshape, BF16),
        grid=(n_tiles + 1,),
        in_specs=[pl.BlockSpec((tm, d), lambda i: (cur(i), 0)),
                  pl.BlockSpec((d, ATTN_Q_W), lambda i: (0, IN_Q_BLK)),
                  pl.BlockSpec((1, HEAD_DIM), fixed)] + tab_specs,
        out_specs=pl.BlockSpec((tm // tq,) + oshape[1:], lambda i: (prev(i), 0, 0, 0)),
        scratch_shapes=[pltpu.VMEM((d, ATTN_Q_W), BF16), pltpu.VMEM((tm, ATTN_Q_W), F32),
                        pltpu.VMEM((tm, ATTN_Q_W), F32)],
        compiler_params=_params("arbitrary"),
        name="q_proj",
    )(h, w_in, gain.reshape(1, HEAD_DIM), *tables)


def _gmlp_kernel(h_ref, wu_ref, wv_ref, g_ref, ws_ref, b_ref, wo_ref, o_ref, wob_ref, wb_ref,
                 *, tm):
    @pl.when(pl.program_id(0) == 0)
    def _():
        wb_ref[:, :GMLP_WIDTH] = wu_ref[...].astype(BF16)
        wb_ref[:, GMLP_WIDTH:] = wv_ref[...].astype(BF16)

    _cast_side_job([(wo_ref, wob_ref)])
    z = jax.nn.gelu(_dot(h_ref[...], wb_ref[...]))
    u = z[:, :GMLP_WIDTH]
    vn = _head_rms(z[:, GMLP_WIDTH:], g_ref[...]).astype(BF16)
    bias = b_ref[...]
    for c in range(tm // CHUNK):
        rows = slice(c * CHUNK, (c + 1) * CHUNK)
        for grp in range(GMLP_GROUPS):
            cols = slice(grp * HEAD_DIM, (grp + 1) * HEAD_DIM)
            mixed = _dot(ws_ref[grp].astype(BF16), vn[rows, cols]) + bias[:, cols]
            o_ref[rows, cols] = (u[rows, cols] * mixed).astype(o_ref.dtype)


def _gmlp(h, w_in, sgu_gain, w_spatial, bias_full, w_o, tm):
    m, d = h.shape
    row = lambda i: (i, 0)
    fixed = lambda i: (0, 0)
    side_in, side_out, side_shapes = _side_specs([w_o], m // tm, row)
    return pl.pallas_call(
        functools.partial(_gmlp_kernel, tm=tm),
        out_shape=[jax.ShapeDtypeStruct((m, GMLP_WIDTH), BF16)] + side_shapes,
        grid=(m // tm,),
        in_specs=[pl.BlockSpec((tm, d), row),
                  pl.BlockSpec((d, GMLP_WIDTH), lambda i: (0, IN_U_BLK)),
                  pl.BlockSpec((d, GMLP_WIDTH), lambda i: (0, IN_V_BLK)),
                  pl.BlockSpec((1, GMLP_WIDTH), fixed),
                  pl.BlockSpec((GMLP_GROUPS, CHUNK, CHUNK), lambda i: (0, 0, 0)),
                  pl.BlockSpec((CHUNK, GMLP_WIDTH), fixed)] + side_in,
        out_specs=[pl.BlockSpec((tm, GMLP_WIDTH), row)] + side_out,
        scratch_shapes=[pltpu.VMEM((d, 2 * GMLP_WIDTH), BF16)],
        compiler_params=_params("arbitrary"),
        name="gmlp",
    )(h, w_in, w_in, sgu_gain.reshape(1, GMLP_WIDTH), w_spatial, bias_full, w_o)


def _memkv_kernel(mem_ref, g_ref, w_ref, kg_ref, kt_ref, v_ref):
    mem_n = _head_rms(mem_ref[...], g_ref[...]).astype(BF16)
    kv = _dot(mem_n, w_ref[...].astype(BF16))
    kg = kg_ref[...]
    for hd in range(N_MEM_HEADS):
        sl = slice(hd * HEAD_DIM, (hd + 1) * HEAD_DIM)
        kt_ref[sl, :] = jnp.transpose(_head_rms(kv[:, sl], kg)).astype(kt_ref.dtype)
    v_ref[...] = kv[:, MEM_WIDTH:].astype(v_ref.dtype)


def _memkv(mem, mem_gain, w, mk_gain):
    mt, d = mem.shape
    return pl.pallas_call(
        _memkv_kernel,
        out_shape=(jax.ShapeDtypeStruct((MEM_WIDTH, mt), BF16),
                   jax.ShapeDtypeStruct((mt, MEM_WIDTH), BF16)),
        compiler_params=pltpu.CompilerParams(vmem_limit_bytes=VMEM_LIMIT_BYTES),
        name="mem_kv",
    )(mem, mem_gain.reshape(1, d), w, mk_gain.reshape(1, HEAD_DIM))


def _memattn_kernel(h_ref, w_ref, g_ref, kt_ref, v_ref, wo_ref, o_ref, wob_ref, wb_ref):
    @pl.when(pl.program_id(0) == 0)
    def _():
        wb_ref[...] = w_ref[...].astype(BF16)

    _cast_side_job([(wo_ref, wob_ref)])
    acc = _dot(h_ref[...], wb_ref[...])
    g = g_ref[...]
    for hd in range(N_MEM_HEADS):
        sl = slice(hd * HEAD_DIM, (hd + 1) * HEAD_DIM)
        qn = _head_rms(acc[:, sl], g).astype(BF16)
        s = _dot(qn, kt_ref[sl, :])
        p = jnp.exp2(s - jnp.max(s, axis=-1, keepdims=True))
        l = jnp.sum(p, axis=-1, keepdims=True)
        o = _dot(p.astype(BF16), v_ref[:, sl])
        o_ref[:, sl] = (o / l).astype(o_ref.dtype)


def _memattn(h, w_in, mq_gain, km_t, vm, w_o, tm):
    m, d = h.shape
    mt = vm.shape[0]
    row = lambda i: (i, 0)
    fixed = lambda i: (0, 0)
    side_in, side_out, side_shapes = _side_specs([w_o], m // tm, row)
    return pl.pallas_call(
        _memattn_kernel,
        out_shape=[jax.ShapeDtypeStruct((m, MEM_WIDTH), BF16)] + side_shapes,
        grid=(m // tm,),
        in_specs=[pl.BlockSpec((tm, d), row),
                  pl.BlockSpec((d, MEM_WIDTH), lambda i: (0, IN_QM_BLK)),
                  pl.BlockSpec((1, HEAD_DIM), fixed),
                  pl.BlockSpec((MEM_WIDTH, mt), fixed),
                  pl.BlockSpec((mt, MEM_WIDTH), fixed)] + side_in,
        out_specs=[pl.BlockSpec((tm, MEM_WIDTH), row)] + side_out,
        scratch_shapes=[pltpu.VMEM((d, MEM_WIDTH), BF16)],
        compiler_params=_params("arbitrary"),
        name="mem_attn",
    )(h, w_in, mq_gain.reshape(1, HEAD_DIM), km_t, vm, w_o)


def _store_heads(o_ref, out, tq):
    for r in range(Q_PER_KV):
        o_ref[:, r * HEAD_DIM:(r + 1) * HEAD_DIM] = out[r * tq:(r + 1) * tq].astype(o_ref.dtype)


def _attn_unshifted_kernel(q_ref, kt_ref, v_ref, *rest, tq, nk, nside):
    side = rest[:nside]
    o_ref = rest[nside]
    side_out = rest[nside + 1:2 * nside + 1]
    l_ref, acc_ref = rest[2 * nside + 1:]
    _cast_side_job(zip(side, side_out))
    l_ref[...] = jnp.zeros_like(l_ref)
    acc_ref[...] = jnp.zeros_like(acc_ref)

    def body(j, carry):
        off = pl.multiple_of(j * KV_BLOCK, KV_BLOCK)
        p = jnp.exp2(_dot(q_ref[0, 0], kt_ref[j]))
        l_ref[...] += p[:, :HEAD_DIM] + p[:, HEAD_DIM:]
        acc_ref[...] += _dot(p.astype(BF16), v_ref[pl.ds(off, KV_BLOCK), :])
        return carry

    lax.fori_loop(0, nk, body, 0, unroll=True)
    l = jnp.sum(l_ref[...], axis=-1, keepdims=True)
    _store_heads(o_ref, acc_ref[...] / l, tq)


def _attn_online_kernel(q_ref, kt_ref, v_ref, *rest, tq, nk, nside):
    side = rest[:nside]
    o_ref = rest[nside]
    side_out = rest[nside + 1:2 * nside + 1]
    _cast_side_job(zip(side, side_out))
    q = q_ref[0, 0]
    rows = Q_PER_KV * tq

    def body(j, carry):
        m, l, acc = carry
        off = pl.multiple_of(j * KV_BLOCK, KV_BLOCK)
        s = _dot(q, kt_ref[j])
        m_new = jnp.maximum(m, jnp.max(s, axis=-1, keepdims=True))
        alpha = jnp.exp2(m - m_new)
        p = jnp.exp2(s - m_new)
        l = alpha * l + jnp.sum(p, axis=-1, keepdims=True)
        acc = alpha * acc + _dot(p.astype(BF16), v_ref[pl.ds(off, KV_BLOCK), :])
        return m_new, l, acc

    m0 = jnp.full((rows, 1), -jnp.inf, F32)
    l0 = jnp.zeros((rows, 1), F32)
    acc0 = jnp.zeros((rows, HEAD_DIM), F32)
    _, l, acc = lax.fori_loop(0, nk, body, (m0, l0, acc0))
    _store_heads(o_ref, acc / l, tq)


def _attention(q, kt, v, *side_weights, unshifted):
    nq, _, rows, _ = q.shape
    tq = rows // Q_PER_KV
    s = v.shape[0]
    nk = s // KV_BLOCK
    gw = Q_PER_KV * HEAD_DIM
    nside = len(side_weights)
    side_in, side_out, side_shapes = _side_specs(side_weights, N_KV_HEADS * nq,
                                                 lambda g, i: (g * nq + i, 0))
    if unshifted:
        body = functools.partial(_attn_unshifted_kernel, tq=tq, nk=nk, nside=nside)
        scratch = [pltpu.VMEM((rows, HEAD_DIM), F32), pltpu.VMEM((rows, HEAD_DIM), F32)]
        name = "attn_unshifted"
    else:
        body = functools.partial(_attn_online_kernel, tq=tq, nk=nk, nside=nside)
        scratch = []
        name = "attn_online"
    return pl.pallas_call(
        body,
        out_shape=[jax.ShapeDtypeStruct((s, ATTN_Q_W), BF16)] + side_shapes,
        grid=(N_KV_HEADS, nq),
        in_specs=[pl.BlockSpec((1, 1, rows, HEAD_DIM), lambda g, i: (i, g, 0, 0)),
                  pl.BlockSpec((nk, HEAD_DIM, KV_BLOCK), lambda g, i: (0, g, 0)),
                  pl.BlockSpec((s, HEAD_DIM), lambda g, i: (0, g))] + side_in,
        out_specs=[pl.BlockSpec((tq, gw), lambda g, i: (i, g))] + side_out,
        scratch_shapes=scratch,
        compiler_params=_params("arbitrary", "arbitrary"),
        name=name,
    )(q, kt, v, *side_weights)


def _gated_merge_kernel(h_ref, a_ref, gm_ref, ym_ref, wga_ref, wgg_ref, wgm_ref,
                        wa_ref, wg_ref, wm_ref, o_ref, wb_ref):
    @pl.when(pl.program_id(1) == 0)
    def _():
        for b, w_ref in enumerate((wga_ref, wgg_ref, wgm_ref)):
            wb_ref[b] = w_ref[...].astype(BF16)

    h = h_ref[...]
    y = None
    for b, (y_ref, w_ref) in enumerate(((a_ref, wa_ref), (gm_ref, wg_ref), (ym_ref, wm_ref))):
        gate = 1.0 / (1.0 + jnp.exp(-_dot(h, wb_ref[b])))
        term = gate * _dot(y_ref[...], w_ref[...])
        y = term if y is None else y + term
    o_ref[...] = y.astype(o_ref.dtype)


def _gated_merge(h, attn, gm, ym, w_in, w_a, w_g, w_m, tm, tn):
    m, d = h.shape
    n = w_a.shape[1]
    nb = n // tn
    first = IN_GATE_BLK * IN_COL_BLOCK // tn
    row = lambda j, i: (i, 0)
    col = lambda j, i: (0, j)
    return pl.pallas_call(
        _gated_merge_kernel,
        out_shape=jax.ShapeDtypeStruct((m, n), BF16),
        grid=(nb, m // tm),
        in_specs=[pl.BlockSpec((tm, d), row),
                  pl.BlockSpec((tm, attn.shape[1]), row),
                  pl.BlockSpec((tm, gm.shape[1]), row),
                  pl.BlockSpec((tm, ym.shape[1]), row),
                  pl.BlockSpec((d, tn), lambda j, i: (0, first + j)),
                  pl.BlockSpec((d, tn), lambda j, i: (0, first + nb + j)),
                  pl.BlockSpec((d, tn), lambda j, i: (0, first + 2 * nb + j)),
                  pl.BlockSpec((w_a.shape[0], tn), col),
                  pl.BlockSpec((w_g.shape[0], tn), col),
                  pl.BlockSpec((w_m.shape[0], tn), col)],
        out_specs=pl.BlockSpec((tm, tn), lambda j, i: (i, j)),
        scratch_shapes=[pltpu.VMEM((3, d, tn), BF16)],
        compiler_params=_params("arbitrary", "arbitrary"),
        name="gated_merge",
    )(h, attn, gm, ym, w_in, w_in, w_in, w_a, w_g, w_m)


def _outproj_kernel(mg_ref, w_ref, x_ref, g_ref, x1_ref, h2_ref):
    x1 = x_ref[...] + _dot(mg_ref[...], w_ref[...])
    x1_ref[...] = x1
    h2_ref[...] = _head_rms(x1, g_ref[...]).astype(h2_ref.dtype)


def _outproj(merged, w, x, gain, tm):
    m, d = x.shape
    row = lambda i: (i, 0)
    fixed = lambda i: (0, 0)
    return pl.pallas_call(
        _outproj_kernel,
        out_shape=(jax.ShapeDtypeStruct((m, d), F32), jax.ShapeDtypeStruct((m, d), BF16)),
        grid=(m // tm,),
        in_specs=[pl.BlockSpec((tm, d), row), pl.BlockSpec((d, d), fixed),
                  pl.BlockSpec((tm, d), row), pl.BlockSpec((1, d), fixed)],
        out_specs=(pl.BlockSpec((tm, d), row), pl.BlockSpec((tm, d), row)),
        compiler_params=_params("parallel"),
        name="out_proj",
    )(merged, w, x, gain.reshape(1, d))


def _ffn_up_kernel(h_ref, w_ref, o_ref):
    z = jnp.maximum(_dot(h_ref[...], w_ref[...]), 0.0)
    o_ref[...] = (z * z).astype(o_ref.dtype)


def _ffn_up(h, w, tm, tn):
    m, d = h.shape
    n = w.shape[1]
    return pl.pallas_call(
        _ffn_up_kernel,
        out_shape=jax.ShapeDtypeStruct((m, n), BF16),
        grid=(m // tm, n // tn),
        in_specs=[pl.BlockSpec((tm, d), lambda i, j: (i, 0)),
                  pl.BlockSpec((d, tn), lambda i, j: (0, j))],
        out_specs=pl.BlockSpec((tm, tn), lambda i, j: (i, j)),
        compiler_params=_params("parallel", "parallel"),
        name="ffn_up",
    )(h, w)


def _ffn_down_kernel(a_ref, w_ref, x_ref, o_ref):
    k = pl.program_id(2)

    @pl.when(k == 0)
    def _():
        o_ref[...] = x_ref[...] + _dot(a_ref[...], w_ref[...])

    @pl.when(k != 0)
    def _():
        o_ref[...] += _dot(a_ref[...], w_ref[...])


def _ffn_down(a, w, x1, tm, tn, tk):
    m, kdim = a.shape
    n = w.shape[1]
    return pl.pallas_call(
        _ffn_down_kernel,
        out_shape=jax.ShapeDtypeStruct((m, n), F32),
        grid=(m // tm, n // tn, kdim // tk),
        in_specs=[pl.BlockSpec((tm, tk), lambda i, j, k: (i, k)),
                  pl.BlockSpec((tk, tn), lambda i, j, k: (k, j)),
                  pl.BlockSpec((tm, tn), lambda i, j, k: (i, j))],
        out_specs=pl.BlockSpec((tm, tn), lambda i, j, k: (i, j)),
        compiler_params=_params("parallel", "parallel", "arbitrary"),
        name="ffn_down",
    )(a, w, x1)


def _rope_tables(s):
    inv = ROPE_THETA ** (-jnp.arange(ROPE_PAIRS, dtype=F32) / ROPE_PAIRS)
    inv_lane = jnp.tile(inv, HEAD_DIM // ROPE_PAIRS)
    lane_grp = jnp.arange(HEAD_DIM) // ROPE_PAIRS
    is_row = (lane_grp % 2) == 0
    sign = jnp.where(lane_grp < 2, -1.0, 1.0).astype(F32)

    def tables(n):
        ang = jnp.arange(n, dtype=F32)[:, None] * inv_lane
        return jnp.cos(ang), jnp.sin(ang) * sign

    cr, sr = tables(s // GRID_W)
    cc, sc = tables(GRID_W)
    return (jnp.where(is_row, cr, 0.0), jnp.where(is_row, sr, 0.0),
            jnp.where(is_row, 0.0, cc), jnp.where(is_row, 0.0, sc))


def kernel(x, mem, norm_mix, w_in, q_norm, k_norm, sgu_norm, w_spatial, b_spatial, mem_norm,
           w_mem_kv, mq_norm, mk_norm, w_attn_o, w_gmlp_o, w_mem_o, w_out, norm_ffn,
           w_ffn_up, w_ffn_down):
    b, s, d = x.shape
    assert b == 1 and d == D_MODEL and norm_mix.shape[0] == 1
    xs = x[0]
    w_in0 = w_in[0]
    tables = _rope_tables(s)
    perm = jnp.asarray(ROPE_PERM)
    gq = q_norm[0][perm] * Q_SCALE
    gk = k_norm[0][perm]
    bias_full = jnp.repeat(b_spatial[0].T, HEAD_DIM, axis=1)

    h, kt, v = _kvproj(xs, norm_mix[0], w_in0, gk, tables, TM_PROJ)
    q = _qproj(h, w_in0, gq, tables, TM_PROJ, ATTN_TQ)
    gm, w_gmlp_o_b = _gmlp(h, w_in0, sgu_norm[0], w_spatial[0], bias_full, w_gmlp_o[0], TM_PROJ)
    km_t, vm = _memkv(mem[0], mem_norm[0], w_mem_kv[0], mk_norm[0])
    ym, w_mem_o_b = _memattn(h, w_in0, mq_norm[0] * Q_SCALE, km_t, vm, w_mem_o[0], TM_PROJ)

    logit_bound = (1.02 * LOG2E * math.sqrt(HEAD_DIM)
                   * jnp.max(jnp.abs(q_norm[0])) * jnp.max(jnp.abs(k_norm[0])))
    attn, w_attn_o_b, w_out_b, w_up_b, w_down_b = lax.cond(
        logit_bound <= MAX_UNSHIFTED_LOGIT,
        functools.partial(_attention, unshifted=True),
        functools.partial(_attention, unshifted=False),
        q, kt, v, w_attn_o[0], w_out[0], w_ffn_up[0], w_ffn_down[0])

    merged = _gated_merge(h, attn, gm, ym, w_in0, w_attn_o_b, w_gmlp_o_b, w_mem_o_b,
                          TM_MERGE, TN_MERGE)
    x1, h2 = _outproj(merged, w_out_b, xs, norm_ffn[0], TM_OUT)
    a = _ffn_up(h2, w_up_b, TM_UP, TN_UP)
    out = _ffn_down(a, w_down_b, x1, TM_DOWN, TN_DOWN, TK_DOWN)
    return out[None]
```

```python
import functools
import math

import jax
import jax.numpy as jnp
from jax import lax
from jax.experimental import pallas as pl
from jax.experimental.pallas import tpu as pltpu

D_MODEL = 2048
HEAD_DIM = 128
N_Q_HEADS = 8
N_KV_HEADS = 2
Q_PER_KV = N_Q_HEADS // N_KV_HEADS
GRID_W = 64
ROPE_THETA = 10000.0
ROPE_PAIRS = HEAD_DIM // 4
GMLP_GROUPS = 4
GMLP_WIDTH = GMLP_GROUPS * HEAD_DIM
CHUNK = 128
N_MEM_HEADS = 4
MEM_WIDTH = N_MEM_HEADS * HEAD_DIM
D_FF = 4 * D_MODEL
EPS = 1e-6
ATTN_Q_W = N_Q_HEADS * HEAD_DIM
ATTN_KV_W = N_KV_HEADS * HEAD_DIM
IN_COL_BLOCK = 512
IN_Q_BLK, IN_KV_BLK, IN_U_BLK, IN_V_BLK, IN_QM_BLK, IN_GATE_BLK = 0, 2, 3, 4, 5, 6

VMEM_LIMIT_BYTES = 56 * 1024 * 1024
LOG2E = math.log2(math.e)
Q_SCALE = HEAD_DIM ** -0.5 * LOG2E
KV_BLOCK = 256
ATTN_TQ = 512
MAX_UNSHIFTED_LOGIT = 64.0
ROPE_PERM = tuple(list(range(0, 32)) + list(range(64, 96)) + list(range(32, 64))
                  + list(range(96, 128)))

TM_PROJ = 512
TM_MERGE, TN_MERGE = 512, 512
TM_OUT = 512
TM_UP, TN_UP = 2048, 1024
TM_DOWN, TN_DOWN, TK_DOWN = 1024, 1024, 4096

F32 = jnp.float32
BF16 = jnp.bfloat16


def _params(*sem):
    return pltpu.CompilerParams(dimension_semantics=sem, vmem_limit_bytes=VMEM_LIMIT_BYTES)


def _dot(a, b):
    return jnp.dot(a, b, preferred_element_type=F32)


def _head_rms(x, gain):
    ms = jnp.mean(x * x, axis=-1, keepdims=True)
    return x * lax.rsqrt(ms + EPS) * gain


def _rope(y, cos, sin_signed):
    return y * cos + pltpu.roll(y, HEAD_DIM // 2, 1) * sin_signed


def _to_rope_order(w):
    grp = lax.broadcasted_iota(jnp.int32, w.shape, 1) // ROPE_PAIRS
    from_right = pltpu.roll(w, HEAD_DIM - ROPE_PAIRS, 1)
    from_left = pltpu.roll(w, ROPE_PAIRS, 1)
    return jnp.where(grp == 1, from_right, jnp.where(grp == 2, from_left, w))


def _rope_tile(row_ref, col_ref, i, tm):
    nrow = tm // GRID_W
    rows = row_ref[pl.ds(pl.multiple_of(i * nrow, nrow), nrow), :]
    col = col_ref[...]
    return jnp.concatenate([rows[r:r + 1, :] + col for r in range(nrow)], axis=0)


def _lagged_step(i, project, finish, buf_a, buf_b):
    @pl.when(i % 2 == 0)
    def _():
        project(buf_a)
        finish(buf_b)

    @pl.when(i % 2 == 1)
    def _():
        project(buf_b)
        finish(buf_a)


def _lag_index_maps(n_tiles):
    return (lambda i: jnp.minimum(i, n_tiles - 1)), (lambda i: jnp.maximum(i - 1, 0))


def _cast_side_job(pairs):
    for src, dst in pairs:
        dst[...] = src[...].astype(dst.dtype)


def _side_specs(arrays, nsteps, index):
    in_specs, out_specs, out_shapes = [], [], []
    for a in arrays:
        rows, cols = a.shape
        blk = (rows // nsteps, cols)
        in_specs.append(pl.BlockSpec(blk, index))
        out_specs.append(pl.BlockSpec(blk, index))
        out_shapes.append(jax.ShapeDtypeStruct(a.shape, BF16))
    return in_specs, out_specs, out_shapes


def _qkvproj_kernel(x_ref, gx_ref, wkv_ref, wq_ref, gk_ref, gq_ref, crow_ref, srow_ref, ccol_ref,
                    scol_ref, h_ref, kt_ref, v_ref, q_ref, wkvb_ref, wqb_ref,
                    kv_a, kv_b, q_a, q_b, *, tm, tq):
    i = pl.program_id(0)

    @pl.when(i == 0)
    def _():
        for hd in range(N_KV_HEADS):
            sl = slice(hd * HEAD_DIM, (hd + 1) * HEAD_DIM)
            wkvb_ref[:, sl] = _to_rope_order(wkv_ref[:, sl]).astype(BF16)
        wkvb_ref[:, ATTN_KV_W:] = wkv_ref[:, ATTN_KV_W:].astype(BF16)
        for hd in range(N_Q_HEADS):
            sl = slice(hd * HEAD_DIM, (hd + 1) * HEAD_DIM)
            wqb_ref[:, sl] = _to_rope_order(wq_ref[:, sl]).astype(BF16)
        kv_b[...] = jnp.zeros_like(kv_b)
        q_b[...] = jnp.zeros_like(q_b)

    def project(bufs):
        kv_acc, q_acc = bufs
        h = _head_rms(x_ref[...], gx_ref[...]).astype(BF16)
        h_ref[...] = h
        kv_acc[...] = _dot(h, wkvb_ref[...])
        q_acc[...] = _dot(h, wqb_ref[...])

    def finish(bufs):
        kv_acc, q_acc = bufs
        prev = jnp.maximum(i - 1, 0)
        cos = _rope_tile(crow_ref, ccol_ref, prev, tm)
        sin = _rope_tile(srow_ref, scol_ref, prev, tm)
        gk = gk_ref[...]
        for hd in range(N_KV_HEADS):
            sl = slice(hd * HEAD_DIM, (hd + 1) * HEAD_DIM)
            y = _rope(_head_rms(kv_acc[:, sl], gk), cos, sin)
            for c in range(tm // KV_BLOCK):
                blk = y[c * KV_BLOCK:(c + 1) * KV_BLOCK, :]
                kt_ref[c, sl, :] = jnp.transpose(blk).astype(kt_ref.dtype)
        v_ref[...] = kv_acc[:, ATTN_KV_W:].astype(v_ref.dtype)
        gq = gq_ref[...]
        for hd in range(N_Q_HEADS):
            grp, r = divmod(hd, Q_PER_KV)
            x = q_acc[:, hd * HEAD_DIM:(hd + 1) * HEAD_DIM]
            y = _rope(_head_rms(x, gq), cos, sin).astype(q_ref.dtype)
            for t in range(tm // tq):
                q_ref[t, grp, r * tq:(r + 1) * tq, :] = y[t * tq:(t + 1) * tq]

    _lagged_step(i, project, finish, (kv_a, q_a), (kv_b, q_b))


def _qkvproj(x, gx, w_in, gk, gq, tables, tm, tq):
    m, d = x.shape
    n_tiles = m // tm
    cur, prev = _lag_index_maps(n_tiles)
    fixed = lambda i: (0, 0)
    nkv = 2 * ATTN_KV_W
    tab_specs = [pl.BlockSpec(t.shape, fixed) for t in tables]
    qshape = (m // tq, N_KV_HEADS, Q_PER_KV * tq, HEAD_DIM)
    once = pl.Buffered(1)
    return pl.pallas_call(
        functools.partial(_qkvproj_kernel, tm=tm, tq=tq),
        out_shape=(jax.ShapeDtypeStruct((m, d), BF16),
                   jax.ShapeDtypeStruct((m // KV_BLOCK, ATTN_KV_W, KV_BLOCK), BF16),
                   jax.ShapeDtypeStruct((m, ATTN_KV_W), BF16),
                   jax.ShapeDtypeStruct(qshape, BF16)),
        grid=(n_tiles + 1,),
        in_specs=[pl.BlockSpec((tm, d), lambda i: (cur(i), 0)), pl.BlockSpec((1, d), fixed),
                  pl.BlockSpec((d, nkv), lambda i: (0, IN_KV_BLK), pipeline_mode=once),
                  pl.BlockSpec((d, ATTN_Q_W), lambda i: (0, IN_Q_BLK), pipeline_mode=once),
                  pl.BlockSpec((1, HEAD_DIM), fixed),
                  pl.BlockSpec((1, HEAD_DIM), fixed)] + tab_specs,
        out_specs=(pl.BlockSpec((tm, d), lambda i: (cur(i), 0)),
                   pl.BlockSpec((tm // KV_BLOCK, ATTN_KV_W, KV_BLOCK),
                                lambda i: (prev(i), 0, 0)),
                   pl.BlockSpec((tm, ATTN_KV_W), lambda i: (prev(i), 0)),
                   pl.BlockSpec((tm // tq,) + qshape[1:], lambda i: (prev(i), 0, 0, 0))),
        scratch_shapes=[pltpu.VMEM((d, nkv), BF16), pltpu.VMEM((d, ATTN_Q_W), BF16),
                        pltpu.VMEM((tm, nkv), F32), pltpu.VMEM((tm, nkv), F32),
                        pltpu.VMEM((tm, ATTN_Q_W), F32), pltpu.VMEM((tm, ATTN_Q_W), F32)],
        compiler_params=_params("arbitrary"),
        name="qkv_proj",
    )(x, gx.reshape(1, d), w_in, w_in, gk.reshape(1, HEAD_DIM), gq.reshape(1, HEAD_DIM),
      *tables)


def _gmlp_kernel(h_ref, wu_ref, wv_ref, g_ref, ws_ref, b_ref, wo_ref, o_ref, wob_ref, wb_ref,
                 *, tm):
    @pl.when(pl.program_id(0) == 0)
    def _():
        wb_ref[:, :GMLP_WIDTH] = wu_ref[...].astype(BF16)
        wb_ref[:, GMLP_WIDTH:] = wv_ref[...].astype(BF16)

    _cast_side_job([(wo_ref, wob_ref)])
    z = jax.nn.gelu(_dot(h_ref[...], wb_ref[...]))
    u = z[:, :GMLP_WIDTH]
    vn = _head_rms(z[:, GMLP_WIDTH:], g_ref[...]).astype(BF16)
    bias = b_ref[...]
    for c in range(tm // CHUNK):
        rows = slice(c * CHUNK, (c + 1) * CHUNK)
        for grp in range(GMLP_GROUPS):
            cols = slice(grp * HEAD_DIM, (grp + 1) * HEAD_DIM)
            mixed = _dot(ws_ref[grp].astype(BF16), vn[rows, cols]) + bias[:, cols]
            o_ref[rows, cols] = (u[rows, cols] * mixed).astype(o_ref.dtype)


def _gmlp(h, w_in, sgu_gain, w_spatial, bias_full, w_o, tm):
    m, d = h.shape
    row = lambda i: (i, 0)
    fixed = lambda i: (0, 0)
    side_in, side_out, side_shapes = _side_specs([w_o], m // tm, row)
    return pl.pallas_call(
        functools.partial(_gmlp_kernel, tm=tm),
        out_shape=[jax.ShapeDtypeStruct((m, GMLP_WIDTH), BF16)] + side_shapes,
        grid=(m // tm,),
        in_specs=[pl.BlockSpec((tm, d), row),
                  pl.BlockSpec((d, GMLP_WIDTH), lambda i: (0, IN_U_BLK)),
                  pl.BlockSpec((d, GMLP_WIDTH), lambda i: (0, IN_V_BLK)),
                  pl.BlockSpec((1, GMLP_WIDTH), fixed),
                  pl.BlockSpec((GMLP_GROUPS, CHUNK, CHUNK), lambda i: (0, 0, 0)),
                  pl.BlockSpec((CHUNK, GMLP_WIDTH), fixed)] + side_in,
        out_specs=[pl.BlockSpec((tm, GMLP_WIDTH), row)] + side_out,
        scratch_shapes=[pltpu.VMEM((d, 2 * GMLP_WIDTH), BF16)],
        compiler_params=_params("arbitrary"),
        name="gmlp",
    )(h, w_in, w_in, sgu_gain.reshape(1, GMLP_WIDTH), w_spatial, bias_full, w_o)


def _memkv_kernel(mem_ref, g_ref, w_ref, kg_ref, kt_ref, v_ref):
    mem_n = _head_rms(mem_ref[...], g_ref[...]).astype(BF16)
    kv = _dot(mem_n, w_ref[...].astype(BF16))
    kg = kg_ref[...]
    for hd in range(N_MEM_HEADS):
        sl = slice(hd * HEAD_DIM, (hd + 1) * HEAD_DIM)
        kt_ref[sl, :] = jnp.transpose(_head_rms(kv[:, sl], kg)).astype(kt_ref.dtype)
    v_ref[...] = kv[:, MEM_WIDTH:].astype(v_ref.dtype)


def _memkv(mem, mem_gain, w, mk_gain):
    mt, d = mem.shape
    return pl.pallas_call(
        _memkv_kernel,
        out_shape=(jax.ShapeDtypeStruct((MEM_WIDTH, mt), BF16),
                   jax.ShapeDtypeStruct((mt, MEM_WIDTH), BF16)),
        compiler_params=pltpu.CompilerParams(vmem_limit_bytes=VMEM_LIMIT_BYTES),
        name="mem_kv",
    )(mem, mem_gain.reshape(1, d), w, mk_gain.reshape(1, HEAD_DIM))


def _memattn_kernel(h_ref, w_ref, g_ref, kt_ref, v_ref, wo_ref, o_ref, wob_ref, wb_ref):
    @pl.when(pl.program_id(0) == 0)
    def _():
        wb_ref[...] = w_ref[...].astype(BF16)

    _cast_side_job([(wo_ref, wob_ref)])
    acc = _dot(h_ref[...], wb_ref[...])
    g = g_ref[...]
    for hd in range(N_MEM_HEADS):
        sl = slice(hd * HEAD_DIM, (hd + 1) * HEAD_DIM)
        qn = _head_rms(acc[:, sl], g).astype(BF16)
        s = _dot(qn, kt_ref[sl, :])
        p = jnp.exp2(s - jnp.max(s, axis=-1, keepdims=True))
        l = jnp.sum(p, axis=-1, keepdims=True)
        o = _dot(p.astype(BF16), v_ref[:, sl])
        o_ref[:, sl] = (o / l).astype(o_ref.dtype)


def _memattn(h, w_in, mq_gain, km_t, vm, w_o, tm):
    m, d = h.shape
    mt = vm.shape[0]
    row = lambda i: (i, 0)
    fixed = lambda i: (0, 0)
    side_in, side_out, side_shapes = _side_specs([w_o], m // tm, row)
    return pl.pallas_call(
        _memattn_kernel,
        out_shape=[jax.ShapeDtypeStruct((m, MEM_WIDTH), BF16)] + side_shapes,
        grid=(m // tm,),
        in_specs=[pl.BlockSpec((tm, d), row),
                  pl.BlockSpec((d, MEM_WIDTH), lambda i: (0, IN_QM_BLK)),
                  pl.BlockSpec((1, HEAD_DIM), fixed),
                  pl.BlockSpec((MEM_WIDTH, mt), fixed),
                  pl.BlockSpec((mt, MEM_WIDTH), fixed)] + side_in,
        out_specs=[pl.BlockSpec((tm, MEM_WIDTH), row)] + side_out,
        scratch_shapes=[pltpu.VMEM((d, MEM_WIDTH), BF16)],
        compiler_params=_params("arbitrary"),
        name="mem_attn",
    )(h, w_in, mq_gain.reshape(1, HEAD_DIM), km_t, vm, w_o)


def _store_heads(o_ref, out, tq):
    for r in range(Q_PER_KV):
        o_ref[:, r * HEAD_DIM:(r + 1) * HEAD_DIM] = out[r * tq:(r + 1) * tq].astype(o_ref.dtype)


def _attn_unshifted_kernel(q_ref, kt_ref, v_ref, *rest, tq, nk, nside):
    side = rest[:nside]
    o_ref = rest[nside]
    side_out = rest[nside + 1:2 * nside + 1]
    l_ref, acc_ref = rest[2 * nside + 1:]
    _cast_side_job(zip(side, side_out))
    l_ref[...] = jnp.zeros_like(l_ref)
    acc_ref[...] = jnp.zeros_like(acc_ref)

    def body(j, carry):
        off = pl.multiple_of(j * KV_BLOCK, KV_BLOCK)
        p = jnp.exp2(_dot(q_ref[0, 0], kt_ref[j]))
        l_ref[...] += p[:, :HEAD_DIM] + p[:, HEAD_DIM:]
        acc_ref[...] += _dot(p.astype(BF16), v_ref[pl.ds(off, KV_BLOCK), :])
        return carry

    lax.fori_loop(0, nk, body, 0, unroll=True)
    l = jnp.sum(l_ref[...], axis=-1, keepdims=True)
    _store_heads(o_ref, acc_ref[...] / l, tq)


def _attn_online_kernel(q_ref, kt_ref, v_ref, *rest, tq, nk, nside):
    side = rest[:nside]
    o_ref = rest[nside]
    side_out = rest[nside + 1:2 * nside + 1]
    _cast_side_job(zip(side, side_out))
    q = q_ref[0, 0]
    rows = Q_PER_KV * tq

    def body(j, carry):
        m, l, acc = carry
        off = pl.multiple_of(j * KV_BLOCK, KV_BLOCK)
        s = _dot(q, kt_ref[j])
        m_new = jnp.maximum(m, jnp.max(s, axis=-1, keepdims=True))
        alpha = jnp.exp2(m - m_new)
        p = jnp.exp2(s - m_new)
        l = alpha * l + jnp.sum(p, axis=-1, keepdims=True)
        acc = alpha * acc + _dot(p.astype(BF16), v_ref[pl.ds(off, KV_BLOCK), :])
        return m_new, l, acc

    m0 = jnp.full((rows, 1), -jnp.inf, F32)
    l0 = jnp.zeros((rows, 1), F32)
    acc0 = jnp.zeros((rows, HEAD_DIM), F32)
    _, l, acc = lax.fori_loop(0, nk, body, (m0, l0, acc0))
    _store_heads(o_ref, acc / l, tq)


def _attention(q, kt, v, *side_weights, unshifted):
    nq, _, rows, _ = q.shape
    tq = rows // Q_PER_KV
    s = v.shape[0]
    nk = s // KV_BLOCK
    gw = Q_PER_KV * HEAD_DIM
    nside = len(side_weights)
    side_in, side_out, side_shapes = _side_specs(side_weights, N_KV_HEADS * nq,
                                                 lambda g, i: (g * nq + i, 0))
    if unshifted:
        body = functools.partial(_attn_unshifted_kernel, tq=tq, nk=nk, nside=nside)
        scratch = [pltpu.VMEM((rows, HEAD_DIM), F32), pltpu.VMEM((rows, HEAD_DIM), F32)]
        name = "attn_unshifted"
    else:
        body = functools.partial(_attn_online_kernel, tq=tq, nk=nk, nside=nside)
        scratch = []
        name = "attn_online"
    return pl.pallas_call(
        body,
        out_shape=[jax.ShapeDtypeStruct((s, ATTN_Q_W), BF16)] + side_shapes,
        grid=(N_KV_HEADS, nq),
        in_specs=[pl.BlockSpec((1, 1, rows, HEAD_DIM), lambda g, i: (i, g, 0, 0)),
                  pl.BlockSpec((nk, HEAD_DIM, KV_BLOCK), lambda g, i: (0, g, 0)),
                  pl.BlockSpec((s, HEAD_DIM), lambda g, i: (0, g))] + side_in,
        out_specs=[pl.BlockSpec((tq, gw), lambda g, i: (i, g))] + side_out,
        scratch_shapes=scratch,
        compiler_params=_params("arbitrary", "arbitrary"),
        name=name,
    )(q, kt, v, *side_weights)


def _gated_merge_kernel(h_ref, a_ref, gm_ref, ym_ref, wga_ref, wgg_ref, wgm_ref,
                        wa_ref, wg_ref, wm_ref, o_ref, wb_ref):
    @pl.when(pl.program_id(1) == 0)
    def _():
        for b, w_ref in enumerate((wga_ref, wgg_ref, wgm_ref)):
            wb_ref[b] = w_ref[...].astype(BF16)

    h = h_ref[...]
    y = None
    for b, (y_ref, w_ref) in enumerate(((a_ref, wa_ref), (gm_ref, wg_ref), (ym_ref, wm_ref))):
        gate = 1.0 / (1.0 + jnp.exp(-_dot(h, wb_ref[b])))
        term = gate * _dot(y_ref[...], w_ref[...])
        y = term if y is None else y + term
    o_ref[...] = y.astype(o_ref.dtype)


def _gated_merge(h, attn, gm, ym, w_in, w_a, w_g, w_m, tm, tn):
    m, d = h.shape
    n = w_a.shape[1]
    nb = n // tn
    first = IN_GATE_BLK * IN_COL_BLOCK // tn
    row = lambda j, i: (i, 0)
    col = lambda j, i: (0, j)
    return pl.pallas_call(
        _gated_merge_kernel,
        out_shape=jax.ShapeDtypeStruct((m, n), BF16),
        grid=(nb, m // tm),
        in_specs=[pl.BlockSpec((tm, d), row),
                  pl.BlockSpec((tm, attn.shape[1]), row),
                  pl.BlockSpec((tm, gm.shape[1]), row),
                  pl.BlockSpec((tm, ym.shape[1]), row),
                  pl.BlockSpec((d, tn), lambda j, i: (0, first + j)),
                  pl.BlockSpec((d, tn), lambda j, i: (0, first + nb + j)),
                  pl.BlockSpec((d, tn), lambda j, i: (0, first + 2 * nb + j)),
                  pl.BlockSpec((w_a.shape[0], tn), col),
                  pl.BlockSpec((w_g.shape[0], tn), col),
                  pl.BlockSpec((w_m.shape[0], tn), col)],
        out_specs=pl.BlockSpec((tm, tn), lambda j, i: (i, j)),
        scratch_shapes=[pltpu.VMEM((3, d, tn), BF16)],
        compiler_params=_params("arbitrary", "arbitrary"),
        name="gated_merge",
    )(h, attn, gm, ym, w_in, w_in, w_in, w_a, w_g, w_m)


def _outproj_kernel(mg_ref, w_ref, x_ref, g_ref, x1_ref, h2_ref):
    x1 = x_ref[...] + _dot(mg_ref[...], w_ref[...])
    x1_ref[...] = x1
    h2_ref[...] = _head_rms(x1, g_ref[...]).astype(h2_ref.dtype)


def _outproj(merged, w, x, gain, tm):
    m, d = x.shape
    row = lambda i: (i, 0)
    fixed = lambda i: (0, 0)
    return pl.pallas_call(
        _outproj_kernel,
        out_shape=(jax.ShapeDtypeStruct((m, d), F32), jax.ShapeDtypeStruct((m, d), BF16)),
        grid=(m // tm,),
        in_specs=[pl.BlockSpec((tm, d), row), pl.BlockSpec((d, d), fixed),
                  pl.BlockSpec((tm, d), row), pl.BlockSpec((1, d), fixed)],
        out_specs=(pl.BlockSpec((tm, d), row), pl.BlockSpec((tm, d), row)),
        compiler_params=_params("parallel"),
        name="out_proj",
    )(merged, w, x, gain.reshape(1, d))


def _ffn_up_kernel(h_ref, w_ref, o_ref):
    z = jnp.maximum(_dot(h_ref[...], w_ref[...]), 0.0)
    o_ref[...] = (z * z).astype(o_ref.dtype)


def _ffn_up(h, w, tm, tn):
    m, d = h.shape
    n = w.shape[1]
    return pl.pallas_call(
        _ffn_up_kernel,
        out_shape=jax.ShapeDtypeStruct((m, n), BF16),
        grid=(m // tm, n // tn),
        in_specs=[pl.BlockSpec((tm, d), lambda i, j: (i, 0)),
                  pl.BlockSpec((d, tn), lambda i, j: (0, j))],
        out_specs=pl.BlockSpec((tm, tn), lambda i, j: (i, j)),
        compiler_params=_params("parallel", "parallel"),
        name="ffn_up",
    )(h, w)


def _ffn_down_kernel(a_ref, w_ref, x_ref, o_ref):
    k = pl.program_id(2)

    @pl.when(k == 0)
    def _():
        o_ref[...] = x_ref[...] + _dot(a_ref[...], w_ref[...])

    @pl.when(k != 0)
    def _():
        o_ref[...] += _dot(a_ref[...], w_ref[...])


def _ffn_down(a, w, x1, tm, tn, tk):
    m, kdim = a.shape
    n = w.shape[1]
    return pl.pallas_call(
        _ffn_down_kernel,
        out_shape=jax.ShapeDtypeStruct((m, n), F32),
        grid=(m // tm, n // tn, kdim // tk),
        in_specs=[pl.BlockSpec((tm, tk), lambda i, j, k: (i, k)),
                  pl.BlockSpec((tk, tn), lambda i, j, k: (k, j)),
                  pl.BlockSpec((tm, tn), lambda i, j, k: (i, j))],
        out_specs=pl.BlockSpec((tm, tn), lambda i, j, k: (i, j)),
        compiler_params=_params("parallel", "parallel", "arbitrary"),
        name="ffn_down",
    )(a, w, x1)


def _rope_tables(s):
    inv = ROPE_THETA ** (-jnp.arange(ROPE_PAIRS, dtype=F32) / ROPE_PAIRS)
    inv_lane = jnp.tile(inv, HEAD_DIM // ROPE_PAIRS)
    lane_grp = jnp.arange(HEAD_DIM) // ROPE_PAIRS
    is_row = (lane_grp % 2) == 0
    sign = jnp.where(lane_grp < 2, -1.0, 1.0).astype(F32)

    def tables(n):
        ang = jnp.arange(n, dtype=F32)[:, None] * inv_lane
        return jnp.cos(ang), jnp.sin(ang) * sign

    cr, sr = tables(s // GRID_W)
    cc, sc = tables(GRID_W)
    return (jnp.where(is_row, cr, 0.0), jnp.where(is_row, sr, 0.0),
            jnp.where(is_row, 0.0, cc), jnp.where(is_row, 0.0, sc))


def kernel(x, mem, norm_mix, w_in, q_norm, k_norm, sgu_norm, w_spatial, b_spatial, mem_norm,
           w_mem_kv, mq_norm, mk_norm, w_attn_o, w_gmlp_o, w_mem_o, w_out, norm_ffn,
           w_ffn_up, w_ffn_down):
    b, s, d = x.shape
    assert b == 1 and d == D_MODEL and norm_mix.shape[0] == 1
    xs = x[0]
    w_in0 = w_in[0]
    tables = _rope_tables(s)
    perm = jnp.asarray(ROPE_PERM)
    gq = q_norm[0][perm] * Q_SCALE
    gk = k_norm[0][perm]
    bias_full = jnp.repeat(b_spatial[0].T, HEAD_DIM, axis=1)

    h, kt, v, q = _qkvproj(xs, norm_mix[0], w_in0, gk, gq, tables, TM_PROJ, ATTN_TQ)
    gm, w_gmlp_o_b = _gmlp(h, w_in0, sgu_norm[0], w_spatial[0], bias_full, w_gmlp_o[0], TM_PROJ)
    km_t, vm = _memkv(mem[0], mem_norm[0], w_mem_kv[0], mk_norm[0])
    ym, w_mem_o_b = _memattn(h, w_in0, mq_norm[0] * Q_SCALE, km_t, vm, w_mem_o[0], TM_PROJ)

    logit_bound = (1.02 * LOG2E * math.sqrt(HEAD_DIM)
                   * jnp.max(jnp.abs(q_norm[0])) * jnp.max(jnp.abs(k_norm[0])))
    attn, w_attn_o_b, w_out_b, w_up_b, w_down_b = lax.cond(
        logit_bound <= MAX_UNSHIFTED_LOGIT,
        functools.partial(_attention, unshifted=True),
        functools.partial(_attention, unshifted=False),
        q, kt, v, w_attn_o[0], w_out[0], w_ffn_up[0], w_ffn_down[0])

    merged = _gated_merge(h, attn, gm, ym, w_in0, w_attn_o_b, w_gmlp_o_b, w_mem_o_b,
                          TM_MERGE, TN_MERGE)
    x1, h2 = _outproj(merged, w_out_b, xs, norm_ffn[0], TM_OUT)
    a = _ffn_up(h2, w_up_b, TM_UP, TN_UP)
    out = _ffn_down(a, w_down_b, x1, TM_DOWN, TN_DOWN, TK_DOWN)
    return out[None]
```

```python
import functools
import math

import jax
import jax.numpy as jnp
from jax import lax
from jax.experimental import pallas as pl
from jax.experimental.pallas import tpu as pltpu

D_MODEL = 2048
HEAD_DIM = 128
N_Q_HEADS = 8
N_KV_HEADS = 2
Q_PER_KV = N_Q_HEADS // N_KV_HEADS
GRID_W = 64
ROPE_THETA = 10000.0
ROPE_PAIRS = HEAD_DIM // 4
GMLP_GROUPS = 4
GMLP_WIDTH = GMLP_GROUPS * HEAD_DIM
CHUNK = 128
N_MEM_HEADS = 4
MEM_WIDTH = N_MEM_HEADS * HEAD_DIM
D_FF = 4 * D_MODEL
EPS = 1e-6
ATTN_Q_W = N_Q_HEADS * HEAD_DIM
ATTN_KV_W = N_KV_HEADS * HEAD_DIM
IN_COL_BLOCK = 512
IN_Q_BLK, IN_KV_BLK, IN_U_BLK, IN_V_BLK, IN_QM_BLK, IN_GATE_BLK = 0, 2, 3, 4, 5, 6

VMEM_LIMIT_BYTES = 56 * 1024 * 1024
LOG2E = math.log2(math.e)
Q_SCALE = HEAD_DIM ** -0.5 * LOG2E
KV_BLOCK = 256
ATTN_TQ = 512
MAX_UNSHIFTED_LOGIT = 64.0
ROPE_PERM = tuple(list(range(0, 32)) + list(range(64, 96)) + list(range(32, 64))
                  + list(range(96, 128)))

TM_PROJ = 512
TM_MERGE, TN_MERGE = 512, 512
TM_OUT = 512
TM_UP, TN_UP = 2048, 1024
TM_DOWN, TN_DOWN, TK_DOWN = 1024, 1024, 4096

F32 = jnp.float32
BF16 = jnp.bfloat16


def _params(*sem):
    return pltpu.CompilerParams(dimension_semantics=sem, vmem_limit_bytes=VMEM_LIMIT_BYTES)


def _dot(a, b):
    return jnp.dot(a, b, preferred_element_type=F32)


def _head_rms(x, gain):
    ms = jnp.mean(x * x, axis=-1, keepdims=True)
    return x * lax.rsqrt(ms + EPS) * gain


def _rope(y, cos, sin_signed):
    return y * cos + pltpu.roll(y, HEAD_DIM // 2, 1) * sin_signed


def _to_rope_order(w):
    grp = lax.broadcasted_iota(jnp.int32, w.shape, 1) // ROPE_PAIRS
    from_right = pltpu.roll(w, HEAD_DIM - ROPE_PAIRS, 1)
    from_left = pltpu.roll(w, ROPE_PAIRS, 1)
    return jnp.where(grp == 1, from_right, jnp.where(grp == 2, from_left, w))


def _gain_in_rope_order(g_ref):
    return _to_rope_order(jnp.broadcast_to(g_ref[...], (8, HEAD_DIM)))[0:1]


def _rope_tile(row_ref, col_ref, i, tm):
    nrow = tm // GRID_W
    rows = row_ref[pl.ds(pl.multiple_of(i * nrow, nrow), nrow), :]
    col = col_ref[...]
    return jnp.concatenate([rows[r:r + 1, :] + col for r in range(nrow)], axis=0)


def _lagged_step(i, project, finish, buf_a, buf_b):
    @pl.when(i % 2 == 0)
    def _():
        project(buf_a)
        finish(buf_b)

    @pl.when(i % 2 == 1)
    def _():
        project(buf_b)
        finish(buf_a)


def _lag_index_maps(n_tiles):
    return (lambda i: jnp.minimum(i, n_tiles - 1)), (lambda i: jnp.maximum(i - 1, 0))


def _cast_side_job(pairs):
    for src, dst in pairs:
        dst[...] = src[...].astype(dst.dtype)


def _side_specs(arrays, nsteps, index):
    in_specs, out_specs, out_shapes = [], [], []
    for a in arrays:
        rows, cols = a.shape
        blk = (rows // nsteps, cols)
        in_specs.append(pl.BlockSpec(blk, index))
        out_specs.append(pl.BlockSpec(blk, index))
        out_shapes.append(jax.ShapeDtypeStruct(a.shape, BF16))
    return in_specs, out_specs, out_shapes


def _qkvproj_kernel(x_ref, gx_ref, wkv_ref, wq_ref, gk_ref, gq_ref, crow_ref, srow_ref, ccol_ref,
                    scol_ref, h_ref, kt_ref, v_ref, q_ref, wkvb_ref, wqb_ref,
                    kv_a, kv_b, q_a, q_b, *, tm, tq):
    i = pl.program_id(0)

    @pl.when(i == 0)
    def _():
        for hd in range(N_KV_HEADS):
            sl = slice(hd * HEAD_DIM, (hd + 1) * HEAD_DIM)
            wkvb_ref[:, sl] = _to_rope_order(wkv_ref[:, sl]).astype(BF16)
        wkvb_ref[:, ATTN_KV_W:] = wkv_ref[:, ATTN_KV_W:].astype(BF16)
        for hd in range(N_Q_HEADS):
            sl = slice(hd * HEAD_DIM, (hd + 1) * HEAD_DIM)
            wqb_ref[:, sl] = _to_rope_order(wq_ref[:, sl]).astype(BF16)
        kv_b[...] = jnp.zeros_like(kv_b)
        q_b[...] = jnp.zeros_like(q_b)

    def project(bufs):
        kv_acc, q_acc = bufs
        h = _head_rms(x_ref[...], gx_ref[...]).astype(BF16)
        h_ref[...] = h
        kv_acc[...] = _dot(h, wkvb_ref[...])
        q_acc[...] = _dot(h, wqb_ref[...])

    def finish(bufs):
        kv_acc, q_acc = bufs
        prev = jnp.maximum(i - 1, 0)
        cos = _rope_tile(crow_ref, ccol_ref, prev, tm)
        sin = _rope_tile(srow_ref, scol_ref, prev, tm)
        gk = _gain_in_rope_order(gk_ref)
        for hd in range(N_KV_HEADS):
            sl = slice(hd * HEAD_DIM, (hd + 1) * HEAD_DIM)
            y = _rope(_head_rms(kv_acc[:, sl], gk), cos, sin)
            for c in range(tm // KV_BLOCK):
                blk = y[c * KV_BLOCK:(c + 1) * KV_BLOCK, :]
                kt_ref[c, sl, :] = jnp.transpose(blk).astype(kt_ref.dtype)
        v_ref[...] = kv_acc[:, ATTN_KV_W:].astype(v_ref.dtype)
        gq = _gain_in_rope_order(gq_ref) * Q_SCALE
        for hd in range(N_Q_HEADS):
            grp, r = divmod(hd, Q_PER_KV)
            x = q_acc[:, hd * HEAD_DIM:(hd + 1) * HEAD_DIM]
            y = _rope(_head_rms(x, gq), cos, sin).astype(q_ref.dtype)
            for t in range(tm // tq):
                q_ref[t, grp, r * tq:(r + 1) * tq, :] = y[t * tq:(t + 1) * tq]

    _lagged_step(i, project, finish, (kv_a, q_a), (kv_b, q_b))


def _qkvproj(x, gx, w_in, gk, gq, tables, tm, tq):
    m, d = x.shape
    n_tiles = m // tm
    cur, prev = _lag_index_maps(n_tiles)
    fixed = lambda i: (0, 0)
    nkv = 2 * ATTN_KV_W
    tab_specs = [pl.BlockSpec(t.shape, fixed) for t in tables]
    qshape = (m // tq, N_KV_HEADS, Q_PER_KV * tq, HEAD_DIM)
    once = pl.Buffered(1)
    return pl.pallas_call(
        functools.partial(_qkvproj_kernel, tm=tm, tq=tq),
        out_shape=(jax.ShapeDtypeStruct((m, d), BF16),
                   jax.ShapeDtypeStruct((m // KV_BLOCK, ATTN_KV_W, KV_BLOCK), BF16),
                   jax.ShapeDtypeStruct((m, ATTN_KV_W), BF16),
                   jax.ShapeDtypeStruct(qshape, BF16)),
        grid=(n_tiles + 1,),
        in_specs=[pl.BlockSpec((tm, d), lambda i: (cur(i), 0)), pl.BlockSpec((1, d), fixed),
                  pl.BlockSpec((d, nkv), lambda i: (0, IN_KV_BLK), pipeline_mode=once),
                  pl.BlockSpec((d, ATTN_Q_W), lambda i: (0, IN_Q_BLK), pipeline_mode=once),
                  pl.BlockSpec((1, HEAD_DIM), fixed),
                  pl.BlockSpec((1, HEAD_DIM), fixed)] + tab_specs,
        out_specs=(pl.BlockSpec((tm, d), lambda i: (cur(i), 0)),
                   pl.BlockSpec((tm // KV_BLOCK, ATTN_KV_W, KV_BLOCK),
                                lambda i: (prev(i), 0, 0)),
                   pl.BlockSpec((tm, ATTN_KV_W), lambda i: (prev(i), 0)),
                   pl.BlockSpec((tm // tq,) + qshape[1:], lambda i: (prev(i), 0, 0, 0))),
        scratch_shapes=[pltpu.VMEM((d, nkv), BF16), pltpu.VMEM((d, ATTN_Q_W), BF16),
                        pltpu.VMEM((tm, nkv), F32), pltpu.VMEM((tm, nkv), F32),
                        pltpu.VMEM((tm, ATTN_Q_W), F32), pltpu.VMEM((tm, ATTN_Q_W), F32)],
        compiler_params=_params("arbitrary"),
        name="qkv_proj",
    )(x, gx.reshape(1, d), w_in, w_in, gk.reshape(1, HEAD_DIM), gq.reshape(1, HEAD_DIM),
      *tables)


def _memory_kv(mem_ref, gmem_ref, wkv_ref, gk_ref, kt_s, v_s):
    mem_n = _head_rms(mem_ref[...], gmem_ref[...]).astype(BF16)
    kv = _dot(mem_n, wkv_ref[...].astype(BF16))
    gk = gk_ref[...]
    for hd in range(N_MEM_HEADS):
        sl = slice(hd * HEAD_DIM, (hd + 1) * HEAD_DIM)
        kt_s[sl, :] = jnp.transpose(_head_rms(kv[:, sl], gk)).astype(kt_s.dtype)
    v_s[...] = kv[:, MEM_WIDTH:].astype(v_s.dtype)


def _branches_kernel(h_ref, wu_ref, wv_ref, wqm_ref, gs_ref, ws_ref, b_ref, gq_ref,
                     mem_ref, gmem_ref, wkv_ref, gk_ref, wgo_ref, wmo_ref,
                     gm_ref, ym_ref, wgob_ref, wmob_ref, wzb_ref, wqb_ref, kt_s, v_s, *, tm):
    @pl.when(pl.program_id(0) == 0)
    def _():
        wzb_ref[:, :GMLP_WIDTH] = wu_ref[...].astype(BF16)
        wzb_ref[:, GMLP_WIDTH:] = wv_ref[...].astype(BF16)
        wqb_ref[...] = wqm_ref[...].astype(BF16)
        _memory_kv(mem_ref, gmem_ref, wkv_ref, gk_ref, kt_s, v_s)

    _cast_side_job([(wgo_ref, wgob_ref), (wmo_ref, wmob_ref)])
    h = h_ref[...]
    z = jax.nn.gelu(_dot(h, wzb_ref[...]))
    acc = _dot(h, wqb_ref[...])

    u = z[:, :GMLP_WIDTH]
    vn = _head_rms(z[:, GMLP_WIDTH:], gs_ref[...]).astype(BF16)
    bias = b_ref[...]
    for c in range(tm // CHUNK):
        rows = slice(c * CHUNK, (c + 1) * CHUNK)
        for grp in range(GMLP_GROUPS):
            cols = slice(grp * HEAD_DIM, (grp + 1) * HEAD_DIM)
            mixed = _dot(ws_ref[grp].astype(BF16), vn[rows, cols]) + bias[:, cols]
            gm_ref[rows, cols] = (u[rows, cols] * mixed).astype(gm_ref.dtype)

    g = gq_ref[...] * Q_SCALE
    for hd in range(N_MEM_HEADS):
        sl = slice(hd * HEAD_DIM, (hd + 1) * HEAD_DIM)
        qn = _head_rms(acc[:, sl], g).astype(BF16)
        s = _dot(qn, kt_s[sl, :])
        p = jnp.exp2(s - jnp.max(s, axis=-1, keepdims=True))
        l = jnp.sum(p, axis=-1, keepdims=True)
        o = _dot(p.astype(BF16), v_s[:, sl])
        ym_ref[:, sl] = (o / l).astype(ym_ref.dtype)


def _branches(h, w_in, sgu_gain, w_spatial, bias_full, mq_gain, mem, mem_gain, w_mem_kv, mk_gain,
              w_go, w_mo, tm):
    m, d = h.shape
    mt = mem.shape[0]
    row = lambda i: (i, 0)
    fixed = lambda i: (0, 0)
    once = pl.Buffered(1)
    side_in, side_out, side_shapes = _side_specs([w_go, w_mo], m // tm, row)
    return pl.pallas_call(
        functools.partial(_branches_kernel, tm=tm),
        out_shape=[jax.ShapeDtypeStruct((m, GMLP_WIDTH), BF16),
                   jax.ShapeDtypeStruct((m, MEM_WIDTH), BF16)] + side_shapes,
        grid=(m // tm,),
        in_specs=[pl.BlockSpec((tm, d), row),
                  pl.BlockSpec((d, GMLP_WIDTH), lambda i: (0, IN_U_BLK), pipeline_mode=once),
                  pl.BlockSpec((d, GMLP_WIDTH), lambda i: (0, IN_V_BLK), pipeline_mode=once),
                  pl.BlockSpec((d, MEM_WIDTH), lambda i: (0, IN_QM_BLK), pipeline_mode=once),
                  pl.BlockSpec((1, GMLP_WIDTH), fixed),
                  pl.BlockSpec((GMLP_GROUPS, CHUNK, CHUNK), lambda i: (0, 0, 0)),
                  pl.BlockSpec((CHUNK, GMLP_WIDTH), fixed),
                  pl.BlockSpec((1, HEAD_DIM), fixed),
                  pl.BlockSpec((mt, d), fixed, pipeline_mode=once),
                  pl.BlockSpec((1, d), fixed),
                  pl.BlockSpec((d, 2 * MEM_WIDTH), fixed, pipeline_mode=once),
                  pl.BlockSpec((1, HEAD_DIM), fixed)] + side_in,
        out_specs=[pl.BlockSpec((tm, GMLP_WIDTH), row),
                   pl.BlockSpec((tm, MEM_WIDTH), row)] + side_out,
        scratch_shapes=[pltpu.VMEM((d, 2 * GMLP_WIDTH), BF16), pltpu.VMEM((d, MEM_WIDTH), BF16),
                        pltpu.VMEM((MEM_WIDTH, mt), BF16), pltpu.VMEM((mt, MEM_WIDTH), BF16)],
        compiler_params=_params("arbitrary"),
        name="branches",
    )(h, w_in, w_in, w_in, sgu_gain.reshape(1, GMLP_WIDTH), w_spatial, bias_full,
      mq_gain.reshape(1, HEAD_DIM), mem, mem_gain.reshape(1, d), w_mem_kv,
      mk_gain.reshape(1, HEAD_DIM), w_go, w_mo)


def _store_heads(o_ref, out, tq):
    for r in range(Q_PER_KV):
        o_ref[:, r * HEAD_DIM:(r + 1) * HEAD_DIM] = out[r * tq:(r + 1) * tq].astype(o_ref.dtype)


def _attn_unshifted_kernel(q_ref, kt_ref, v_ref, *rest, tq, nk, nside):
    side = rest[:nside]
    o_ref = rest[nside]
    side_out = rest[nside + 1:2 * nside + 1]
    l_ref, acc_ref = rest[2 * nside + 1:]
    _cast_side_job(zip(side, side_out))
    l_ref[...] = jnp.zeros_like(l_ref)
    acc_ref[...] = jnp.zeros_like(acc_ref)

    def body(j, carry):
        off = pl.multiple_of(j * KV_BLOCK, KV_BLOCK)
        p = jnp.exp2(_dot(q_ref[0, 0], kt_ref[j]))
        l_ref[...] += p[:, :HEAD_DIM] + p[:, HEAD_DIM:]
        acc_ref[...] += _dot(p.astype(BF16), v_ref[pl.ds(off, KV_BLOCK), :])
        return carry

    lax.fori_loop(0, nk, body, 0, unroll=True)
    l = jnp.sum(l_ref[...], axis=-1, keepdims=True)
    _store_heads(o_ref, acc_ref[...] / l, tq)


def _attn_online_kernel(q_ref, kt_ref, v_ref, *rest, tq, nk, nside):
    side = rest[:nside]
    o_ref = rest[nside]
    side_out = rest[nside + 1:2 * nside + 1]
    _cast_side_job(zip(side, side_out))
    q = q_ref[0, 0]
    rows = Q_PER_KV * tq

    def body(j, carry):
        m, l, acc = carry
        off = pl.multiple_of(j * KV_BLOCK, KV_BLOCK)
        s = _dot(q, kt_ref[j])
        m_new = jnp.maximum(m, jnp.max(s, axis=-1, keepdims=True))
        alpha = jnp.exp2(m - m_new)
        p = jnp.exp2(s - m_new)
        l = alpha * l + jnp.sum(p, axis=-1, keepdims=True)
        acc = alpha * acc + _dot(p.astype(BF16), v_ref[pl.ds(off, KV_BLOCK), :])
        return m_new, l, acc

    m0 = jnp.full((rows, 1), -jnp.inf, F32)
    l0 = jnp.zeros((rows, 1), F32)
    acc0 = jnp.zeros((rows, HEAD_DIM), F32)
    _, l, acc = lax.fori_loop(0, nk, body, (m0, l0, acc0))
    _store_heads(o_ref, acc / l, tq)


def _attention(q, kt, v, *side_weights, unshifted):
    nq, _, rows, _ = q.shape
    tq = rows // Q_PER_KV
    s = v.shape[0]
    nk = s // KV_BLOCK
    gw = Q_PER_KV * HEAD_DIM
    nside = len(side_weights)
    side_in, side_out, side_shapes = _side_specs(side_weights, N_KV_HEADS * nq,
                                                 lambda g, i: (g * nq + i, 0))
    if unshifted:
        body = functools.partial(_attn_unshifted_kernel, tq=tq, nk=nk, nside=nside)
        scratch = [pltpu.VMEM((rows, HEAD_DIM), F32), pltpu.VMEM((rows, HEAD_DIM), F32)]
        name = "attn_unshifted"
    else:
        body = functools.partial(_attn_online_kernel, tq=tq, nk=nk, nside=nside)
        scratch = []
        name = "attn_online"
    return pl.pallas_call(
        body,
        out_shape=[jax.ShapeDtypeStruct((s, ATTN_Q_W), BF16)] + side_shapes,
        grid=(N_KV_HEADS, nq),
        in_specs=[pl.BlockSpec((1, 1, rows, HEAD_DIM), lambda g, i: (i, g, 0, 0)),
                  pl.BlockSpec((nk, HEAD_DIM, KV_BLOCK), lambda g, i: (0, g, 0)),
                  pl.BlockSpec((s, HEAD_DIM), lambda g, i: (0, g))] + side_in,
        out_specs=[pl.BlockSpec((tq, gw), lambda g, i: (i, g))] + side_out,
        scratch_shapes=scratch,
        compiler_params=_params("arbitrary", "arbitrary"),
        name=name,
    )(q, kt, v, *side_weights)


def _gated_merge_kernel(h_ref, a_ref, gm_ref, ym_ref, wga_ref, wgg_ref, wgm_ref,
                        wa_ref, wg_ref, wm_ref, o_ref, wb_ref):
    @pl.when(pl.program_id(1) == 0)
    def _():
        for b, w_ref in enumerate((wga_ref, wgg_ref, wgm_ref)):
            wb_ref[b] = w_ref[...].astype(BF16)

    h = h_ref[...]
    y = None
    for b, (y_ref, w_ref) in enumerate(((a_ref, wa_ref), (gm_ref, wg_ref), (ym_ref, wm_ref))):
        gate = 1.0 / (1.0 + jnp.exp(-_dot(h, wb_ref[b])))
        term = gate * _dot(y_ref[...], w_ref[...])
        y = term if y is None else y + term
    o_ref[...] = y.astype(o_ref.dtype)


def _gated_merge(h, attn, gm, ym, w_in, w_a, w_g, w_m, tm, tn):
    m, d = h.shape
    n = w_a.shape[1]
    nb = n // tn
    first = IN_GATE_BLK * IN_COL_BLOCK // tn
    row = lambda j, i: (i, 0)
    col = lambda j, i: (0, j)
    return pl.pallas_call(
        _gated_merge_kernel,
        out_shape=jax.ShapeDtypeStruct((m, n), BF16),
        grid=(nb, m // tm),
        in_specs=[pl.BlockSpec((tm, d), row),
                  pl.BlockSpec((tm, attn.shape[1]), row),
                  pl.BlockSpec((tm, gm.shape[1]), row),
                  pl.BlockSpec((tm, ym.shape[1]), row),
                  pl.BlockSpec((d, tn), lambda j, i: (0, first + j)),
                  pl.BlockSpec((d, tn), lambda j, i: (0, first + nb + j)),
                  pl.BlockSpec((d, tn), lambda j, i: (0, first + 2 * nb + j)),
                  pl.BlockSpec((w_a.shape[0], tn), col),
                  pl.BlockSpec((w_g.shape[0], tn), col),
                  pl.BlockSpec((w_m.shape[0], tn), col)],
        out_specs=pl.BlockSpec((tm, tn), lambda j, i: (i, j)),
        scratch_shapes=[pltpu.VMEM((3, d, tn), BF16)],
        compiler_params=_params("arbitrary", "arbitrary"),
        name="gated_merge",
    )(h, attn, gm, ym, w_in, w_in, w_in, w_a, w_g, w_m)


def _outproj_kernel(mg_ref, w_ref, x_ref, g_ref, x1_ref, h2_ref):
    x1 = x_ref[...] + _dot(mg_ref[...], w_ref[...])
    x1_ref[...] = x1
    h2_ref[...] = _head_rms(x1, g_ref[...]).astype(h2_ref.dtype)


def _outproj(merged, w, x, gain, tm):
    m, d = x.shape
    row = lambda i: (i, 0)
    fixed = lambda i: (0, 0)
    return pl.pallas_call(
        _outproj_kernel,
        out_shape=(jax.ShapeDtypeStruct((m, d), F32), jax.ShapeDtypeStruct((m, d), BF16)),
        grid=(m // tm,),
        in_specs=[pl.BlockSpec((tm, d), row), pl.BlockSpec((d, d), fixed),
                  pl.BlockSpec((tm, d), row), pl.BlockSpec((1, d), fixed)],
        out_specs=(pl.BlockSpec((tm, d), row), pl.BlockSpec((tm, d), row)),
        compiler_params=_params("parallel"),
        name="out_proj",
    )(merged, w, x, gain.reshape(1, d))


def _ffn_up_kernel(h_ref, w_ref, o_ref):
    z = jnp.maximum(_dot(h_ref[...], w_ref[...]), 0.0)
    o_ref[...] = (z * z).astype(o_ref.dtype)


def _ffn_up(h, w, tm, tn):
    m, d = h.shape
    n = w.shape[1]
    return pl.pallas_call(
        _ffn_up_kernel,
        out_shape=jax.ShapeDtypeStruct((m, n), BF16),
        grid=(m // tm, n // tn),
        in_specs=[pl.BlockSpec((tm, d), lambda i, j: (i, 0)),
                  pl.BlockSpec((d, tn), lambda i, j: (0, j))],
        out_specs=pl.BlockSpec((tm, tn), lambda i, j: (i, j)),
        compiler_params=_params("parallel", "parallel"),
        name="ffn_up",
    )(h, w)


def _ffn_down_kernel(a_ref, w_ref, x_ref, o_ref):
    k = pl.program_id(2)

    @pl.when(k == 0)
    def _():
        o_ref[...] = x_ref[...] + _dot(a_ref[...], w_ref[...])

    @pl.when(k != 0)
    def _():
        o_ref[...] += _dot(a_ref[...], w_ref[...])


def _ffn_down(a, w, x1, tm, tn, tk):
    m, kdim = a.shape
    n = w.shape[1]
    return pl.pallas_call(
        _ffn_down_kernel,
        out_shape=jax.ShapeDtypeStruct((m, n), F32),
        grid=(m // tm, n // tn, kdim // tk),
        in_specs=[pl.BlockSpec((tm, tk), lambda i, j, k: (i, k)),
                  pl.BlockSpec((tk, tn), lambda i, j, k: (k, j)),
                  pl.BlockSpec((tm, tn), lambda i, j, k: (i, j))],
        out_specs=pl.BlockSpec((tm, tn), lambda i, j, k: (i, j)),
        compiler_params=_params("parallel", "parallel", "arbitrary"),
        name="ffn_down",
    )(a, w, x1)


def _rope_tables(s):
    inv = ROPE_THETA ** (-jnp.arange(ROPE_PAIRS, dtype=F32) / ROPE_PAIRS)
    inv_lane = jnp.tile(inv, HEAD_DIM // ROPE_PAIRS)
    lane_grp = jnp.arange(HEAD_DIM) // ROPE_PAIRS
    is_row = (lane_grp % 2) == 0
    sign = jnp.where(lane_grp < 2, -1.0, 1.0).astype(F32)

    def tables(n):
        ang = jnp.arange(n, dtype=F32)[:, None] * inv_lane
        return jnp.cos(ang), jnp.sin(ang) * sign

    cr, sr = tables(s // GRID_W)
    cc, sc = tables(GRID_W)
    return (jnp.where(is_row, cr, 0.0), jnp.where(is_row, sr, 0.0),
            jnp.where(is_row, 0.0, cc), jnp.where(is_row, 0.0, sc))


def kernel(x, mem, norm_mix, w_in, q_norm, k_norm, sgu_norm, w_spatial, b_spatial, mem_norm,
           w_mem_kv, mq_norm, mk_norm, w_attn_o, w_gmlp_o, w_mem_o, w_out, norm_ffn,
           w_ffn_up, w_ffn_down):
    b, s, d = x.shape
    assert b == 1 and d == D_MODEL and norm_mix.shape[0] == 1
    xs = x[0]
    w_in0 = w_in[0]
    tables = _rope_tables(s)
    bias_full = jnp.repeat(b_spatial[0].T, HEAD_DIM, axis=1)

    h, kt, v, q = _qkvproj(xs, norm_mix[0], w_in0, k_norm[0], q_norm[0], tables, TM_PROJ,
                           ATTN_TQ)
    gm, ym, w_gmlp_o_b, w_mem_o_b = _branches(
        h, w_in0, sgu_norm[0], w_spatial[0], bias_full, mq_norm[0], mem[0], mem_norm[0],
        w_mem_kv[0], mk_norm[0], w_gmlp_o[0], w_mem_o[0], TM_PROJ)

    logit_bound = (1.02 * LOG2E * math.sqrt(HEAD_DIM)
                   * jnp.max(jnp.abs(q_norm[0])) * jnp.max(jnp.abs(k_norm[0])))
    attn, w_attn_o_b, w_out_b, w_up_b, w_down_b = lax.cond(
        logit_bound <= MAX_UNSHIFTED_LOGIT,
        functools.partial(_attention, unshifted=True),
        functools.partial(_attention, unshifted=False),
        q, kt, v, w_attn_o[0], w_out[0], w_ffn_up[0], w_ffn_down[0])

    merged = _gated_merge(h, attn, gm, ym, w_in0, w_attn_o_b, w_gmlp_o_b, w_mem_o_b,
                          TM_MERGE, TN_MERGE)
    x1, h2 = _outproj(merged, w_out_b, xs, norm_ffn[0], TM_OUT)
    a = _ffn_up(h2, w_up_b, TM_UP, TN_UP)
    out = _ffn_down(a, w_down_b, x1, TM_DOWN, TN_DOWN, TK_DOWN)
    return out[None]
```

```python
import functools
import math

import jax
import jax.numpy as jnp
from jax import lax
from jax.experimental import pallas as pl
from jax.experimental.pallas import tpu as pltpu

D_MODEL = 2048
HEAD_DIM = 128
N_Q_HEADS = 8
N_KV_HEADS = 2
Q_PER_KV = N_Q_HEADS // N_KV_HEADS
GRID_W = 64
ROPE_THETA = 10000.0
ROPE_PAIRS = HEAD_DIM // 4
GMLP_GROUPS = 4
GMLP_WIDTH = GMLP_GROUPS * HEAD_DIM
CHUNK = 128
N_MEM_HEADS = 4
MEM_WIDTH = N_MEM_HEADS * HEAD_DIM
D_FF = 4 * D_MODEL
EPS = 1e-6
ATTN_Q_W = N_Q_HEADS * HEAD_DIM
ATTN_KV_W = N_KV_HEADS * HEAD_DIM
IN_COL_BLOCK = 512
IN_Q_BLK, IN_KV_BLK, IN_U_BLK, IN_V_BLK, IN_QM_BLK, IN_GATE_BLK = 0, 2, 3, 4, 5, 6

VMEM_LIMIT_BYTES = 56 * 1024 * 1024
LOG2E = math.log2(math.e)
Q_SCALE = HEAD_DIM ** -0.5 * LOG2E
KV_BLOCK = 256
ATTN_TQ = 512
MAX_UNSHIFTED_LOGIT = 64.0
ROPE_PERM = tuple(list(range(0, 32)) + list(range(64, 96)) + list(range(32, 64))
                  + list(range(96, 128)))

TM_PROJ = 512
TM_MERGE, TN_MERGE = 1024, 512
TM_OUT = 512
TM_UP, TN_UP = 2048, 1024
TM_DOWN, TN_DOWN, TK_DOWN = 1024, 1024, 4096

F32 = jnp.float32
BF16 = jnp.bfloat16


def _params(*sem):
    return pltpu.CompilerParams(dimension_semantics=sem, vmem_limit_bytes=VMEM_LIMIT_BYTES)


def _dot(a, b):
    return jnp.dot(a, b, preferred_element_type=F32)


def _head_rms(x, gain):
    ms = jnp.mean(x * x, axis=-1, keepdims=True)
    return x * lax.rsqrt(ms + EPS) * gain


def _rope(y, cos, sin_signed):
    return y * cos + pltpu.roll(y, HEAD_DIM // 2, 1) * sin_signed


def _to_rope_order(w):
    grp = lax.broadcasted_iota(jnp.int32, w.shape, 1) // ROPE_PAIRS
    from_right = pltpu.roll(w, HEAD_DIM - ROPE_PAIRS, 1)
    from_left = pltpu.roll(w, ROPE_PAIRS, 1)
    return jnp.where(grp == 1, from_right, jnp.where(grp == 2, from_left, w))


def _gain_in_rope_order(g_ref):
    return _to_rope_order(jnp.broadcast_to(g_ref[...], (8, HEAD_DIM)))[0:1]


def _rope_tile(row_ref, col_ref, i, tm):
    nrow = tm // GRID_W
    rows = row_ref[pl.ds(pl.multiple_of(i * nrow, nrow), nrow), :]
    col = col_ref[...]
    return jnp.concatenate([rows[r:r + 1, :] + col for r in range(nrow)], axis=0)


def _lagged_step(i, project, finish, buf_a, buf_b):
    @pl.when(i % 2 == 0)
    def _():
        project(buf_a)
        finish(buf_b)

    @pl.when(i % 2 == 1)
    def _():
        project(buf_b)
        finish(buf_a)


def _lag_index_maps(n_tiles):
    return (lambda i: jnp.minimum(i, n_tiles - 1)), (lambda i: jnp.maximum(i - 1, 0))


def _cast_side_job(pairs):
    for src, dst in pairs:
        skip = src.shape[1] - dst.shape[1]
        dst[...] = src[:, skip:].astype(dst.dtype)


def _side_specs(arrays, nsteps, index, out_cols=None):
    in_specs, out_specs, out_shapes = [], [], []
    for k, a in enumerate(arrays):
        rows, cols = a.shape
        keep = cols if out_cols is None or out_cols[k] is None else out_cols[k]
        in_specs.append(pl.BlockSpec((rows // nsteps, cols), index))
        out_specs.append(pl.BlockSpec((rows // nsteps, keep), index))
        out_shapes.append(jax.ShapeDtypeStruct((rows, keep), BF16))
    return in_specs, out_specs, out_shapes


def _qkvproj_kernel(x_ref, gx_ref, wkv_ref, wq_ref, gk_ref, gq_ref, crow_ref, srow_ref, ccol_ref,
                    scol_ref, h_ref, kt_ref, v_ref, q_ref, wkvb_ref, wqb_ref,
                    kv_a, kv_b, q_a, q_b, *, tm, tq):
    i = pl.program_id(0)

    @pl.when(i == 0)
    def _():
        for hd in range(N_KV_HEADS):
            sl = slice(hd * HEAD_DIM, (hd + 1) * HEAD_DIM)
            wkvb_ref[:, sl] = _to_rope_order(wkv_ref[:, sl]).astype(BF16)
        wkvb_ref[:, ATTN_KV_W:] = wkv_ref[:, ATTN_KV_W:].astype(BF16)
        for hd in range(N_Q_HEADS):
            sl = slice(hd * HEAD_DIM, (hd + 1) * HEAD_DIM)
            wqb_ref[:, sl] = _to_rope_order(wq_ref[:, sl]).astype(BF16)
        kv_b[...] = jnp.zeros_like(kv_b)
        q_b[...] = jnp.zeros_like(q_b)

    def project(bufs):
        kv_acc, q_acc = bufs
        h = _head_rms(x_ref[...], gx_ref[...]).astype(BF16)
        h_ref[...] = h
        kv_acc[...] = _dot(h, wkvb_ref[...])
        q_acc[...] = _dot(h, wqb_ref[...])

    def finish(bufs):
        kv_acc, q_acc = bufs
        prev = jnp.maximum(i - 1, 0)
        cos = _rope_tile(crow_ref, ccol_ref, prev, tm)
        sin = _rope_tile(srow_ref, scol_ref, prev, tm)
        gk = _gain_in_rope_order(gk_ref)
        for hd in range(N_KV_HEADS):
            sl = slice(hd * HEAD_DIM, (hd + 1) * HEAD_DIM)
            y = _rope(_head_rms(kv_acc[:, sl], gk), cos, sin)
            for c in range(tm // KV_BLOCK):
                blk = y[c * KV_BLOCK:(c + 1) * KV_BLOCK, :]
                kt_ref[c, sl, :] = jnp.transpose(blk).astype(kt_ref.dtype)
        v_ref[...] = kv_acc[:, ATTN_KV_W:].astype(v_ref.dtype)
        gq = _gain_in_rope_order(gq_ref) * Q_SCALE
        for hd in range(N_Q_HEADS):
            grp, r = divmod(hd, Q_PER_KV)
            x = q_acc[:, hd * HEAD_DIM:(hd + 1) * HEAD_DIM]
            y = _rope(_head_rms(x, gq), cos, sin).astype(q_ref.dtype)
            for t in range(tm // tq):
                q_ref[t, grp, r * tq:(r + 1) * tq, :] = y[t * tq:(t + 1) * tq]

    _lagged_step(i, project, finish, (kv_a, q_a), (kv_b, q_b))


def _qkvproj(x, gx, w_in, gk, gq, tables, tm, tq):
    m, d = x.shape
    n_tiles = m // tm
    cur, prev = _lag_index_maps(n_tiles)
    fixed = lambda i: (0, 0)
    nkv = 2 * ATTN_KV_W
    tab_specs = [pl.BlockSpec(t.shape, fixed) for t in tables]
    qshape = (m // tq, N_KV_HEADS, Q_PER_KV * tq, HEAD_DIM)
    once = pl.Buffered(1)
    return pl.pallas_call(
        functools.partial(_qkvproj_kernel, tm=tm, tq=tq),
        out_shape=(jax.ShapeDtypeStruct((m, d), BF16),
                   jax.ShapeDtypeStruct((m // KV_BLOCK, ATTN_KV_W, KV_BLOCK), BF16),
                   jax.ShapeDtypeStruct((m, ATTN_KV_W), BF16),
                   jax.ShapeDtypeStruct(qshape, BF16)),
        grid=(n_tiles + 1,),
        in_specs=[pl.BlockSpec((tm, d), lambda i: (cur(i), 0)), pl.BlockSpec((1, d), fixed),
                  pl.BlockSpec((d, nkv), lambda i: (0, IN_KV_BLK), pipeline_mode=once),
                  pl.BlockSpec((d, ATTN_Q_W), lambda i: (0, IN_Q_BLK), pipeline_mode=once),
                  pl.BlockSpec((1, HEAD_DIM), fixed),
                  pl.BlockSpec((1, HEAD_DIM), fixed)] + tab_specs,
        out_specs=(pl.BlockSpec((tm, d), lambda i: (cur(i), 0)),
                   pl.BlockSpec((tm // KV_BLOCK, ATTN_KV_W, KV_BLOCK),
                                lambda i: (prev(i), 0, 0)),
                   pl.BlockSpec((tm, ATTN_KV_W), lambda i: (prev(i), 0)),
                   pl.BlockSpec((tm // tq,) + qshape[1:], lambda i: (prev(i), 0, 0, 0))),
        scratch_shapes=[pltpu.VMEM((d, nkv), BF16), pltpu.VMEM((d, ATTN_Q_W), BF16),
                        pltpu.VMEM((tm, nkv), F32), pltpu.VMEM((tm, nkv), F32),
                        pltpu.VMEM((tm, ATTN_Q_W), F32), pltpu.VMEM((tm, ATTN_Q_W), F32)],
        compiler_params=_params("arbitrary"),
        name="qkv_proj",
    )(x, gx.reshape(1, d), w_in, w_in, gk.reshape(1, HEAD_DIM), gq.reshape(1, HEAD_DIM),
      *tables)


def _memory_kv(mem_ref, gmem_ref, wkv_ref, gk_ref, kt_s, v_s):
    mem_n = _head_rms(mem_ref[...], gmem_ref[...]).astype(BF16)
    kv = _dot(mem_n, wkv_ref[...].astype(BF16))
    gk = gk_ref[...]
    for hd in range(N_MEM_HEADS):
        sl = slice(hd * HEAD_DIM, (hd + 1) * HEAD_DIM)
        kt_s[sl, :] = jnp.transpose(_head_rms(kv[:, sl], gk)).astype(kt_s.dtype)
    v_s[...] = kv[:, MEM_WIDTH:].astype(v_s.dtype)


def _branches_kernel(h_ref, wu_ref, wv_ref, wqm_ref, gs_ref, ws_ref, b_ref, gq_ref,
                     mem_ref, gmem_ref, wkv_ref, gk_ref, wgo_ref, wmo_ref,
                     gm_ref, ym_ref, wgob_ref, wmob_ref, wzb_ref, wqb_ref, kt_s, v_s, *, tm):
    @pl.when(pl.program_id(0) == 0)
    def _():
        wzb_ref[:, :GMLP_WIDTH] = wu_ref[...].astype(BF16)
        wzb_ref[:, GMLP_WIDTH:] = wv_ref[...].astype(BF16)
        wqb_ref[...] = wqm_ref[...].astype(BF16)
        _memory_kv(mem_ref, gmem_ref, wkv_ref, gk_ref, kt_s, v_s)

    _cast_side_job([(wgo_ref, wgob_ref), (wmo_ref, wmob_ref)])
    h = h_ref[...]
    z = jax.nn.gelu(_dot(h, wzb_ref[...]))
    acc = _dot(h, wqb_ref[...])

    u = z[:, :GMLP_WIDTH]
    vn = _head_rms(z[:, GMLP_WIDTH:], gs_ref[...]).astype(BF16)
    bias = b_ref[...]
    for c in range(tm // CHUNK):
        rows = slice(c * CHUNK, (c + 1) * CHUNK)
        for grp in range(GMLP_GROUPS):
            cols = slice(grp * HEAD_DIM, (grp + 1) * HEAD_DIM)
            mixed = _dot(ws_ref[grp].astype(BF16), vn[rows, cols]) + bias[:, cols]
            gm_ref[rows, cols] = (u[rows, cols] * mixed).astype(gm_ref.dtype)

    g = gq_ref[...] * Q_SCALE
    for hd in range(N_MEM_HEADS):
        sl = slice(hd * HEAD_DIM, (hd + 1) * HEAD_DIM)
        qn = _head_rms(acc[:, sl], g).astype(BF16)
        s = _dot(qn, kt_s[sl, :])
        p = jnp.exp2(s - jnp.max(s, axis=-1, keepdims=True))
        l = jnp.sum(p, axis=-1, keepdims=True)
        o = _dot(p.astype(BF16), v_s[:, sl])
        ym_ref[:, sl] = (o / l).astype(ym_ref.dtype)


def _branches(h, w_in, sgu_gain, w_spatial, bias_full, mq_gain, mem, mem_gain, w_mem_kv, mk_gain,
              w_go, w_mo, tm):
    m, d = h.shape
    mt = mem.shape[0]
    row = lambda i: (i, 0)
    fixed = lambda i: (0, 0)
    once = pl.Buffered(1)
    side_in, side_out, side_shapes = _side_specs([w_go, w_mo], m // tm, row)
    return pl.pallas_call(
        functools.partial(_branches_kernel, tm=tm),
        out_shape=[jax.ShapeDtypeStruct((m, GMLP_WIDTH), BF16),
                   jax.ShapeDtypeStruct((m, MEM_WIDTH), BF16)] + side_shapes,
        grid=(m // tm,),
        in_specs=[pl.BlockSpec((tm, d), row),
                  pl.BlockSpec((d, GMLP_WIDTH), lambda i: (0, IN_U_BLK), pipeline_mode=once),
                  pl.BlockSpec((d, GMLP_WIDTH), lambda i: (0, IN_V_BLK), pipeline_mode=once),
                  pl.BlockSpec((d, MEM_WIDTH), lambda i: (0, IN_QM_BLK), pipeline_mode=once),
                  pl.BlockSpec((1, GMLP_WIDTH), fixed),
                  pl.BlockSpec((GMLP_GROUPS, CHUNK, CHUNK), lambda i: (0, 0, 0)),
                  pl.BlockSpec((CHUNK, GMLP_WIDTH), fixed),
                  pl.BlockSpec((1, HEAD_DIM), fixed),
                  pl.BlockSpec((mt, d), fixed, pipeline_mode=once),
                  pl.BlockSpec((1, d), fixed),
                  pl.BlockSpec((d, 2 * MEM_WIDTH), fixed, pipeline_mode=once),
                  pl.BlockSpec((1, HEAD_DIM), fixed)] + side_in,
        out_specs=[pl.BlockSpec((tm, GMLP_WIDTH), row),
                   pl.BlockSpec((tm, MEM_WIDTH), row)] + side_out,
        scratch_shapes=[pltpu.VMEM((d, 2 * GMLP_WIDTH), BF16), pltpu.VMEM((d, MEM_WIDTH), BF16),
                        pltpu.VMEM((MEM_WIDTH, mt), BF16), pltpu.VMEM((mt, MEM_WIDTH), BF16)],
        compiler_params=_params("arbitrary"),
        name="branches",
    )(h, w_in, w_in, w_in, sgu_gain.reshape(1, GMLP_WIDTH), w_spatial, bias_full,
      mq_gain.reshape(1, HEAD_DIM), mem, mem_gain.reshape(1, d), w_mem_kv,
      mk_gain.reshape(1, HEAD_DIM), w_go, w_mo)


def _store_heads(o_ref, out, tq):
    for r in range(Q_PER_KV):
        o_ref[:, r * HEAD_DIM:(r + 1) * HEAD_DIM] = out[r * tq:(r + 1) * tq].astype(o_ref.dtype)


def _attn_unshifted_kernel(q_ref, kt_ref, v_ref, *rest, tq, nk, nside):
    side = rest[:nside]
    o_ref = rest[nside]
    side_out = rest[nside + 1:2 * nside + 1]
    l_ref, acc_ref = rest[2 * nside + 1:]
    _cast_side_job(zip(side, side_out))
    l_ref[...] = jnp.zeros_like(l_ref)
    acc_ref[...] = jnp.zeros_like(acc_ref)

    def body(j, carry):
        off = pl.multiple_of(j * KV_BLOCK, KV_BLOCK)
        p = jnp.exp2(_dot(q_ref[0, 0], kt_ref[j]))
        l_ref[...] += p[:, :HEAD_DIM] + p[:, HEAD_DIM:]
        acc_ref[...] += _dot(p.astype(BF16), v_ref[pl.ds(off, KV_BLOCK), :])
        return carry

    lax.fori_loop(0, nk, body, 0, unroll=True)
    l = jnp.sum(l_ref[...], axis=-1, keepdims=True)
    _store_heads(o_ref, acc_ref[...] / l, tq)


def _attn_online_kernel(q_ref, kt_ref, v_ref, *rest, tq, nk, nside):
    side = rest[:nside]
    o_ref = rest[nside]
    side_out = rest[nside + 1:2 * nside + 1]
    _cast_side_job(zip(side, side_out))
    q = q_ref[0, 0]
    rows = Q_PER_KV * tq

    def body(j, carry):
        m, l, acc = carry
        off = pl.multiple_of(j * KV_BLOCK, KV_BLOCK)
        s = _dot(q, kt_ref[j])
        m_new = jnp.maximum(m, jnp.max(s, axis=-1, keepdims=True))
        alpha = jnp.exp2(m - m_new)
        p = jnp.exp2(s - m_new)
        l = alpha * l + jnp.sum(p, axis=-1, keepdims=True)
        acc = alpha * acc + _dot(p.astype(BF16), v_ref[pl.ds(off, KV_BLOCK), :])
        return m_new, l, acc

    m0 = jnp.full((rows, 1), -jnp.inf, F32)
    l0 = jnp.zeros((rows, 1), F32)
    acc0 = jnp.zeros((rows, HEAD_DIM), F32)
    _, l, acc = lax.fori_loop(0, nk, body, (m0, l0, acc0))
    _store_heads(o_ref, acc / l, tq)


def _attention(q, kt, v, *side_weights, unshifted, side_cols):
    nq, _, rows, _ = q.shape
    tq = rows // Q_PER_KV
    s = v.shape[0]
    nk = s // KV_BLOCK
    gw = Q_PER_KV * HEAD_DIM
    nside = len(side_weights)
    side_in, side_out, side_shapes = _side_specs(side_weights, N_KV_HEADS * nq,
                                                 lambda g, i: (g * nq + i, 0), side_cols)
    if unshifted:
        body = functools.partial(_attn_unshifted_kernel, tq=tq, nk=nk, nside=nside)
        scratch = [pltpu.VMEM((rows, HEAD_DIM), F32), pltpu.VMEM((rows, HEAD_DIM), F32)]
        name = "attn_unshifted"
    else:
        body = functools.partial(_attn_online_kernel, tq=tq, nk=nk, nside=nside)
        scratch = []
        name = "attn_online"
    return pl.pallas_call(
        body,
        out_shape=[jax.ShapeDtypeStruct((s, ATTN_Q_W), BF16)] + side_shapes,
        grid=(N_KV_HEADS, nq),
        in_specs=[pl.BlockSpec((1, 1, rows, HEAD_DIM), lambda g, i: (i, g, 0, 0)),
                  pl.BlockSpec((nk, HEAD_DIM, KV_BLOCK), lambda g, i: (0, g, 0)),
                  pl.BlockSpec((s, HEAD_DIM), lambda g, i: (0, g))] + side_in,
        out_specs=[pl.BlockSpec((tq, gw), lambda g, i: (i, g))] + side_out,
        scratch_shapes=scratch,
        compiler_params=_params("arbitrary", "arbitrary"),
        name=name,
    )(q, kt, v, *side_weights)


def _gated_merge_kernel(h_ref, a_ref, gm_ref, ym_ref, wga_ref, wgg_ref, wgm_ref,
                        wa_ref, wg_ref, wm_ref, o_ref):
    h = h_ref[...]
    y = None
    for y_ref, wgate_ref, w_ref in ((a_ref, wga_ref, wa_ref), (gm_ref, wgg_ref, wg_ref),
                                    (ym_ref, wgm_ref, wm_ref)):
        gate = 1.0 / (1.0 + jnp.exp(-_dot(h, wgate_ref[...])))
        term = gate * _dot(y_ref[...], w_ref[...])
        y = term if y is None else y + term
    o_ref[...] = y.astype(o_ref.dtype)


def _gated_merge(h, attn, gm, ym, w_gate, w_a, w_g, w_m, tm, tn):
    m, d = h.shape
    n = w_a.shape[1]
    nb = n // tn
    row = lambda j, i: (i, 0)
    col = lambda j, i: (0, j)
    return pl.pallas_call(
        _gated_merge_kernel,
        out_shape=jax.ShapeDtypeStruct((m, n), BF16),
        grid=(nb, m // tm),
        in_specs=[pl.BlockSpec((tm, d), row),
                  pl.BlockSpec((tm, attn.shape[1]), row),
                  pl.BlockSpec((tm, gm.shape[1]), row),
                  pl.BlockSpec((tm, ym.shape[1]), row),
                  pl.BlockSpec((d, tn), lambda j, i: (0, j)),
                  pl.BlockSpec((d, tn), lambda j, i: (0, nb + j)),
                  pl.BlockSpec((d, tn), lambda j, i: (0, 2 * nb + j)),
                  pl.BlockSpec((w_a.shape[0], tn), col),
                  pl.BlockSpec((w_g.shape[0], tn), col),
                  pl.BlockSpec((w_m.shape[0], tn), col)],
        out_specs=pl.BlockSpec((tm, tn), lambda j, i: (i, j)),
        compiler_params=_params("arbitrary", "arbitrary"),
        name="gated_merge",
    )(h, attn, gm, ym, w_gate, w_gate, w_gate, w_a, w_g, w_m)


def _outproj_kernel(mg_ref, w_ref, x_ref, g_ref, x1_ref, h2_ref):
    x1 = x_ref[...] + _dot(mg_ref[...], w_ref[...])
    x1_ref[...] = x1
    h2_ref[...] = _head_rms(x1, g_ref[...]).astype(h2_ref.dtype)


def _outproj(merged, w, x, gain, tm):
    m, d = x.shape
    row = lambda i: (i, 0)
    fixed = lambda i: (0, 0)
    return pl.pallas_call(
        _outproj_kernel,
        out_shape=(jax.ShapeDtypeStruct((m, d), F32), jax.ShapeDtypeStruct((m, d), BF16)),
        grid=(m // tm,),
        in_specs=[pl.BlockSpec((tm, d), row), pl.BlockSpec((d, d), fixed),
                  pl.BlockSpec((tm, d), row), pl.BlockSpec((1, d), fixed)],
        out_specs=(pl.BlockSpec((tm, d), row), pl.BlockSpec((tm, d), row)),
        compiler_params=_params("parallel"),
        name="out_proj",
    )(merged, w, x, gain.reshape(1, d))


def _ffn_up_kernel(h_ref, w_ref, o_ref):
    z = jnp.maximum(_dot(h_ref[...], w_ref[...]), 0.0)
    o_ref[...] = (z * z).astype(o_ref.dtype)


def _ffn_up(h, w, tm, tn):
    m, d = h.shape
    n = w.shape[1]
    return pl.pallas_call(
        _ffn_up_kernel,
        out_shape=jax.ShapeDtypeStruct((m, n), BF16),
        grid=(m // tm, n // tn),
        in_specs=[pl.BlockSpec((tm, d), lambda i, j: (i, 0)),
                  pl.BlockSpec((d, tn), lambda i, j: (0, j))],
        out_specs=pl.BlockSpec((tm, tn), lambda i, j: (i, j)),
        compiler_params=_params("parallel", "parallel"),
        name="ffn_up",
    )(h, w)


def _ffn_down_kernel(a_ref, w_ref, x_ref, o_ref):
    k = pl.program_id(2)

    @pl.when(k == 0)
    def _():
        o_ref[...] = x_ref[...] + _dot(a_ref[...], w_ref[...])

    @pl.when(k != 0)
    def _():
        o_ref[...] += _dot(a_ref[...], w_ref[...])


def _ffn_down(a, w, x1, tm, tn, tk):
    m, kdim = a.shape
    n = w.shape[1]
    return pl.pallas_call(
        _ffn_down_kernel,
        out_shape=jax.ShapeDtypeStruct((m, n), F32),
        grid=(m // tm, n // tn, kdim // tk),
        in_specs=[pl.BlockSpec((tm, tk), lambda i, j, k: (i, k)),
                  pl.BlockSpec((tk, tn), lambda i, j, k: (k, j)),
                  pl.BlockSpec((tm, tn), lambda i, j, k: (i, j))],
        out_specs=pl.BlockSpec((tm, tn), lambda i, j, k: (i, j)),
        compiler_params=_params("parallel", "parallel", "arbitrary"),
        name="ffn_down",
    )(a, w, x1)


def _rope_tables(s):
    inv = ROPE_THETA ** (-jnp.arange(ROPE_PAIRS, dtype=F32) / ROPE_PAIRS)
    inv_lane = jnp.tile(inv, HEAD_DIM // ROPE_PAIRS)
    lane_grp = jnp.arange(HEAD_DIM) // ROPE_PAIRS
    is_row = (lane_grp % 2) == 0
    sign = jnp.where(lane_grp < 2, -1.0, 1.0).astype(F32)

    def tables(n):
        ang = jnp.arange(n, dtype=F32)[:, None] * inv_lane
        return jnp.cos(ang), jnp.sin(ang) * sign

    cr, sr = tables(s // GRID_W)
    cc, sc = tables(GRID_W)
    return (jnp.where(is_row, cr, 0.0), jnp.where(is_row, sr, 0.0),
            jnp.where(is_row, 0.0, cc), jnp.where(is_row, 0.0, sc))


def kernel(x, mem, norm_mix, w_in, q_norm, k_norm, sgu_norm, w_spatial, b_spatial, mem_norm,
           w_mem_kv, mq_norm, mk_norm, w_attn_o, w_gmlp_o, w_mem_o, w_out, norm_ffn,
           w_ffn_up, w_ffn_down):
    b, s, d = x.shape
    assert b == 1 and d == D_MODEL and norm_mix.shape[0] == 1
    xs = x[0]
    w_in0 = w_in[0]
    tables = _rope_tables(s)
    bias_full = jnp.repeat(b_spatial[0].T, HEAD_DIM, axis=1)

    h, kt, v, q = _qkvproj(xs, norm_mix[0], w_in0, k_norm[0], q_norm[0], tables, TM_PROJ,
                           ATTN_TQ)
    gm, ym, w_gmlp_o_b, w_mem_o_b = _branches(
        h, w_in0, sgu_norm[0], w_spatial[0], bias_full, mq_norm[0], mem[0], mem_norm[0],
        w_mem_kv[0], mk_norm[0], w_gmlp_o[0], w_mem_o[0], TM_PROJ)

    logit_bound = (1.02 * LOG2E * math.sqrt(HEAD_DIM)
                   * jnp.max(jnp.abs(q_norm[0])) * jnp.max(jnp.abs(k_norm[0])))
    n_gate = w_in0.shape[1] - IN_GATE_BLK * IN_COL_BLOCK
    side_cols = (n_gate, None, None, None, None)
    attn, w_gate_b, w_attn_o_b, w_out_b, w_up_b, w_down_b = lax.cond(
        logit_bound <= MAX_UNSHIFTED_LOGIT,
        functools.partial(_attention, unshifted=True, side_cols=side_cols),
        functools.partial(_attention, unshifted=False, side_cols=side_cols),
        q, kt, v, w_in0, w_attn_o[0], w_out[0], w_ffn_up[0], w_ffn_down[0])

    merged = _gated_merge(h, attn, gm, ym, w_gate_b, w_attn_o_b, w_gmlp_o_b, w_mem_o_b,
                          TM_MERGE, TN_MERGE)
    x1, h2 = _outproj(merged, w_out_b, xs, norm_ffn[0], TM_OUT)
    a = _ffn_up(h2, w_up_b, TM_UP, TN_UP)
    out = _ffn_down(a, w_down_b, x1, TM_DOWN, TN_DOWN, TK_DOWN)
    return out[None]
```

```python
import functools
import math

import jax
import jax.numpy as jnp
from jax import lax
from jax.experimental import pallas as pl
from jax.experimental.pallas import tpu as pltpu

D_MODEL = 2048
HEAD_DIM = 128
N_Q_HEADS = 8
N_KV_HEADS = 2
Q_PER_KV = N_Q_HEADS // N_KV_HEADS
GRID_W = 64
ROPE_THETA = 10000.0
ROPE_PAIRS = HEAD_DIM // 4
GMLP_GROUPS = 4
GMLP_WIDTH = GMLP_GROUPS * HEAD_DIM
CHUNK = 128
N_MEM_HEADS = 4
MEM_WIDTH = N_MEM_HEADS * HEAD_DIM
D_FF = 4 * D_MODEL
EPS = 1e-6
ATTN_Q_W = N_Q_HEADS * HEAD_DIM
ATTN_KV_W = N_KV_HEADS * HEAD_DIM
IN_COL_BLOCK = 512
IN_Q_BLK, IN_KV_BLK, IN_U_BLK, IN_V_BLK, IN_QM_BLK, IN_GATE_BLK = 0, 2, 3, 4, 5, 6

VMEM_LIMIT_BYTES = 56 * 1024 * 1024
LOG2E = math.log2(math.e)
Q_SCALE = HEAD_DIM ** -0.5 * LOG2E
KV_BLOCK = 256
ATTN_TQ = 512
MAX_UNSHIFTED_LOGIT = 64.0
ROPE_PERM = tuple(list(range(0, 32)) + list(range(64, 96)) + list(range(32, 64))
                  + list(range(96, 128)))

TM_PROJ = 512
TM_MERGE, TN_MERGE = 1024, 512
TM_OUT = 512
TM_UP, TN_UP = 2048, 1024
TM_DOWN, TN_DOWN, TK_DOWN = 1024, 1024, 4096

F32 = jnp.float32
BF16 = jnp.bfloat16


def _params(*sem):
    return pltpu.CompilerParams(dimension_semantics=sem, vmem_limit_bytes=VMEM_LIMIT_BYTES)


def _dot(a, b):
    return jnp.dot(a, b, preferred_element_type=F32)


def _head_rms(x, gain):
    ms = jnp.mean(x * x, axis=-1, keepdims=True)
    return x * lax.rsqrt(ms + EPS) * gain


def _rope(y, cos, sin_signed):
    return y * cos + pltpu.roll(y, HEAD_DIM // 2, 1) * sin_signed


def _to_rope_order(w):
    grp = lax.broadcasted_iota(jnp.int32, w.shape, 1) // ROPE_PAIRS
    from_right = pltpu.roll(w, HEAD_DIM - ROPE_PAIRS, 1)
    from_left = pltpu.roll(w, ROPE_PAIRS, 1)
    return jnp.where(grp == 1, from_right, jnp.where(grp == 2, from_left, w))


def _gain_in_rope_order(g_ref):
    return _to_rope_order(jnp.broadcast_to(g_ref[...], (8, HEAD_DIM)))[0:1]


def _rope_tile(row_ref, col_ref, i, tm):
    nrow = tm // GRID_W
    rows = row_ref[pl.ds(pl.multiple_of(i * nrow, nrow), nrow), :]
    col = col_ref[...]
    return jnp.concatenate([rows[r:r + 1, :] + col for r in range(nrow)], axis=0)


def _lagged_step(i, project, finish, buf_a, buf_b):
    @pl.when(i % 2 == 0)
    def _():
        project(buf_a)
        finish(buf_b)

    @pl.when(i % 2 == 1)
    def _():
        project(buf_b)
        finish(buf_a)


def _lag_index_maps(n_tiles):
    return (lambda i: jnp.minimum(i, n_tiles - 1)), (lambda i: jnp.maximum(i - 1, 0))


def _cast_side_job(pairs):
    for src, dst in pairs:
        skip = src.shape[1] - dst.shape[1]
        dst[...] = src[:, skip:].astype(dst.dtype)


def _side_specs(arrays, nsteps, index, out_cols=None):
    in_specs, out_specs, out_shapes = [], [], []
    for k, a in enumerate(arrays):
        rows, cols = a.shape
        keep = cols if out_cols is None or out_cols[k] is None else out_cols[k]
        in_specs.append(pl.BlockSpec((rows // nsteps, cols), index))
        out_specs.append(pl.BlockSpec((rows // nsteps, keep), index))
        out_shapes.append(jax.ShapeDtypeStruct((rows, keep), BF16))
    return in_specs, out_specs, out_shapes


def _qkvproj_kernel(x_ref, gx_ref, wkv_ref, wq_ref, gk_ref, gq_ref, crow_ref, srow_ref, ccol_ref,
                    scol_ref, h_ref, kt_ref, v_ref, q_ref, wkvb_ref, wqb_ref,
                    kv_a, kv_b, q_a, q_b, *, tm, tq):
    i = pl.program_id(0)

    @pl.when(i == 0)
    def _():
        for hd in range(N_KV_HEADS):
            sl = slice(hd * HEAD_DIM, (hd + 1) * HEAD_DIM)
            wkvb_ref[:, sl] = _to_rope_order(wkv_ref[:, sl]).astype(BF16)
        wkvb_ref[:, ATTN_KV_W:] = wkv_ref[:, ATTN_KV_W:].astype(BF16)
        for hd in range(N_Q_HEADS):
            sl = slice(hd * HEAD_DIM, (hd + 1) * HEAD_DIM)
            wqb_ref[:, sl] = _to_rope_order(wq_ref[:, sl]).astype(BF16)
        kv_b[...] = jnp.zeros_like(kv_b)
        q_b[...] = jnp.zeros_like(q_b)

    def project(bufs):
        kv_acc, q_acc = bufs
        h = _head_rms(x_ref[...], gx_ref[...]).astype(BF16)
        h_ref[...] = h
        kv_acc[...] = _dot(h, wkvb_ref[...])
        q_acc[...] = _dot(h, wqb_ref[...])

    def finish(bufs):
        kv_acc, q_acc = bufs
        prev = jnp.maximum(i - 1, 0)
        cos = _rope_tile(crow_ref, ccol_ref, prev, tm)
        sin = _rope_tile(srow_ref, scol_ref, prev, tm)
        gk = _gain_in_rope_order(gk_ref)
        for hd in range(N_KV_HEADS):
            sl = slice(hd * HEAD_DIM, (hd + 1) * HEAD_DIM)
            y = _rope(_head_rms(kv_acc[:, sl], gk), cos, sin)
            for c in range(tm // KV_BLOCK):
                blk = y[c * KV_BLOCK:(c + 1) * KV_BLOCK, :]
                kt_ref[c, sl, :] = jnp.transpose(blk).astype(kt_ref.dtype)
        v_ref[...] = kv_acc[:, ATTN_KV_W:].astype(v_ref.dtype)
        gq = _gain_in_rope_order(gq_ref) * Q_SCALE
        for hd in range(N_Q_HEADS):
            grp, r = divmod(hd, Q_PER_KV)
            x = q_acc[:, hd * HEAD_DIM:(hd + 1) * HEAD_DIM]
            y = _rope(_head_rms(x, gq), cos, sin).astype(q_ref.dtype)
            for t in range(tm // tq):
                q_ref[t, grp, r * tq:(r + 1) * tq, :] = y[t * tq:(t + 1) * tq]

    _lagged_step(i, project, finish, (kv_a, q_a), (kv_b, q_b))


def _qkvproj(x, gx, w_in, gk, gq, tables, tm, tq):
    m, d = x.shape
    n_tiles = m // tm
    cur, prev = _lag_index_maps(n_tiles)
    fixed = lambda i: (0, 0)
    nkv = 2 * ATTN_KV_W
    tab_specs = [pl.BlockSpec(t.shape, fixed) for t in tables]
    qshape = (m // tq, N_KV_HEADS, Q_PER_KV * tq, HEAD_DIM)
    once = pl.Buffered(1)
    return pl.pallas_call(
        functools.partial(_qkvproj_kernel, tm=tm, tq=tq),
        out_shape=(jax.ShapeDtypeStruct((m, d), BF16),
                   jax.ShapeDtypeStruct((m // KV_BLOCK, ATTN_KV_W, KV_BLOCK), BF16),
                   jax.ShapeDtypeStruct((m, ATTN_KV_W), BF16),
                   jax.ShapeDtypeStruct(qshape, BF16)),
        grid=(n_tiles + 1,),
        in_specs=[pl.BlockSpec((tm, d), lambda i: (cur(i), 0)), pl.BlockSpec((1, d), fixed),
                  pl.BlockSpec((d, nkv), lambda i: (0, IN_KV_BLK), pipeline_mode=once),
                  pl.BlockSpec((d, ATTN_Q_W), lambda i: (0, IN_Q_BLK), pipeline_mode=once),
                  pl.BlockSpec((1, HEAD_DIM), fixed),
                  pl.BlockSpec((1, HEAD_DIM), fixed)] + tab_specs,
        out_specs=(pl.BlockSpec((tm, d), lambda i: (cur(i), 0)),
                   pl.BlockSpec((tm // KV_BLOCK, ATTN_KV_W, KV_BLOCK),
                                lambda i: (prev(i), 0, 0)),
                   pl.BlockSpec((tm, ATTN_KV_W), lambda i: (prev(i), 0)),
                   pl.BlockSpec((tm // tq,) + qshape[1:], lambda i: (prev(i), 0, 0, 0))),
        scratch_shapes=[pltpu.VMEM((d, nkv), BF16), pltpu.VMEM((d, ATTN_Q_W), BF16),
                        pltpu.VMEM((tm, nkv), F32), pltpu.VMEM((tm, nkv), F32),
                        pltpu.VMEM((tm, ATTN_Q_W), F32), pltpu.VMEM((tm, ATTN_Q_W), F32)],
        compiler_params=_params("arbitrary"),
        name="qkv_proj",
    )(x, gx.reshape(1, d), w_in, w_in, gk.reshape(1, HEAD_DIM), gq.reshape(1, HEAD_DIM),
      *tables)


def _memory_kv(mem_ref, gmem_ref, wkv_ref, gk_ref, kt_s, v_s):
    mem_n = _head_rms(mem_ref[...], gmem_ref[...]).astype(BF16)
    kv = _dot(mem_n, wkv_ref[...].astype(BF16))
    gk = gk_ref[...]
    for hd in range(N_MEM_HEADS):
        sl = slice(hd * HEAD_DIM, (hd + 1) * HEAD_DIM)
        kt_s[sl, :] = jnp.transpose(_head_rms(kv[:, sl], gk)).astype(kt_s.dtype)
    v_s[...] = kv[:, MEM_WIDTH:].astype(v_s.dtype)


def _branches_kernel(h_ref, wu_ref, wv_ref, wqm_ref, gs_ref, ws_ref, b_ref, gq_ref,
                     mem_ref, gmem_ref, wkv_ref, gk_ref, wgo_ref, wmo_ref,
                     gm_ref, ym_ref, wgob_ref, wmob_ref, wzb_ref, wqb_ref, kt_s, v_s, *, tm):
    @pl.when(pl.program_id(0) == 0)
    def _():
        wzb_ref[:, :GMLP_WIDTH] = wu_ref[...].astype(BF16)
        wzb_ref[:, GMLP_WIDTH:] = wv_ref[...].astype(BF16)
        wqb_ref[...] = wqm_ref[...].astype(BF16)
        _memory_kv(mem_ref, gmem_ref, wkv_ref, gk_ref, kt_s, v_s)

    _cast_side_job([(wgo_ref, wgob_ref), (wmo_ref, wmob_ref)])
    h = h_ref[...]
    z = jax.nn.gelu(_dot(h, wzb_ref[...]))
    acc = _dot(h, wqb_ref[...])

    u = z[:, :GMLP_WIDTH]
    vn = _head_rms(z[:, GMLP_WIDTH:], gs_ref[...]).astype(BF16)
    bias = b_ref[...]
    for c in range(tm // CHUNK):
        rows = slice(c * CHUNK, (c + 1) * CHUNK)
        for grp in range(GMLP_GROUPS):
            cols = slice(grp * HEAD_DIM, (grp + 1) * HEAD_DIM)
            mixed = _dot(ws_ref[grp].astype(BF16), vn[rows, cols]) + bias[:, cols]
            gm_ref[rows, cols] = (u[rows, cols] * mixed).astype(gm_ref.dtype)

    g = gq_ref[...] * Q_SCALE
    for hd in range(N_MEM_HEADS):
        sl = slice(hd * HEAD_DIM, (hd + 1) * HEAD_DIM)
        qn = _head_rms(acc[:, sl], g).astype(BF16)
        s = _dot(qn, kt_s[sl, :])
        p = jnp.exp2(s - jnp.max(s, axis=-1, keepdims=True))
        l = jnp.sum(p, axis=-1, keepdims=True)
        o = _dot(p.astype(BF16), v_s[:, sl])
        ym_ref[:, sl] = (o / l).astype(ym_ref.dtype)


def _branches(h, w_in, sgu_gain, w_spatial, bias_full, mq_gain, mem, mem_gain, w_mem_kv, mk_gain,
              w_go, w_mo, tm):
    m, d = h.shape
    mt = mem.shape[0]
    row = lambda i: (i, 0)
    fixed = lambda i: (0, 0)
    once = pl.Buffered(1)
    side_in, side_out, side_shapes = _side_specs([w_go, w_mo], m // tm, row)
    return pl.pallas_call(
        functools.partial(_branches_kernel, tm=tm),
        out_shape=[jax.ShapeDtypeStruct((m, GMLP_WIDTH), BF16),
                   jax.ShapeDtypeStruct((m, MEM_WIDTH), BF16)] + side_shapes,
        grid=(m // tm,),
        in_specs=[pl.BlockSpec((tm, d), row),
                  pl.BlockSpec((d, GMLP_WIDTH), lambda i: (0, IN_U_BLK), pipeline_mode=once),
                  pl.BlockSpec((d, GMLP_WIDTH), lambda i: (0, IN_V_BLK), pipeline_mode=once),
                  pl.BlockSpec((d, MEM_WIDTH), lambda i: (0, IN_QM_BLK), pipeline_mode=once),
                  pl.BlockSpec((1, GMLP_WIDTH), fixed),
                  pl.BlockSpec((GMLP_GROUPS, CHUNK, CHUNK), lambda i: (0, 0, 0)),
                  pl.BlockSpec((CHUNK, GMLP_WIDTH), fixed),
                  pl.BlockSpec((1, HEAD_DIM), fixed),
                  pl.BlockSpec((mt, d), fixed, pipeline_mode=once),
                  pl.BlockSpec((1, d), fixed),
                  pl.BlockSpec((d, 2 * MEM_WIDTH), fixed, pipeline_mode=once),
                  pl.BlockSpec((1, HEAD_DIM), fixed)] + side_in,
        out_specs=[pl.BlockSpec((tm, GMLP_WIDTH), row),
                   pl.BlockSpec((tm, MEM_WIDTH), row)] + side_out,
        scratch_shapes=[pltpu.VMEM((d, 2 * GMLP_WIDTH), BF16), pltpu.VMEM((d, MEM_WIDTH), BF16),
                        pltpu.VMEM((MEM_WIDTH, mt), BF16), pltpu.VMEM((mt, MEM_WIDTH), BF16)],
        compiler_params=_params("arbitrary"),
        name="branches",
    )(h, w_in, w_in, w_in, sgu_gain.reshape(1, GMLP_WIDTH), w_spatial, bias_full,
      mq_gain.reshape(1, HEAD_DIM), mem, mem_gain.reshape(1, d), w_mem_kv,
      mk_gain.reshape(1, HEAD_DIM), w_go, w_mo)


def _store_heads(o_ref, out, tq):
    for r in range(Q_PER_KV):
        o_ref[:, r * HEAD_DIM:(r + 1) * HEAD_DIM] = out[r * tq:(r + 1) * tq].astype(o_ref.dtype)


def _attn_unshifted(q_ref, kt_ref, v_ref, o_ref, l_ref, acc_ref, tq, nk):
    l_ref[...] = jnp.zeros_like(l_ref)
    acc_ref[...] = jnp.zeros_like(acc_ref)

    def body(j, carry):
        off = pl.multiple_of(j * KV_BLOCK, KV_BLOCK)
        p = jnp.exp2(_dot(q_ref[0, 0], kt_ref[j]))
        l_ref[...] += p[:, :HEAD_DIM] + p[:, HEAD_DIM:]
        acc_ref[...] += _dot(p.astype(BF16), v_ref[pl.ds(off, KV_BLOCK), :])
        return carry

    lax.fori_loop(0, nk, body, 0, unroll=True)
    l = jnp.sum(l_ref[...], axis=-1, keepdims=True)
    _store_heads(o_ref, acc_ref[...] / l, tq)


def _attn_online(q_ref, kt_ref, v_ref, o_ref, tq, nk):
    q = q_ref[0, 0]
    rows = Q_PER_KV * tq

    def body(j, carry):
        m, l, acc = carry
        off = pl.multiple_of(j * KV_BLOCK, KV_BLOCK)
        s = _dot(q, kt_ref[j])
        m_new = jnp.maximum(m, jnp.max(s, axis=-1, keepdims=True))
        alpha = jnp.exp2(m - m_new)
        p = jnp.exp2(s - m_new)
        l = alpha * l + jnp.sum(p, axis=-1, keepdims=True)
        acc = alpha * acc + _dot(p.astype(BF16), v_ref[pl.ds(off, KV_BLOCK), :])
        return m_new, l, acc

    m0 = jnp.full((rows, 1), -jnp.inf, F32)
    l0 = jnp.zeros((rows, 1), F32)
    acc0 = jnp.zeros((rows, HEAD_DIM), F32)
    _, l, acc = lax.fori_loop(0, nk, body, (m0, l0, acc0))
    _store_heads(o_ref, acc / l, tq)


def _attn_kernel(unshifted_ref, q_ref, kt_ref, v_ref, *rest, tq, nk, nside):
    side = rest[:nside]
    o_ref = rest[nside]
    side_out = rest[nside + 1:2 * nside + 1]
    l_ref, acc_ref = rest[2 * nside + 1:]
    _cast_side_job(zip(side, side_out))
    unshifted = unshifted_ref[0] != 0

    @pl.when(unshifted)
    def _():
        _attn_unshifted(q_ref, kt_ref, v_ref, o_ref, l_ref, acc_ref, tq, nk)

    @pl.when(jnp.logical_not(unshifted))
    def _():
        _attn_online(q_ref, kt_ref, v_ref, o_ref, tq, nk)


def _attention(unshifted, q, kt, v, side_weights, side_cols):
    nq, _, rows, _ = q.shape
    tq = rows // Q_PER_KV
    s = v.shape[0]
    nk = s // KV_BLOCK
    gw = Q_PER_KV * HEAD_DIM
    nside = len(side_weights)
    side_in, side_out, side_shapes = _side_specs(side_weights, N_KV_HEADS * nq,
                                                 lambda g, i: (g * nq + i, 0), side_cols)
    return pl.pallas_call(
        functools.partial(_attn_kernel, tq=tq, nk=nk, nside=nside),
        out_shape=[jax.ShapeDtypeStruct((s, ATTN_Q_W), BF16)] + side_shapes,
        grid=(N_KV_HEADS, nq),
        in_specs=[pl.BlockSpec(memory_space=pltpu.SMEM),
                  pl.BlockSpec((1, 1, rows, HEAD_DIM), lambda g, i: (i, g, 0, 0)),
                  pl.BlockSpec((nk, HEAD_DIM, KV_BLOCK), lambda g, i: (0, g, 0)),
                  pl.BlockSpec((s, HEAD_DIM), lambda g, i: (0, g))] + side_in,
        out_specs=[pl.BlockSpec((tq, gw), lambda g, i: (i, g))] + side_out,
        scratch_shapes=[pltpu.VMEM((rows, HEAD_DIM), F32), pltpu.VMEM((rows, HEAD_DIM), F32)],
        compiler_params=_params("arbitrary", "arbitrary"),
        name="flash_attn",
    )(unshifted, q, kt, v, *side_weights)


def _gated_merge_kernel(h_ref, a_ref, gm_ref, ym_ref, wga_ref, wgg_ref, wgm_ref,
                        wa_ref, wg_ref, wm_ref, o_ref):
    h = h_ref[...]
    y = None
    for y_ref, wgate_ref, w_ref in ((a_ref, wga_ref, wa_ref), (gm_ref, wgg_ref, wg_ref),
                                    (ym_ref, wgm_ref, wm_ref)):
        gate = 1.0 / (1.0 + jnp.exp(-_dot(h, wgate_ref[...])))
        term = gate * _dot(y_ref[...], w_ref[...])
        y = term if y is None else y + term
    o_ref[...] = y.astype(o_ref.dtype)


def _gated_merge(h, attn, gm, ym, w_gate, w_a, w_g, w_m, tm, tn):
    m, d = h.shape
    n = w_a.shape[1]
    nb = n // tn
    row = lambda j, i: (i, 0)
    col = lambda j, i: (0, j)
    return pl.pallas_call(
        _gated_merge_kernel,
        out_shape=jax.ShapeDtypeStruct((m, n), BF16),
        grid=(nb, m // tm),
        in_specs=[pl.BlockSpec((tm, d), row),
                  pl.BlockSpec((tm, attn.shape[1]), row),
                  pl.BlockSpec((tm, gm.shape[1]), row),
                  pl.BlockSpec((tm, ym.shape[1]), row),
                  pl.BlockSpec((d, tn), lambda j, i: (0, j)),
                  pl.BlockSpec((d, tn), lambda j, i: (0, nb + j)),
                  pl.BlockSpec((d, tn), lambda j, i: (0, 2 * nb + j)),
                  pl.BlockSpec((w_a.shape[0], tn), col),
                  pl.BlockSpec((w_g.shape[0], tn), col),
                  pl.BlockSpec((w_m.shape[0], tn), col)],
        out_specs=pl.BlockSpec((tm, tn), lambda j, i: (i, j)),
        compiler_params=_params("arbitrary", "arbitrary"),
        name="gated_merge",
    )(h, attn, gm, ym, w_gate, w_gate, w_gate, w_a, w_g, w_m)


def _outproj_kernel(mg_ref, w_ref, x_ref, g_ref, x1_ref, h2_ref):
    x1 = x_ref[...] + _dot(mg_ref[...], w_ref[...])
    x1_ref[...] = x1
    h2_ref[...] = _head_rms(x1, g_ref[...]).astype(h2_ref.dtype)


def _outproj(merged, w, x, gain, tm):
    m, d = x.shape
    row = lambda i: (i, 0)
    fixed = lambda i: (0, 0)
    return pl.pallas_call(
        _outproj_kernel,
        out_shape=(jax.ShapeDtypeStruct((m, d), F32), jax.ShapeDtypeStruct((m, d), BF16)),
        grid=(m // tm,),
        in_specs=[pl.BlockSpec((tm, d), row), pl.BlockSpec((d, d), fixed),
                  pl.BlockSpec((tm, d), row), pl.BlockSpec((1, d), fixed)],
        out_specs=(pl.BlockSpec((tm, d), row), pl.BlockSpec((tm, d), row)),
        compiler_params=_params("parallel"),
        name="out_proj",
    )(merged, w, x, gain.reshape(1, d))


def _ffn_up_kernel(h_ref, w_ref, o_ref):
    z = jnp.maximum(_dot(h_ref[...], w_ref[...]), 0.0)
    o_ref[...] = (z * z).astype(o_ref.dtype)


def _ffn_up(h, w, tm, tn):
    m, d = h.shape
    n = w.shape[1]
    return pl.pallas_call(
        _ffn_up_kernel,
        out_shape=jax.ShapeDtypeStruct((m, n), BF16),
        grid=(m // tm, n // tn),
        in_specs=[pl.BlockSpec((tm, d), lambda i, j: (i, 0)),
                  pl.BlockSpec((d, tn), lambda i, j: (0, j))],
        out_specs=pl.BlockSpec((tm, tn), lambda i, j: (i, j)),
        compiler_params=_params("parallel", "parallel"),
        name="ffn_up",
    )(h, w)


def _ffn_down_kernel(a_ref, w_ref, x_ref, o_ref):
    k = pl.program_id(2)

    @pl.when(k == 0)
    def _():
        o_ref[...] = x_ref[...] + _dot(a_ref[...], w_ref[...])

    @pl.when(k != 0)
    def _():
        o_ref[...] += _dot(a_ref[...], w_ref[...])


def _ffn_down(a, w, x1, tm, tn, tk):
    m, kdim = a.shape
    n = w.shape[1]
    return pl.pallas_call(
        _ffn_down_kernel,
        out_shape=jax.ShapeDtypeStruct((m, n), F32),
        grid=(m // tm, n // tn, kdim // tk),
        in_specs=[pl.BlockSpec((tm, tk), lambda i, j, k: (i, k)),
                  pl.BlockSpec((tk, tn), lambda i, j, k: (k, j)),
                  pl.BlockSpec((tm, tn), lambda i, j, k: (i, j))],
        out_specs=pl.BlockSpec((tm, tn), lambda i, j, k: (i, j)),
        compiler_params=_params("parallel", "parallel", "arbitrary"),
        name="ffn_down",
    )(a, w, x1)


def _rope_tables(s):
    inv = ROPE_THETA ** (-jnp.arange(ROPE_PAIRS, dtype=F32) / ROPE_PAIRS)
    inv_lane = jnp.tile(inv, HEAD_DIM // ROPE_PAIRS)
    lane_grp = jnp.arange(HEAD_DIM) // ROPE_PAIRS
    is_row = (lane_grp % 2) == 0
    sign = jnp.where(lane_grp < 2, -1.0, 1.0).astype(F32)

    def tables(n):
        ang = jnp.arange(n, dtype=F32)[:, None] * inv_lane
        return jnp.cos(ang), jnp.sin(ang) * sign

    cr, sr = tables(s // GRID_W)
    cc, sc = tables(GRID_W)
    return (jnp.where(is_row, cr, 0.0), jnp.where(is_row, sr, 0.0),
            jnp.where(is_row, 0.0, cc), jnp.where(is_row, 0.0, sc))


def kernel(x, mem, norm_mix, w_in, q_norm, k_norm, sgu_norm, w_spatial, b_spatial, mem_norm,
           w_mem_kv, mq_norm, mk_norm, w_attn_o, w_gmlp_o, w_mem_o, w_out, norm_ffn,
           w_ffn_up, w_ffn_down):
    b, s, d = x.shape
    assert b == 1 and d == D_MODEL and norm_mix.shape[0] == 1
    xs = x[0]
    w_in0 = w_in[0]
    tables = _rope_tables(s)
    bias_full = jnp.repeat(b_spatial[0].T, HEAD_DIM, axis=1)

    h, kt, v, q = _qkvproj(xs, norm_mix[0], w_in0, k_norm[0], q_norm[0], tables, TM_PROJ,
                           ATTN_TQ)
    gm, ym, w_gmlp_o_b, w_mem_o_b = _branches(
        h, w_in0, sgu_norm[0], w_spatial[0], bias_full, mq_norm[0], mem[0], mem_norm[0],
        w_mem_kv[0], mk_norm[0], w_gmlp_o[0], w_mem_o[0], TM_PROJ)

    logit_bound = (1.02 * LOG2E * math.sqrt(HEAD_DIM)
                   * jnp.max(jnp.abs(q_norm[0])) * jnp.max(jnp.abs(k_norm[0])))
    n_gate = w_in0.shape[1] - IN_GATE_BLK * IN_COL_BLOCK
    unshifted = (logit_bound <= MAX_UNSHIFTED_LOGIT).astype(jnp.int32).reshape(1)
    attn, w_gate_b, w_attn_o_b, w_out_b, w_up_b, w_down_b = _attention(
        unshifted, q, kt, v, (w_in0, w_attn_o[0], w_out[0], w_ffn_up[0], w_ffn_down[0]),
        (n_gate, None, None, None, None))

    merged = _gated_merge(h, attn, gm, ym, w_gate_b, w_attn_o_b, w_gmlp_o_b, w_mem_o_b,
                          TM_MERGE, TN_MERGE)
    x1, h2 = _outproj(merged, w_out_b, xs, norm_ffn[0], TM_OUT)
    a = _ffn_up(h2, w_up_b, TM_UP, TN_UP)
    out = _ffn_down(a, w_down_b, x1, TM_DOWN, TN_DOWN, TK_DOWN)
    return out[None]
```

```python
import functools
import math

import jax
import jax.numpy as jnp
from jax import lax
from jax.experimental import pallas as pl
from jax.experimental.pallas import tpu as pltpu

D_MODEL = 2048
HEAD_DIM = 128
N_Q_HEADS = 8
N_KV_HEADS = 2
Q_PER_KV = N_Q_HEADS // N_KV_HEADS
GRID_W = 64
ROPE_THETA = 10000.0
ROPE_PAIRS = HEAD_DIM // 4
GMLP_GROUPS = 4
GMLP_WIDTH = GMLP_GROUPS * HEAD_DIM
CHUNK = 128
N_MEM_HEADS = 4
MEM_WIDTH = N_MEM_HEADS * HEAD_DIM
EPS = 1e-6
ATTN_Q_W = N_Q_HEADS * HEAD_DIM
ATTN_KV_W = N_KV_HEADS * HEAD_DIM
IN_COL_BLOCK = 512
IN_Q_BLK, IN_KV_BLK, IN_U_BLK, IN_V_BLK, IN_QM_BLK, IN_GATE_BLK = 0, 2, 3, 4, 5, 6

VMEM_LIMIT_BYTES = 56 * 1024 * 1024
LOG2E = math.log2(math.e)
Q_SCALE = HEAD_DIM ** -0.5 * LOG2E
KV_BLOCK = 256
SIDE_JOB_PIECES = 16
ATTN_TQ = 512
MAX_UNSHIFTED_LOGIT = 64.0

TM_PROJ = 512
TM_MERGE, TN_MERGE = 1024, 512
TM_OUT = 512
TM_UP, TN_UP = 2048, 1024
TM_DOWN, TN_DOWN, TK_DOWN = 1024, 1024, 4096

F32 = jnp.float32
BF16 = jnp.bfloat16


def _params(*sem):
    return pltpu.CompilerParams(dimension_semantics=sem, vmem_limit_bytes=VMEM_LIMIT_BYTES)


def _dot(a, b):
    return jnp.dot(a, b, preferred_element_type=F32)


def _head_rms(x, gain):
    ms = jnp.mean(x * x, axis=-1, keepdims=True)
    return x * lax.rsqrt(ms + EPS) * gain


def _rope(y, cos, sin_signed):
    return y * cos + pltpu.roll(y, HEAD_DIM // 2, 1) * sin_signed


def _to_rope_order(w):
    grp = lax.broadcasted_iota(jnp.int32, w.shape, 1) // ROPE_PAIRS
    from_right = pltpu.roll(w, HEAD_DIM - ROPE_PAIRS, 1)
    from_left = pltpu.roll(w, ROPE_PAIRS, 1)
    return jnp.where(grp == 1, from_right, jnp.where(grp == 2, from_left, w))


def _gain_in_rope_order(g_ref):
    return _to_rope_order(jnp.broadcast_to(g_ref[...], (8, HEAD_DIM)))[0:1]


def _rope_tile(row_ref, col_ref, i, tm):
    nrow = tm // GRID_W
    rows = row_ref[pl.ds(pl.multiple_of(i * nrow, nrow), nrow), :]
    col = col_ref[...]
    return jnp.concatenate([rows[r:r + 1, :] + col for r in range(nrow)], axis=0)


def _lagged_step(i, project, finish, buf_a, buf_b):
    @pl.when(i % 2 == 0)
    def _():
        project(buf_a)
        finish(buf_b)

    @pl.when(i % 2 == 1)
    def _():
        project(buf_b)
        finish(buf_a)


def _lag_index_maps(n_tiles):
    return (lambda i: jnp.minimum(i, n_tiles - 1)), (lambda i: jnp.maximum(i - 1, 0))


def _cast_side_job(pairs, piece=0, pieces=1):
    for src, dst in pairs:
        skip = src.shape[1] - dst.shape[1]
        w = dst.shape[1] // pieces
        dst[:, piece * w:(piece + 1) * w] = (
            src[:, skip + piece * w:skip + (piece + 1) * w].astype(dst.dtype))


def _side_specs(arrays, nsteps, index, out_cols=None):
    in_specs, out_specs, out_shapes = [], [], []
    for k, a in enumerate(arrays):
        rows, cols = a.shape
        keep = cols if out_cols is None or out_cols[k] is None else out_cols[k]
        in_specs.append(pl.BlockSpec((rows // nsteps, cols), index))
        out_specs.append(pl.BlockSpec((rows // nsteps, keep), index))
        out_shapes.append(jax.ShapeDtypeStruct((rows, keep), BF16))
    return in_specs, out_specs, out_shapes


def _qkvproj_kernel(x_ref, gx_ref, wkv_ref, wq_ref, gk_ref, gq_ref, crow_ref, srow_ref, ccol_ref,
                    scol_ref, h_ref, kt_ref, v_ref, q_ref, wkvb_ref, wqb_ref,
                    kv_a, kv_b, q_a, q_b, *, tm, tq):
    i = pl.program_id(0)

    @pl.when(i == 0)
    def _():
        for hd in range(N_KV_HEADS):
            sl = slice(hd * HEAD_DIM, (hd + 1) * HEAD_DIM)
            wkvb_ref[:, sl] = _to_rope_order(wkv_ref[:, sl]).astype(BF16)
        wkvb_ref[:, ATTN_KV_W:] = wkv_ref[:, ATTN_KV_W:].astype(BF16)
        for hd in range(N_Q_HEADS):
            sl = slice(hd * HEAD_DIM, (hd + 1) * HEAD_DIM)
            wqb_ref[:, sl] = _to_rope_order(wq_ref[:, sl]).astype(BF16)
        kv_b[...] = jnp.zeros_like(kv_b)
        q_b[...] = jnp.zeros_like(q_b)

    def project(bufs):
        kv_acc, q_acc = bufs
        h = _head_rms(x_ref[...], gx_ref[...]).astype(BF16)
        h_ref[...] = h
        kv_acc[...] = _dot(h, wkvb_ref[...])
        q_acc[...] = _dot(h, wqb_ref[...])

    def finish(bufs):
        kv_acc, q_acc = bufs
        prev = jnp.maximum(i - 1, 0)
        cos = _rope_tile(crow_ref, ccol_ref, prev, tm)
        sin = _rope_tile(srow_ref, scol_ref, prev, tm)
        gk = _gain_in_rope_order(gk_ref)
        for hd in range(N_KV_HEADS):
            sl = slice(hd * HEAD_DIM, (hd + 1) * HEAD_DIM)
            y = _rope(_head_rms(kv_acc[:, sl], gk), cos, sin)
            for c in range(tm // KV_BLOCK):
                blk = y[c * KV_BLOCK:(c + 1) * KV_BLOCK, :]
                kt_ref[c, sl, :] = jnp.transpose(blk).astype(kt_ref.dtype)
        v_ref[...] = kv_acc[:, ATTN_KV_W:].astype(v_ref.dtype)
        gq = _gain_in_rope_order(gq_ref) * Q_SCALE
        for hd in range(N_Q_HEADS):
            grp, r = divmod(hd, Q_PER_KV)
            x = q_acc[:, hd * HEAD_DIM:(hd + 1) * HEAD_DIM]
            y = _rope(_head_rms(x, gq), cos, sin).astype(q_ref.dtype)
            for t in range(tm // tq):
                q_ref[t, grp, r * tq:(r + 1) * tq, :] = y[t * tq:(t + 1) * tq]

    _lagged_step(i, project, finish, (kv_a, q_a), (kv_b, q_b))


def _qkvproj(x, gx, w_in, gk, gq, tables, tm, tq):
    m, d = x.shape
    n_tiles = m // tm
    cur, prev = _lag_index_maps(n_tiles)
    fixed = lambda i: (0, 0)
    nkv = 2 * ATTN_KV_W
    tab_specs = [pl.BlockSpec(t.shape, fixed) for t in tables]
    qshape = (m // tq, N_KV_HEADS, Q_PER_KV * tq, HEAD_DIM)
    once = pl.Buffered(1)
    return pl.pallas_call(
        functools.partial(_qkvproj_kernel, tm=tm, tq=tq),
        out_shape=(jax.ShapeDtypeStruct((m, d), BF16),
                   jax.ShapeDtypeStruct((m // KV_BLOCK, ATTN_KV_W, KV_BLOCK), BF16),
                   jax.ShapeDtypeStruct((m, ATTN_KV_W), BF16),
                   jax.ShapeDtypeStruct(qshape, BF16)),
        grid=(n_tiles + 1,),
        in_specs=[pl.BlockSpec((tm, d), lambda i: (cur(i), 0)), pl.BlockSpec((1, d), fixed),
                  pl.BlockSpec((d, nkv), lambda i: (0, IN_KV_BLK), pipeline_mode=once),
                  pl.BlockSpec((d, ATTN_Q_W), lambda i: (0, IN_Q_BLK), pipeline_mode=once),
                  pl.BlockSpec((1, HEAD_DIM), fixed),
                  pl.BlockSpec((1, HEAD_DIM), fixed)] + tab_specs,
        out_specs=(pl.BlockSpec((tm, d), lambda i: (cur(i), 0)),
                   pl.BlockSpec((tm // KV_BLOCK, ATTN_KV_W, KV_BLOCK),
                                lambda i: (prev(i), 0, 0)),
                   pl.BlockSpec((tm, ATTN_KV_W), lambda i: (prev(i), 0)),
                   pl.BlockSpec((tm // tq,) + qshape[1:], lambda i: (prev(i), 0, 0, 0))),
        scratch_shapes=[pltpu.VMEM((d, nkv), BF16), pltpu.VMEM((d, ATTN_Q_W), BF16),
                        pltpu.VMEM((tm, nkv), F32), pltpu.VMEM((tm, nkv), F32),
                        pltpu.VMEM((tm, ATTN_Q_W), F32), pltpu.VMEM((tm, ATTN_Q_W), F32)],
        compiler_params=_params("arbitrary"),
        name="qkv_proj",
    )(x, gx.reshape(1, d), w_in, w_in, gk.reshape(1, HEAD_DIM), gq.reshape(1, HEAD_DIM),
      *tables)


def _memory_kv(mem_ref, gmem_ref, wkv_ref, gk_ref, kt_s, v_s):
    mem_n = _head_rms(mem_ref[...], gmem_ref[...]).astype(BF16)
    kv = _dot(mem_n, wkv_ref[...].astype(BF16))
    gk = gk_ref[...]
    for hd in range(N_MEM_HEADS):
        sl = slice(hd * HEAD_DIM, (hd + 1) * HEAD_DIM)
        kt_s[sl, :] = jnp.transpose(_head_rms(kv[:, sl], gk)).astype(kt_s.dtype)
    v_s[...] = kv[:, MEM_WIDTH:].astype(v_s.dtype)


def _branches_kernel(h_ref, wu_ref, wv_ref, wqm_ref, gs_ref, ws_ref, b_ref, gq_ref,
                     mem_ref, gmem_ref, wkv_ref, gk_ref, wgo_ref, wmo_ref,
                     gm_ref, ym_ref, wgob_ref, wmob_ref, wzb_ref, wqb_ref, kt_s, v_s, *, tm):
    @pl.when(pl.program_id(0) == 0)
    def _():
        wzb_ref[:, :GMLP_WIDTH] = wu_ref[...].astype(BF16)
        wzb_ref[:, GMLP_WIDTH:] = wv_ref[...].astype(BF16)
        wqb_ref[...] = wqm_ref[...].astype(BF16)
        _memory_kv(mem_ref, gmem_ref, wkv_ref, gk_ref, kt_s, v_s)

    _cast_side_job([(wgo_ref, wgob_ref), (wmo_ref, wmob_ref)])
    h = h_ref[...]
    z = jax.nn.gelu(_dot(h, wzb_ref[...]))
    acc = _dot(h, wqb_ref[...])

    u = z[:, :GMLP_WIDTH]
    vn = _head_rms(z[:, GMLP_WIDTH:], gs_ref[...]).astype(BF16)
    bias = b_ref[...]
    for c in range(tm // CHUNK):
        rows = slice(c * CHUNK, (c + 1) * CHUNK)
        for grp in range(GMLP_GROUPS):
            cols = slice(grp * HEAD_DIM, (grp + 1) * HEAD_DIM)
            mixed = _dot(ws_ref[grp].astype(BF16), vn[rows, cols]) + bias[:, cols]
            gm_ref[rows, cols] = (u[rows, cols] * mixed).astype(gm_ref.dtype)

    g = gq_ref[...] * Q_SCALE
    for hd in range(N_MEM_HEADS):
        sl = slice(hd * HEAD_DIM, (hd + 1) * HEAD_DIM)
        qn = _head_rms(acc[:, sl], g).astype(BF16)
        s = _dot(qn, kt_s[sl, :])
        p = jnp.exp2(s - jnp.max(s, axis=-1, keepdims=True))
        l = jnp.sum(p, axis=-1, keepdims=True)
        o = _dot(p.astype(BF16), v_s[:, sl])
        ym_ref[:, sl] = (o / l).astype(ym_ref.dtype)


def _branches(h, w_in, sgu_gain, w_spatial, bias_full, mq_gain, mem, mem_gain, w_mem_kv, mk_gain,
              w_go, w_mo, tm):
    m, d = h.shape
    mt = mem.shape[0]
    row = lambda i: (i, 0)
    fixed = lambda i: (0, 0)
    once = pl.Buffered(1)
    side_in, side_out, side_shapes = _side_specs([w_go, w_mo], m // tm, row)
    return pl.pallas_call(
        functools.partial(_branches_kernel, tm=tm),
        out_shape=[jax.ShapeDtypeStruct((m, GMLP_WIDTH), BF16),
                   jax.ShapeDtypeStruct((m, MEM_WIDTH), BF16)] + side_shapes,
        grid=(m // tm,),
        in_specs=[pl.BlockSpec((tm, d), row),
                  pl.BlockSpec((d, GMLP_WIDTH), lambda i: (0, IN_U_BLK), pipeline_mode=once),
                  pl.BlockSpec((d, GMLP_WIDTH), lambda i: (0, IN_V_BLK), pipeline_mode=once),
                  pl.BlockSpec((d, MEM_WIDTH), lambda i: (0, IN_QM_BLK), pipeline_mode=once),
                  pl.BlockSpec((1, GMLP_WIDTH), fixed),
                  pl.BlockSpec((GMLP_GROUPS, CHUNK, CHUNK), lambda i: (0, 0, 0)),
                  pl.BlockSpec((CHUNK, GMLP_WIDTH), fixed),
                  pl.BlockSpec((1, HEAD_DIM), fixed),
                  pl.BlockSpec((mt, d), fixed, pipeline_mode=once),
                  pl.BlockSpec((1, d), fixed),
                  pl.BlockSpec((d, 2 * MEM_WIDTH), fixed, pipeline_mode=once),
                  pl.BlockSpec((1, HEAD_DIM), fixed)] + side_in,
        out_specs=[pl.BlockSpec((tm, GMLP_WIDTH), row),
                   pl.BlockSpec((tm, MEM_WIDTH), row)] + side_out,
        scratch_shapes=[pltpu.VMEM((d, 2 * GMLP_WIDTH), BF16), pltpu.VMEM((d, MEM_WIDTH), BF16),
                        pltpu.VMEM((MEM_WIDTH, mt), BF16), pltpu.VMEM((mt, MEM_WIDTH), BF16)],
        compiler_params=_params("arbitrary"),
        name="branches",
    )(h, w_in, w_in, w_in, sgu_gain.reshape(1, GMLP_WIDTH), w_spatial, bias_full,
      mq_gain.reshape(1, HEAD_DIM), mem, mem_gain.reshape(1, d), w_mem_kv,
      mk_gain.reshape(1, HEAD_DIM), w_go, w_mo)


def _store_heads(o_ref, out, tq):
    for r in range(Q_PER_KV):
        o_ref[:, r * HEAD_DIM:(r + 1) * HEAD_DIM] = out[r * tq:(r + 1) * tq].astype(o_ref.dtype)


def _attn_unshifted(q_ref, kt_ref, v_ref, o_ref, l_ref, acc_ref, tq, nk, side_pairs):
    l_ref[...] = jnp.zeros_like(l_ref)
    acc_ref[...] = jnp.zeros_like(acc_ref)
    every = nk // SIDE_JOB_PIECES
    for j in range(nk):
        p = jnp.exp2(_dot(q_ref[0, 0], kt_ref[j]))
        l_ref[...] += p[:, :HEAD_DIM] + p[:, HEAD_DIM:]
        acc_ref[...] += _dot(p.astype(BF16), v_ref[j * KV_BLOCK:(j + 1) * KV_BLOCK, :])
        if j % every == 0:
            _cast_side_job(side_pairs, piece=j // every, pieces=SIDE_JOB_PIECES)
    l = jnp.sum(l_ref[...], axis=-1, keepdims=True)
    _store_heads(o_ref, acc_ref[...] / l, tq)


def _attn_online(q_ref, kt_ref, v_ref, o_ref, tq, nk):
    q = q_ref[0, 0]
    rows = Q_PER_KV * tq

    def body(j, carry):
        m, l, acc = carry
        off = pl.multiple_of(j * KV_BLOCK, KV_BLOCK)
        s = _dot(q, kt_ref[j])
        m_new = jnp.maximum(m, jnp.max(s, axis=-1, keepdims=True))
        alpha = jnp.exp2(m - m_new)
        p = jnp.exp2(s - m_new)
        l = alpha * l + jnp.sum(p, axis=-1, keepdims=True)
        acc = alpha * acc + _dot(p.astype(BF16), v_ref[pl.ds(off, KV_BLOCK), :])
        return m_new, l, acc

    m0 = jnp.full((rows, 1), -jnp.inf, F32)
    l0 = jnp.zeros((rows, 1), F32)
    acc0 = jnp.zeros((rows, HEAD_DIM), F32)
    _, l, acc = lax.fori_loop(0, nk, body, (m0, l0, acc0))
    _store_heads(o_ref, acc / l, tq)


def _attn_kernel(unshifted_ref, q_ref, kt_ref, v_ref, *rest, tq, nk, nside):
    side = rest[:nside]
    o_ref = rest[nside]
    side_out = rest[nside + 1:2 * nside + 1]
    l_ref, acc_ref = rest[2 * nside + 1:]
    side_pairs = list(zip(side, side_out))
    unshifted = unshifted_ref[0] != 0

    @pl.when(unshifted)
    def _():
        _attn_unshifted(q_ref, kt_ref, v_ref, o_ref, l_ref, acc_ref, tq, nk, side_pairs)

    @pl.when(jnp.logical_not(unshifted))
    def _():
        _cast_side_job(side_pairs)
        _attn_online(q_ref, kt_ref, v_ref, o_ref, tq, nk)


def _attention(unshifted, q, kt, v, side_weights, side_cols):
    nq, _, rows, _ = q.shape
    tq = rows // Q_PER_KV
    s = v.shape[0]
    nk = s // KV_BLOCK
    gw = Q_PER_KV * HEAD_DIM
    nside = len(side_weights)
    side_in, side_out, side_shapes = _side_specs(side_weights, N_KV_HEADS * nq,
                                                 lambda g, i: (g * nq + i, 0), side_cols)
    return pl.pallas_call(
        functools.partial(_attn_kernel, tq=tq, nk=nk, nside=nside),
        out_shape=[jax.ShapeDtypeStruct((s, ATTN_Q_W), BF16)] + side_shapes,
        grid=(N_KV_HEADS, nq),
        in_specs=[pl.BlockSpec(memory_space=pltpu.SMEM),
                  pl.BlockSpec((1, 1, rows, HEAD_DIM), lambda g, i: (i, g, 0, 0)),
                  pl.BlockSpec((nk, HEAD_DIM, KV_BLOCK), lambda g, i: (0, g, 0)),
                  pl.BlockSpec((s, HEAD_DIM), lambda g, i: (0, g))] + side_in,
        out_specs=[pl.BlockSpec((tq, gw), lambda g, i: (i, g))] + side_out,
        scratch_shapes=[pltpu.VMEM((rows, HEAD_DIM), F32), pltpu.VMEM((rows, HEAD_DIM), F32)],
        compiler_params=_params("arbitrary", "arbitrary"),
        name="flash_attn",
    )(unshifted, q, kt, v, *side_weights)


def _gated_merge_kernel(h_ref, a_ref, gm_ref, ym_ref, wga_ref, wgg_ref, wgm_ref,
                        wa_ref, wg_ref, wm_ref, o_ref):
    h = h_ref[...]
    y = None
    for y_ref, wgate_ref, w_ref in ((a_ref, wga_ref, wa_ref), (gm_ref, wgg_ref, wg_ref),
                                    (ym_ref, wgm_ref, wm_ref)):
        gate = 1.0 / (1.0 + jnp.exp(-_dot(h, wgate_ref[...])))
        term = gate * _dot(y_ref[...], w_ref[...])
        y = term if y is None else y + term
    o_ref[...] = y.astype(o_ref.dtype)


def _gated_merge(h, attn, gm, ym, w_gate, w_a, w_g, w_m, tm, tn):
    m, d = h.shape
    n = w_a.shape[1]
    nb = n // tn
    row = lambda j, i: (i, 0)
    col = lambda j, i: (0, j)
    return pl.pallas_call(
        _gated_merge_kernel,
        out_shape=jax.ShapeDtypeStruct((m, n), BF16),
        grid=(nb, m // tm),
        in_specs=[pl.BlockSpec((tm, d), row),
                  pl.BlockSpec((tm, attn.shape[1]), row),
                  pl.BlockSpec((tm, gm.shape[1]), row),
                  pl.BlockSpec((tm, ym.shape[1]), row),
                  pl.BlockSpec((d, tn), lambda j, i: (0, j)),
                  pl.BlockSpec((d, tn), lambda j, i: (0, nb + j)),
                  pl.BlockSpec((d, tn), lambda j, i: (0, 2 * nb + j)),
                  pl.BlockSpec((w_a.shape[0], tn), col),
                  pl.BlockSpec((w_g.shape[0], tn), col),
                  pl.BlockSpec((w_m.shape[0], tn), col)],
        out_specs=pl.BlockSpec((tm, tn), lambda j, i: (i, j)),
        compiler_params=_params("arbitrary", "arbitrary"),
        name="gated_merge",
    )(h, attn, gm, ym, w_gate, w_gate, w_gate, w_a, w_g, w_m)


def _outproj_kernel(mg_ref, w_ref, x_ref, g_ref, x1_ref, h2_ref):
    x1 = x_ref[...] + _dot(mg_ref[...], w_ref[...])
    x1_ref[...] = x1
    h2_ref[...] = _head_rms(x1, g_ref[...]).astype(h2_ref.dtype)


def _outproj(merged, w, x, gain, tm):
    m, d = x.shape
    row = lambda i: (i, 0)
    fixed = lambda i: (0, 0)
    return pl.pallas_call(
        _outproj_kernel,
        out_shape=(jax.ShapeDtypeStruct((m, d), F32), jax.ShapeDtypeStruct((m, d), BF16)),
        grid=(m // tm,),
        in_specs=[pl.BlockSpec((tm, d), row), pl.BlockSpec((d, d), fixed),
                  pl.BlockSpec((tm, d), row), pl.BlockSpec((1, d), fixed)],
        out_specs=(pl.BlockSpec((tm, d), row), pl.BlockSpec((tm, d), row)),
        compiler_params=_params("parallel"),
        name="out_proj",
    )(merged, w, x, gain.reshape(1, d))


def _ffn_up_kernel(h_ref, w_ref, o_ref):
    z = jnp.maximum(_dot(h_ref[...], w_ref[...]), 0.0)
    o_ref[...] = (z * z).astype(o_ref.dtype)


def _ffn_up(h, w, tm, tn):
    m, d = h.shape
    n = w.shape[1]
    return pl.pallas_call(
        _ffn_up_kernel,
        out_shape=jax.ShapeDtypeStruct((m, n), BF16),
        grid=(m // tm, n // tn),
        in_specs=[pl.BlockSpec((tm, d), lambda i, j: (i, 0)),
                  pl.BlockSpec((d, tn), lambda i, j: (0, j))],
        out_specs=pl.BlockSpec((tm, tn), lambda i, j: (i, j)),
        compiler_params=_params("parallel", "parallel"),
        name="ffn_up",
    )(h, w)


def _ffn_down_kernel(a_ref, w_ref, x_ref, o_ref):
    k = pl.program_id(2)

    @pl.when(k == 0)
    def _():
        o_ref[...] = x_ref[...] + _dot(a_ref[...], w_ref[...])

    @pl.when(k != 0)
    def _():
        o_ref[...] += _dot(a_ref[...], w_ref[...])


def _ffn_down(a, w, x1, tm, tn, tk):
    m, kdim = a.shape
    n = w.shape[1]
    return pl.pallas_call(
        _ffn_down_kernel,
        out_shape=jax.ShapeDtypeStruct((m, n), F32),
        grid=(m // tm, n // tn, kdim // tk),
        in_specs=[pl.BlockSpec((tm, tk), lambda i, j, k: (i, k)),
                  pl.BlockSpec((tk, tn), lambda i, j, k: (k, j)),
                  pl.BlockSpec((tm, tn), lambda i, j, k: (i, j))],
        out_specs=pl.BlockSpec((tm, tn), lambda i, j, k: (i, j)),
        compiler_params=_params("parallel", "parallel", "arbitrary"),
        name="ffn_down",
    )(a, w, x1)


def _rope_tables(s):
    inv = ROPE_THETA ** (-jnp.arange(ROPE_PAIRS, dtype=F32) / ROPE_PAIRS)
    inv_lane = jnp.tile(inv, HEAD_DIM // ROPE_PAIRS)
    lane_grp = jnp.arange(HEAD_DIM) // ROPE_PAIRS
    is_row = (lane_grp % 2) == 0
    sign = jnp.where(lane_grp < 2, -1.0, 1.0).astype(F32)

    def tables(n):
        ang = jnp.arange(n, dtype=F32)[:, None] * inv_lane
        return jnp.cos(ang), jnp.sin(ang) * sign

    cr, sr = tables(s // GRID_W)
    cc, sc = tables(GRID_W)
    return (jnp.where(is_row, cr, 0.0), jnp.where(is_row, sr, 0.0),
            jnp.where(is_row, 0.0, cc), jnp.where(is_row, 0.0, sc))


def kernel(x, mem, norm_mix, w_in, q_norm, k_norm, sgu_norm, w_spatial, b_spatial, mem_norm,
           w_mem_kv, mq_norm, mk_norm, w_attn_o, w_gmlp_o, w_mem_o, w_out, norm_ffn,
           w_ffn_up, w_ffn_down):
    b, s, d = x.shape
    assert b == 1 and d == D_MODEL and norm_mix.shape[0] == 1
    xs = x[0]
    w_in0 = w_in[0]
    tables = _rope_tables(s)
    bias_full = jnp.repeat(b_spatial[0].T, HEAD_DIM, axis=1)

    h, kt, v, q = _qkvproj(xs, norm_mix[0], w_in0, k_norm[0], q_norm[0], tables, TM_PROJ,
                           ATTN_TQ)
    gm, ym, w_gmlp_o_b, w_mem_o_b = _branches(
        h, w_in0, sgu_norm[0], w_spatial[0], bias_full, mq_norm[0], mem[0], mem_norm[0],
        w_mem_kv[0], mk_norm[0], w_gmlp_o[0], w_mem_o[0], TM_PROJ)

    logit_bound = (1.02 * LOG2E * math.sqrt(HEAD_DIM)
                   * jnp.max(jnp.abs(q_norm[0])) * jnp.max(jnp.abs(k_norm[0])))
    n_gate = w_in0.shape[1] - IN_GATE_BLK * IN_COL_BLOCK
    unshifted = (logit_bound <= MAX_UNSHIFTED_LOGIT).astype(jnp.int32).reshape(1)
    attn, w_gate_b, w_attn_o_b, w_out_b, w_up_b, w_down_b = _attention(
        unshifted, q, kt, v, (w_in0, w_attn_o[0], w_out[0], w_ffn_up[0], w_ffn_down[0]),
        (n_gate, None, None, None, None))

    merged = _gated_merge(h, attn, gm, ym, w_gate_b, w_attn_o_b, w_gmlp_o_b, w_mem_o_b,
                          TM_MERGE, TN_MERGE)
    x1, h2 = _outproj(merged, w_out_b, xs, norm_ffn[0], TM_OUT)
    a = _ffn_up(h2, w_up_b, TM_UP, TN_UP)
    out = _ffn_down(a, w_down_b, x1, TM_DOWN, TN_DOWN, TK_DOWN)
    return out[None]
```

```python
import functools
import math

import jax
import jax.numpy as jnp
from jax import lax
from jax.experimental import pallas as pl
from jax.experimental.pallas import tpu as pltpu

D_MODEL = 2048
HEAD_DIM = 128
N_Q_HEADS = 8
N_KV_HEADS = 2
Q_PER_KV = N_Q_HEADS // N_KV_HEADS
GRID_W = 64
ROPE_THETA = 10000.0
ROPE_PAIRS = HEAD_DIM // 4
GMLP_GROUPS = 4
GMLP_WIDTH = GMLP_GROUPS * HEAD_DIM
CHUNK = 128
N_MEM_HEADS = 4
MEM_WIDTH = N_MEM_HEADS * HEAD_DIM
EPS = 1e-6
ATTN_Q_W = N_Q_HEADS * HEAD_DIM
ATTN_KV_W = N_KV_HEADS * HEAD_DIM
IN_COL_BLOCK = 512
IN_Q_BLK, IN_KV_BLK, IN_U_BLK, IN_V_BLK, IN_QM_BLK, IN_GATE_BLK = 0, 2, 3, 4, 5, 6

VMEM_LIMIT_BYTES = 56 * 1024 * 1024
LOG2E = math.log2(math.e)
Q_SCALE = HEAD_DIM ** -0.5 * LOG2E
KV_BLOCK = 256
ATTN_TQ = 512
MAX_UNSHIFTED_LOGIT = 64.0

TM_PROJ = 512
TM_MERGE, TN_MERGE = 1024, 512
TM_OUT = 512
TM_UP, TN_UP = 2048, 1024
TM_DOWN, TN_DOWN, TK_DOWN = 1024, 1024, 4096

F32 = jnp.float32
BF16 = jnp.bfloat16


def _params(*sem):
    return pltpu.CompilerParams(dimension_semantics=sem, vmem_limit_bytes=VMEM_LIMIT_BYTES)


def _dot(a, b):
    return jnp.dot(a, b, preferred_element_type=F32)


def _head_rms(x, gain):
    ms = jnp.mean(x * x, axis=-1, keepdims=True)
    return x * lax.rsqrt(ms + EPS) * gain


def _rope(y, cos, sin_signed):
    return y * cos + pltpu.roll(y, HEAD_DIM // 2, 1) * sin_signed


def _to_rope_order(w):
    grp = lax.broadcasted_iota(jnp.int32, w.shape, 1) // ROPE_PAIRS
    from_right = pltpu.roll(w, HEAD_DIM - ROPE_PAIRS, 1)
    from_left = pltpu.roll(w, ROPE_PAIRS, 1)
    return jnp.where(grp == 1, from_right, jnp.where(grp == 2, from_left, w))


def _gain_in_rope_order(g_ref):
    return _to_rope_order(jnp.broadcast_to(g_ref[...], (8, HEAD_DIM)))[0:1]


def _rope_tile(row_ref, col_ref, i, tm):
    nrow = tm // GRID_W
    rows = row_ref[pl.ds(pl.multiple_of(i * nrow, nrow), nrow), :]
    col = col_ref[...]
    return jnp.concatenate([rows[r:r + 1, :] + col for r in range(nrow)], axis=0)


def _lagged_step(i, project, finish, buf_a, buf_b):
    @pl.when(i % 2 == 0)
    def _():
        project(buf_a)
        finish(buf_b)

    @pl.when(i % 2 == 1)
    def _():
        project(buf_b)
        finish(buf_a)


def _lag_index_maps(n_tiles):
    return (lambda i: jnp.minimum(i, n_tiles - 1)), (lambda i: jnp.maximum(i - 1, 0))


def _cast_side_job(pairs):
    for src, dst in pairs:
        skip = src.shape[1] - dst.shape[1]
        dst[...] = src[:, skip:].astype(dst.dtype)


def _side_specs(arrays, nsteps, index, out_cols=None):
    in_specs, out_specs, out_shapes = [], [], []
    for k, a in enumerate(arrays):
        rows, cols = a.shape
        keep = cols if out_cols is None or out_cols[k] is None else out_cols[k]
        in_specs.append(pl.BlockSpec((rows // nsteps, cols), index))
        out_specs.append(pl.BlockSpec((rows // nsteps, keep), index))
        out_shapes.append(jax.ShapeDtypeStruct((rows, keep), BF16))
    return in_specs, out_specs, out_shapes


def _qkvproj_kernel(x_ref, gx_ref, wkv_ref, wq_ref, gk_ref, gq_ref, crow_ref, srow_ref, ccol_ref,
                    scol_ref, h_ref, kt_ref, v_ref, q_ref, wkvb_ref, wqb_ref,
                    kv_a, kv_b, q_a, q_b, *, tm, tq):
    i = pl.program_id(0)

    @pl.when(i == 0)
    def _():
        for hd in range(N_KV_HEADS):
            sl = slice(hd * HEAD_DIM, (hd + 1) * HEAD_DIM)
            wkvb_ref[:, sl] = _to_rope_order(wkv_ref[:, sl]).astype(BF16)
        wkvb_ref[:, ATTN_KV_W:] = wkv_ref[:, ATTN_KV_W:].astype(BF16)
        for hd in range(N_Q_HEADS):
            sl = slice(hd * HEAD_DIM, (hd + 1) * HEAD_DIM)
            wqb_ref[:, sl] = _to_rope_order(wq_ref[:, sl]).astype(BF16)
        kv_b[...] = jnp.zeros_like(kv_b)
        q_b[...] = jnp.zeros_like(q_b)

    def project(bufs):
        kv_acc, q_acc = bufs
        h = _head_rms(x_ref[...], gx_ref[...]).astype(BF16)
        h_ref[...] = h
        kv_acc[...] = _dot(h, wkvb_ref[...])
        q_acc[...] = _dot(h, wqb_ref[...])

    def finish(bufs):
        kv_acc, q_acc = bufs
        prev = jnp.maximum(i - 1, 0)
        cos = _rope_tile(crow_ref, ccol_ref, prev, tm)
        sin = _rope_tile(srow_ref, scol_ref, prev, tm)
        gk = _gain_in_rope_order(gk_ref)
        for hd in range(N_KV_HEADS):
            sl = slice(hd * HEAD_DIM, (hd + 1) * HEAD_DIM)
            y = _rope(_head_rms(kv_acc[:, sl], gk), cos, sin)
            for c in range(tm // KV_BLOCK):
                blk = y[c * KV_BLOCK:(c + 1) * KV_BLOCK, :]
                kt_ref[c, sl, :] = jnp.transpose(blk).astype(kt_ref.dtype)
        v_ref[...] = kv_acc[:, ATTN_KV_W:].astype(v_ref.dtype)
        gq = _gain_in_rope_order(gq_ref) * Q_SCALE
        for hd in range(N_Q_HEADS):
            grp, r = divmod(hd, Q_PER_KV)
            x = q_acc[:, hd * HEAD_DIM:(hd + 1) * HEAD_DIM]
            y = _rope(_head_rms(x, gq), cos, sin).astype(q_ref.dtype)
            for t in range(tm // tq):
                q_ref[t, grp, r * tq:(r + 1) * tq, :] = y[t * tq:(t + 1) * tq]

    _lagged_step(i, project, finish, (kv_a, q_a), (kv_b, q_b))


def _qkvproj(x, gx, w_in, gk, gq, tables, tm, tq):
    m, d = x.shape
    n_tiles = m // tm
    cur, prev = _lag_index_maps(n_tiles)
    fixed = lambda i: (0, 0)
    nkv = 2 * ATTN_KV_W
    tab_specs = [pl.BlockSpec(t.shape, fixed) for t in tables]
    qshape = (m // tq, N_KV_HEADS, Q_PER_KV * tq, HEAD_DIM)
    once = pl.Buffered(1)
    return pl.pallas_call(
        functools.partial(_qkvproj_kernel, tm=tm, tq=tq),
        out_shape=(jax.ShapeDtypeStruct((m, d), BF16),
                   jax.ShapeDtypeStruct((m // KV_BLOCK, ATTN_KV_W, KV_BLOCK), BF16),
                   jax.ShapeDtypeStruct((m, ATTN_KV_W), BF16),
                   jax.ShapeDtypeStruct(qshape, BF16)),
        grid=(n_tiles + 1,),
        in_specs=[pl.BlockSpec((tm, d), lambda i: (cur(i), 0)), pl.BlockSpec((1, d), fixed),
                  pl.BlockSpec((d, nkv), lambda i: (0, IN_KV_BLK), pipeline_mode=once),
                  pl.BlockSpec((d, ATTN_Q_W), lambda i: (0, IN_Q_BLK), pipeline_mode=once),
                  pl.BlockSpec((1, HEAD_DIM), fixed),
                  pl.BlockSpec((1, HEAD_DIM), fixed)] + tab_specs,
        out_specs=(pl.BlockSpec((tm, d), lambda i: (cur(i), 0)),
                   pl.BlockSpec((tm // KV_BLOCK, ATTN_KV_W, KV_BLOCK),
                                lambda i: (prev(i), 0, 0)),
                   pl.BlockSpec((tm, ATTN_KV_W), lambda i: (prev(i), 0)),
                   pl.BlockSpec((tm // tq,) + qshape[1:], lambda i: (prev(i), 0, 0, 0))),
        scratch_shapes=[pltpu.VMEM((d, nkv), BF16), pltpu.VMEM((d, ATTN_Q_W), BF16),
                        pltpu.VMEM((tm, nkv), F32), pltpu.VMEM((tm, nkv), F32),
                        pltpu.VMEM((tm, ATTN_Q_W), F32), pltpu.VMEM((tm, ATTN_Q_W), F32)],
        compiler_params=_params("arbitrary"),
        name="qkv_proj",
    )(x, gx.reshape(1, d), w_in, w_in, gk.reshape(1, HEAD_DIM), gq.reshape(1, HEAD_DIM),
      *tables)


def _memory_kv(mem_ref, gmem_ref, wkv_ref, gk_ref, kt_s, v_s):
    mem_n = _head_rms(mem_ref[...], gmem_ref[...]).astype(BF16)
    kv = _dot(mem_n, wkv_ref[...].astype(BF16))
    gk = gk_ref[...]
    for hd in range(N_MEM_HEADS):
        sl = slice(hd * HEAD_DIM, (hd + 1) * HEAD_DIM)
        kt_s[sl, :] = jnp.transpose(_head_rms(kv[:, sl], gk)).astype(kt_s.dtype)
    v_s[...] = kv[:, MEM_WIDTH:].astype(v_s.dtype)


def _branches_kernel(h_ref, wu_ref, wv_ref, wqm_ref, gs_ref, ws_ref, b_ref, gq_ref,
                     mem_ref, gmem_ref, wkv_ref, gk_ref, wgo_ref, wmo_ref,
                     gm_ref, ym_ref, wgob_ref, wmob_ref, wzb_ref, wqb_ref, kt_s, v_s, *, tm):
    @pl.when(pl.program_id(0) == 0)
    def _():
        wzb_ref[:, :GMLP_WIDTH] = wu_ref[...].astype(BF16)
        wzb_ref[:, GMLP_WIDTH:] = wv_ref[...].astype(BF16)
        wqb_ref[...] = wqm_ref[...].astype(BF16)
        _memory_kv(mem_ref, gmem_ref, wkv_ref, gk_ref, kt_s, v_s)

    _cast_side_job([(wgo_ref, wgob_ref), (wmo_ref, wmob_ref)])
    h = h_ref[...]
    z = jax.nn.gelu(_dot(h, wzb_ref[...]))
    acc = _dot(h, wqb_ref[...])

    u = z[:, :GMLP_WIDTH]
    vn = _head_rms(z[:, GMLP_WIDTH:], gs_ref[...]).astype(BF16)
    bias = b_ref[...]
    for c in range(tm // CHUNK):
        rows = slice(c * CHUNK, (c + 1) * CHUNK)
        for grp in range(GMLP_GROUPS):
            cols = slice(grp * HEAD_DIM, (grp + 1) * HEAD_DIM)
            mixed = _dot(ws_ref[grp].astype(BF16), vn[rows, cols]) + bias[:, cols]
            gm_ref[rows, cols] = (u[rows, cols] * mixed).astype(gm_ref.dtype)

    g = gq_ref[...] * Q_SCALE
    for hd in range(N_MEM_HEADS):
        sl = slice(hd * HEAD_DIM, (hd + 1) * HEAD_DIM)
        qn = _head_rms(acc[:, sl], g).astype(BF16)
        s = _dot(qn, kt_s[sl, :])
        p = jnp.exp2(s - jnp.max(s, axis=-1, keepdims=True))
        l = jnp.sum(p, axis=-1, keepdims=True)
        o = _dot(p.astype(BF16), v_s[:, sl])
        ym_ref[:, sl] = (o / l).astype(ym_ref.dtype)


def _branches(h, w_in, sgu_gain, w_spatial, bias_full, mq_gain, mem, mem_gain, w_mem_kv, mk_gain,
              w_go, w_mo, tm):
    m, d = h.shape
    mt = mem.shape[0]
    row = lambda i: (i, 0)
    fixed = lambda i: (0, 0)
    once = pl.Buffered(1)
    side_in, side_out, side_shapes = _side_specs([w_go, w_mo], m // tm, row)
    return pl.pallas_call(
        functools.partial(_branches_kernel, tm=tm),
        out_shape=[jax.ShapeDtypeStruct((m, GMLP_WIDTH), BF16),
                   jax.ShapeDtypeStruct((m, MEM_WIDTH), BF16)] + side_shapes,
        grid=(m // tm,),
        in_specs=[pl.BlockSpec((tm, d), row),
                  pl.BlockSpec((d, GMLP_WIDTH), lambda i: (0, IN_U_BLK), pipeline_mode=once),
                  pl.BlockSpec((d, GMLP_WIDTH), lambda i: (0, IN_V_BLK), pipeline_mode=once),
                  pl.BlockSpec((d, MEM_WIDTH), lambda i: (0, IN_QM_BLK), pipeline_mode=once),
                  pl.BlockSpec((1, GMLP_WIDTH), fixed),
                  pl.BlockSpec((GMLP_GROUPS, CHUNK, CHUNK), lambda i: (0, 0, 0)),
                  pl.BlockSpec((CHUNK, GMLP_WIDTH), fixed),
                  pl.BlockSpec((1, HEAD_DIM), fixed),
                  pl.BlockSpec((mt, d), fixed, pipeline_mode=once),
                  pl.BlockSpec((1, d), fixed),
                  pl.BlockSpec((d, 2 * MEM_WIDTH), fixed, pipeline_mode=once),
                  pl.BlockSpec((1, HEAD_DIM), fixed)] + side_in,
        out_specs=[pl.BlockSpec((tm, GMLP_WIDTH), row),
                   pl.BlockSpec((tm, MEM_WIDTH), row)] + side_out,
        scratch_shapes=[pltpu.VMEM((d, 2 * GMLP_WIDTH), BF16), pltpu.VMEM((d, MEM_WIDTH), BF16),
                        pltpu.VMEM((MEM_WIDTH, mt), BF16), pltpu.VMEM((mt, MEM_WIDTH), BF16)],
        compiler_params=_params("arbitrary"),
        name="branches",
    )(h, w_in, w_in, w_in, sgu_gain.reshape(1, GMLP_WIDTH), w_spatial, bias_full,
      mq_gain.reshape(1, HEAD_DIM), mem, mem_gain.reshape(1, d), w_mem_kv,
      mk_gain.reshape(1, HEAD_DIM), w_go, w_mo)


def _store_heads(o_ref, out, tq):
    for r in range(Q_PER_KV):
        o_ref[:, r * HEAD_DIM:(r + 1) * HEAD_DIM] = out[r * tq:(r + 1) * tq].astype(o_ref.dtype)


def _attn_unshifted(q_ref, kt_ref, v_ref, o_ref, l_ref, acc_ref, tq, nk):
    l_ref[...] = jnp.zeros_like(l_ref)
    acc_ref[...] = jnp.zeros_like(acc_ref)

    def body(j, carry):
        off = pl.multiple_of(j * KV_BLOCK, KV_BLOCK)
        p = jnp.exp2(_dot(q_ref[0, 0], kt_ref[j]))
        l_ref[...] += p[:, :HEAD_DIM] + p[:, HEAD_DIM:]
        acc_ref[...] += _dot(p.astype(BF16), v_ref[pl.ds(off, KV_BLOCK), :])
        return carry

    lax.fori_loop(0, nk, body, 0, unroll=True)
    l = jnp.sum(l_ref[...], axis=-1, keepdims=True)
    _store_heads(o_ref, acc_ref[...] / l, tq)


def _attn_online(q_ref, kt_ref, v_ref, o_ref, tq, nk):
    q = q_ref[0, 0]
    rows = Q_PER_KV * tq

    def body(j, carry):
        m, l, acc = carry
        off = pl.multiple_of(j * KV_BLOCK, KV_BLOCK)
        s = _dot(q, kt_ref[j])
        m_new = jnp.maximum(m, jnp.max(s, axis=-1, keepdims=True))
        alpha = jnp.exp2(m - m_new)
        p = jnp.exp2(s - m_new)
        l = alpha * l + jnp.sum(p, axis=-1, keepdims=True)
        acc = alpha * acc + _dot(p.astype(BF16), v_ref[pl.ds(off, KV_BLOCK), :])
        return m_new, l, acc

    m0 = jnp.full((rows, 1), -jnp.inf, F32)
    l0 = jnp.zeros((rows, 1), F32)
    acc0 = jnp.zeros((rows, HEAD_DIM), F32)
    _, l, acc = lax.fori_loop(0, nk, body, (m0, l0, acc0))
    _store_heads(o_ref, acc / l, tq)


def _attn_kernel(unshifted_ref, q_ref, kt_ref, v_ref, *rest, tq, nk, nside):
    side = rest[:nside]
    o_ref = rest[nside]
    side_out = rest[nside + 1:2 * nside + 1]
    l_ref, acc_ref = rest[2 * nside + 1:]
    _cast_side_job(zip(side, side_out))
    unshifted = unshifted_ref[0] != 0

    @pl.when(unshifted)
    def _():
        _attn_unshifted(q_ref, kt_ref, v_ref, o_ref, l_ref, acc_ref, tq, nk)

    @pl.when(jnp.logical_not(unshifted))
    def _():
        _attn_online(q_ref, kt_ref, v_ref, o_ref, tq, nk)


def _attention(unshifted, q, kt, v, side_weights, side_cols):
    nq, _, rows, _ = q.shape
    tq = rows // Q_PER_KV
    s = v.shape[0]
    nk = s // KV_BLOCK
    gw = Q_PER_KV * HEAD_DIM
    nside = len(side_weights)
    side_in, side_out, side_shapes = _side_specs(side_weights, N_KV_HEADS * nq,
                                                 lambda g, i: (g * nq + i, 0), side_cols)
    return pl.pallas_call(
        functools.partial(_attn_kernel, tq=tq, nk=nk, nside=nside),
        out_shape=[jax.ShapeDtypeStruct((s, ATTN_Q_W), BF16)] + side_shapes,
        grid=(N_KV_HEADS, nq),
        in_specs=[pl.BlockSpec(memory_space=pltpu.SMEM),
                  pl.BlockSpec((1, 1, rows, HEAD_DIM), lambda g, i: (i, g, 0, 0)),
                  pl.BlockSpec((nk, HEAD_DIM, KV_BLOCK), lambda g, i: (0, g, 0)),
                  pl.BlockSpec((s, HEAD_DIM), lambda g, i: (0, g))] + side_in,
        out_specs=[pl.BlockSpec((tq, gw), lambda g, i: (i, g))] + side_out,
        scratch_shapes=[pltpu.VMEM((rows, HEAD_DIM), F32), pltpu.VMEM((rows, HEAD_DIM), F32)],
        compiler_params=_params("arbitrary", "arbitrary"),
        name="flash_attn",
    )(unshifted, q, kt, v, *side_weights)


def _gated_merge_kernel(h_ref, a_ref, gm_ref, ym_ref, wga_ref, wgg_ref, wgm_ref,
                        wa_ref, wg_ref, wm_ref, o_ref):
    h = h_ref[...]
    y = None
    for y_ref, wgate_ref, w_ref in ((a_ref, wga_ref, wa_ref), (gm_ref, wgg_ref, wg_ref),
                                    (ym_ref, wgm_ref, wm_ref)):
        gate = 1.0 / (1.0 + jnp.exp(-_dot(h, wgate_ref[...])))
        term = gate * _dot(y_ref[...], w_ref[...])
        y = term if y is None else y + term
    o_ref[...] = y.astype(o_ref.dtype)


def _gated_merge(h, attn, gm, ym, w_gate, w_a, w_g, w_m, tm, tn):
    m, d = h.shape
    n = w_a.shape[1]
    nb = n // tn
    row = lambda j, i: (i, 0)
    col = lambda j, i: (0, j)
    return pl.pallas_call(
        _gated_merge_kernel,
        out_shape=jax.ShapeDtypeStruct((m, n), BF16),
        grid=(nb, m // tm),
        in_specs=[pl.BlockSpec((tm, d), row),
                  pl.BlockSpec((tm, attn.shape[1]), row),
                  pl.BlockSpec((tm, gm.shape[1]), row),
                  pl.BlockSpec((tm, ym.shape[1]), row),
                  pl.BlockSpec((d, tn), lambda j, i: (0, j)),
                  pl.BlockSpec((d, tn), lambda j, i: (0, nb + j)),
                  pl.BlockSpec((d, tn), lambda j, i: (0, 2 * nb + j)),
                  pl.BlockSpec((w_a.shape[0], tn), col),
                  pl.BlockSpec((w_g.shape[0], tn), col),
                  pl.BlockSpec((w_m.shape[0], tn), col)],
        out_specs=pl.BlockSpec((tm, tn), lambda j, i: (i, j)),
        compiler_params=_params("arbitrary", "arbitrary"),
        name="gated_merge",
    )(h, attn, gm, ym, w_gate, w_gate, w_gate, w_a, w_g, w_m)


def _outproj_kernel(mg_ref, w_ref, x_ref, g_ref, x1_ref, h2_ref):
    x1 = x_ref[...] + _dot(mg_ref[...], w_ref[...])
    x1_ref[...] = x1
    h2_ref[...] = _head_rms(x1, g_ref[...]).astype(h2_ref.dtype)


def _outproj(merged, w, x, gain, tm):
    m, d = x.shape
    row = lambda i: (i, 0)
    fixed = lambda i: (0, 0)
    return pl.pallas_call(
        _outproj_kernel,
        out_shape=(jax.ShapeDtypeStruct((m, d), F32), jax.ShapeDtypeStruct((m, d), BF16)),
        grid=(m // tm,),
        in_specs=[pl.BlockSpec((tm, d), row), pl.BlockSpec((d, d), fixed),
                  pl.BlockSpec((tm, d), row), pl.BlockSpec((1, d), fixed)],
        out_specs=(pl.BlockSpec((tm, d), row), pl.BlockSpec((tm, d), row)),
        compiler_params=_params("parallel"),
        name="out_proj",
    )(merged, w, x, gain.reshape(1, d))


def _ffn_up_kernel(h_ref, w_ref, o_ref):
    z = jnp.maximum(_dot(h_ref[...], w_ref[...]), 0.0)
    o_ref[...] = (z * z).astype(o_ref.dtype)


def _ffn_up(h, w, tm, tn):
    m, d = h.shape
    n = w.shape[1]
    return pl.pallas_call(
        _ffn_up_kernel,
        out_shape=jax.ShapeDtypeStruct((m, n), BF16),
        grid=(m // tm, n // tn),
        in_specs=[pl.BlockSpec((tm, d), lambda i, j: (i, 0)),
                  pl.BlockSpec((d, tn), lambda i, j: (0, j))],
        out_specs=pl.BlockSpec((tm, tn), lambda i, j: (i, j)),
        compiler_params=_params("parallel", "parallel"),
        name="ffn_up",
    )(h, w)


def _ffn_down_kernel(a_ref, w_ref, x_ref, o_ref):
    k = pl.program_id(2)

    @pl.when(k == 0)
    def _():
        o_ref[...] = x_ref[...] + _dot(a_ref[...], w_ref[...])

    @pl.when(k != 0)
    def _():
        o_ref[...] += _dot(a_ref[...], w_ref[...])


def _ffn_down(a, w, x1, tm, tn, tk):
    m, kdim = a.shape
    n = w.shape[1]
    return pl.pallas_call(
        _ffn_down_kernel,
        out_shape=jax.ShapeDtypeStruct((m, n), F32),
        grid=(m // tm, n // tn, kdim // tk),
        in_specs=[pl.BlockSpec((tm, tk), lambda i, j, k: (i, k)),
                  pl.BlockSpec((tk, tn), lambda i, j, k: (k, j)),
                  pl.BlockSpec((tm, tn), lambda i, j, k: (i, j))],
        out_specs=pl.BlockSpec((tm, tn), lambda i, j, k: (i, j)),
        compiler_params=_params("parallel", "parallel", "arbitrary"),
        name="ffn_down",
    )(a, w, x1)


def _rope_tables(s):
    inv = ROPE_THETA ** (-jnp.arange(ROPE_PAIRS, dtype=F32) / ROPE_PAIRS)
    inv_lane = jnp.tile(inv, HEAD_DIM // ROPE_PAIRS)
    lane_grp = jnp.arange(HEAD_DIM) // ROPE_PAIRS
    is_row = (lane_grp % 2) == 0
    sign = jnp.where(lane_grp < 2, -1.0, 1.0).astype(F32)

    def tables(n):
        ang = jnp.arange(n, dtype=F32)[:, None] * inv_lane
        return jnp.cos(ang), jnp.sin(ang) * sign

    cr, sr = tables(s // GRID_W)
    cc, sc = tables(GRID_W)
    return (jnp.where(is_row, cr, 0.0), jnp.where(is_row, sr, 0.0),
            jnp.where(is_row, 0.0, cc), jnp.where(is_row, 0.0, sc))


def kernel(x, mem, norm_mix, w_in, q_norm, k_norm, sgu_norm, w_spatial, b_spatial, mem_norm,
           w_mem_kv, mq_norm, mk_norm, w_attn_o, w_gmlp_o, w_mem_o, w_out, norm_ffn,
           w_ffn_up, w_ffn_down):
    b, s, d = x.shape
    assert b == 1 and d == D_MODEL and norm_mix.shape[0] == 1
    xs = x[0]
    w_in0 = w_in[0]
    tables = _rope_tables(s)
    bias_full = jnp.repeat(b_spatial[0].T, HEAD_DIM, axis=1)

    h, kt, v, q = _qkvproj(xs, norm_mix[0], w_in0, k_norm[0], q_norm[0], tables, TM_PROJ,
                           ATTN_TQ)
    gm, ym, w_gmlp_o_b, w_mem_o_b = _branches(
        h, w_in0, sgu_norm[0], w_spatial[0], bias_full, mq_norm[0], mem[0], mem_norm[0],
        w_mem_kv[0], mk_norm[0], w_gmlp_o[0], w_mem_o[0], TM_PROJ)

    logit_bound = (1.02 * LOG2E * math.sqrt(HEAD_DIM)
                   * jnp.max(jnp.abs(q_norm[0])) * jnp.max(jnp.abs(k_norm[0])))
    n_gate = w_in0.shape[1] - IN_GATE_BLK * IN_COL_BLOCK
    unshifted = (logit_bound <= MAX_UNSHIFTED_LOGIT).astype(jnp.int32).reshape(1)
    attn, w_gate_b, w_attn_o_b, w_out_b, w_up_b, w_down_b = _attention(
        unshifted, q, kt, v, (w_in0, w_attn_o[0], w_out[0], w_ffn_up[0], w_ffn_down[0]),
        (n_gate, None, None, None, None))

    merged = _gated_merge(h, attn, gm, ym, w_gate_b, w_attn_o_b, w_gmlp_o_b, w_mem_o_b,
                          TM_MERGE, TN_MERGE)
    x1, h2 = _outproj(merged, w_out_b, xs, norm_ffn[0], TM_OUT)
    a = _ffn_up(h2, w_up_b, TM_UP, TN_UP)
    out = _ffn_down(a, w_down_b, x1, TM_DOWN, TN_DOWN, TK_DOWN)
    return out[None]
```

```python
import functools
import math

import jax
import jax.numpy as jnp
from jax import lax
from jax.experimental import pallas as pl
from jax.experimental.pallas import tpu as pltpu

D_MODEL = 2048
HEAD_DIM = 128
N_Q_HEADS = 8
N_KV_HEADS = 2
Q_PER_KV = N_Q_HEADS // N_KV_HEADS
GRID_W = 64
ROPE_THETA = 10000.0
ROPE_PAIRS = HEAD_DIM // 4
GMLP_GROUPS = 4
GMLP_WIDTH = GMLP_GROUPS * HEAD_DIM
CHUNK = 128
N_MEM_HEADS = 4
MEM_WIDTH = N_MEM_HEADS * HEAD_DIM
EPS = 1e-6
ATTN_Q_W = N_Q_HEADS * HEAD_DIM
ATTN_KV_W = N_KV_HEADS * HEAD_DIM
IN_COL_BLOCK = 512
IN_Q_BLK, IN_KV_BLK, IN_U_BLK, IN_V_BLK, IN_QM_BLK, IN_GATE_BLK = 0, 2, 3, 4, 5, 6

VMEM_LIMIT_BYTES = 56 * 1024 * 1024
LOG2E = math.log2(math.e)
Q_SCALE = HEAD_DIM ** -0.5 * LOG2E
KV_BLOCK = 256
ATTN_TQ = 512
MAX_UNSHIFTED_LOGIT = 64.0

TM_PROJ = 512
TM_BRANCH = 1024
TM_MERGE, TN_MERGE = 1024, 512
TM_OUT = 512
TM_UP, TN_UP = 2048, 1024
TM_DOWN, TN_DOWN, TK_DOWN = 1024, 2048, 1024

F32 = jnp.float32
BF16 = jnp.bfloat16


def _params(*sem):
    return pltpu.CompilerParams(dimension_semantics=sem, vmem_limit_bytes=VMEM_LIMIT_BYTES)


def _dot(a, b):
    return jnp.dot(a, b, preferred_element_type=F32)


def _head_rms(x, gain):
    ms = jnp.mean(x * x, axis=-1, keepdims=True)
    return x * lax.rsqrt(ms + EPS) * gain


def _rope(y, cos, sin_signed):
    return y * cos + pltpu.roll(y, HEAD_DIM // 2, 1) * sin_signed


def _to_rope_order(w):
    grp = lax.broadcasted_iota(jnp.int32, w.shape, 1) // ROPE_PAIRS
    from_right = pltpu.roll(w, HEAD_DIM - ROPE_PAIRS, 1)
    from_left = pltpu.roll(w, ROPE_PAIRS, 1)
    return jnp.where(grp == 1, from_right, jnp.where(grp == 2, from_left, w))


def _gain_in_rope_order(g_ref):
    return _to_rope_order(jnp.broadcast_to(g_ref[...], (8, HEAD_DIM)))[0:1]


def _rope_tile(row_ref, col_ref, i, tm):
    nrow = tm // GRID_W
    rows = row_ref[pl.ds(pl.multiple_of(i * nrow, nrow), nrow), :]
    col = col_ref[...]
    return jnp.concatenate([rows[r:r + 1, :] + col for r in range(nrow)], axis=0)


def _lagged_step(i, project, finish, buf_a, buf_b):
    @pl.when(i % 2 == 0)
    def _():
        project(buf_a)
        finish(buf_b)

    @pl.when(i % 2 == 1)
    def _():
        project(buf_b)
        finish(buf_a)


def _lag_index_maps(n_tiles):
    return (lambda i: jnp.minimum(i, n_tiles - 1)), (lambda i: jnp.maximum(i - 1, 0))


def _cast_side_job(pairs):
    for src, dst in pairs:
        skip = src.shape[1] - dst.shape[1]
        dst[...] = src[:, skip:].astype(dst.dtype)


def _side_specs(arrays, nsteps, index, out_cols=None):
    in_specs, out_specs, out_shapes = [], [], []
    for k, a in enumerate(arrays):
        rows, cols = a.shape
        keep = cols if out_cols is None or out_cols[k] is None else out_cols[k]
        in_specs.append(pl.BlockSpec((rows // nsteps, cols), index))
        out_specs.append(pl.BlockSpec((rows // nsteps, keep), index))
        out_shapes.append(jax.ShapeDtypeStruct((rows, keep), BF16))
    return in_specs, out_specs, out_shapes


def _qkvproj_kernel(x_ref, gx_ref, wkv_ref, wq_ref, gk_ref, gq_ref, crow_ref, srow_ref, ccol_ref,
                    scol_ref, h_ref, kt_ref, v_ref, q_ref, wkvb_ref, wqb_ref,
                    kv_a, kv_b, q_a, q_b, *, tm, tq):
    i = pl.program_id(0)

    @pl.when(i == 0)
    def _():
        for hd in range(N_KV_HEADS):
            sl = slice(hd * HEAD_DIM, (hd + 1) * HEAD_DIM)
            wkvb_ref[:, sl] = _to_rope_order(wkv_ref[:, sl]).astype(BF16)
        wkvb_ref[:, ATTN_KV_W:] = wkv_ref[:, ATTN_KV_W:].astype(BF16)
        for hd in range(N_Q_HEADS):
            sl = slice(hd * HEAD_DIM, (hd + 1) * HEAD_DIM)
            wqb_ref[:, sl] = _to_rope_order(wq_ref[:, sl]).astype(BF16)
        kv_b[...] = jnp.zeros_like(kv_b)
        q_b[...] = jnp.zeros_like(q_b)

    def project(bufs):
        kv_acc, q_acc = bufs
        h = _head_rms(x_ref[...], gx_ref[...]).astype(BF16)
        h_ref[...] = h
        kv_acc[...] = _dot(h, wkvb_ref[...])
        q_acc[...] = _dot(h, wqb_ref[...])

    def finish(bufs):
        kv_acc, q_acc = bufs
        prev = jnp.maximum(i - 1, 0)
        cos = _rope_tile(crow_ref, ccol_ref, prev, tm)
        sin = _rope_tile(srow_ref, scol_ref, prev, tm)
        gk = _gain_in_rope_order(gk_ref)
        for hd in range(N_KV_HEADS):
            sl = slice(hd * HEAD_DIM, (hd + 1) * HEAD_DIM)
            y = _rope(_head_rms(kv_acc[:, sl], gk), cos, sin)
            for c in range(tm // KV_BLOCK):
                blk = y[c * KV_BLOCK:(c + 1) * KV_BLOCK, :]
                kt_ref[c, sl, :] = jnp.transpose(blk).astype(kt_ref.dtype)
        v_ref[...] = kv_acc[:, ATTN_KV_W:].astype(v_ref.dtype)
        gq = _gain_in_rope_order(gq_ref) * Q_SCALE
        for hd in range(N_Q_HEADS):
            grp, r = divmod(hd, Q_PER_KV)
            x = q_acc[:, hd * HEAD_DIM:(hd + 1) * HEAD_DIM]
            y = _rope(_head_rms(x, gq), cos, sin).astype(q_ref.dtype)
            for t in range(tm // tq):
                q_ref[t, grp, r * tq:(r + 1) * tq, :] = y[t * tq:(t + 1) * tq]

    _lagged_step(i, project, finish, (kv_a, q_a), (kv_b, q_b))


def _qkvproj(x, gx, w_in, gk, gq, tables, tm, tq):
    m, d = x.shape
    n_tiles = m // tm
    cur, prev = _lag_index_maps(n_tiles)
    fixed = lambda i: (0, 0)
    nkv = 2 * ATTN_KV_W
    tab_specs = [pl.BlockSpec(t.shape, fixed) for t in tables]
    qshape = (m // tq, N_KV_HEADS, Q_PER_KV * tq, HEAD_DIM)
    once = pl.Buffered(1)
    return pl.pallas_call(
        functools.partial(_qkvproj_kernel, tm=tm, tq=tq),
        out_shape=(jax.ShapeDtypeStruct((m, d), BF16),
                   jax.ShapeDtypeStruct((m // KV_BLOCK, ATTN_KV_W, KV_BLOCK), BF16),
                   jax.ShapeDtypeStruct((m, ATTN_KV_W), BF16),
                   jax.ShapeDtypeStruct(qshape, BF16)),
        grid=(n_tiles + 1,),
        in_specs=[pl.BlockSpec((tm, d), lambda i: (cur(i), 0)), pl.BlockSpec((1, d), fixed),
                  pl.BlockSpec((d, nkv), lambda i: (0, IN_KV_BLK), pipeline_mode=once),
                  pl.BlockSpec((d, ATTN_Q_W), lambda i: (0, IN_Q_BLK), pipeline_mode=once),
                  pl.BlockSpec((1, HEAD_DIM), fixed),
                  pl.BlockSpec((1, HEAD_DIM), fixed)] + tab_specs,
        out_specs=(pl.BlockSpec((tm, d), lambda i: (cur(i), 0)),
                   pl.BlockSpec((tm // KV_BLOCK, ATTN_KV_W, KV_BLOCK),
                                lambda i: (prev(i), 0, 0)),
                   pl.BlockSpec((tm, ATTN_KV_W), lambda i: (prev(i), 0)),
                   pl.BlockSpec((tm // tq,) + qshape[1:], lambda i: (prev(i), 0, 0, 0))),
        scratch_shapes=[pltpu.VMEM((d, nkv), BF16), pltpu.VMEM((d, ATTN_Q_W), BF16),
                        pltpu.VMEM((tm, nkv), F32), pltpu.VMEM((tm, nkv), F32),
                        pltpu.VMEM((tm, ATTN_Q_W), F32), pltpu.VMEM((tm, ATTN_Q_W), F32)],
        compiler_params=_params("arbitrary"),
        name="qkv_proj",
    )(x, gx.reshape(1, d), w_in, w_in, gk.reshape(1, HEAD_DIM), gq.reshape(1, HEAD_DIM),
      *tables)


def _memory_kv(mem_ref, gmem_ref, wkv_ref, gk_ref, kt_s, v_s):
    mem_n = _head_rms(mem_ref[...], gmem_ref[...]).astype(BF16)
    kv = _dot(mem_n, wkv_ref[...].astype(BF16))
    gk = gk_ref[...]
    for hd in range(N_MEM_HEADS):
        sl = slice(hd * HEAD_DIM, (hd + 1) * HEAD_DIM)
        kt_s[sl, :] = jnp.transpose(_head_rms(kv[:, sl], gk)).astype(kt_s.dtype)
    v_s[...] = kv[:, MEM_WIDTH:].astype(v_s.dtype)


def _branches_kernel(h_ref, wu_ref, wv_ref, wqm_ref, gs_ref, ws_ref, b_ref, gq_ref,
                     mem_ref, gmem_ref, wkv_ref, gk_ref, wgo_ref, wmo_ref,
                     gm_ref, ym_ref, wgob_ref, wmob_ref, wzb_ref, wqb_ref, kt_s, v_s, *, tm):
    @pl.when(pl.program_id(0) == 0)
    def _():
        wzb_ref[:, :GMLP_WIDTH] = wu_ref[...].astype(BF16)
        wzb_ref[:, GMLP_WIDTH:] = wv_ref[...].astype(BF16)
        wqb_ref[...] = wqm_ref[...].astype(BF16)
        _memory_kv(mem_ref, gmem_ref, wkv_ref, gk_ref, kt_s, v_s)

    _cast_side_job([(wgo_ref, wgob_ref), (wmo_ref, wmob_ref)])
    h = h_ref[...]
    z = jax.nn.gelu(_dot(h, wzb_ref[...]))
    acc = _dot(h, wqb_ref[...])

    u = z[:, :GMLP_WIDTH]
    vn = _head_rms(z[:, GMLP_WIDTH:], gs_ref[...]).astype(BF16)
    bias = b_ref[...]
    for c in range(tm // CHUNK):
        rows = slice(c * CHUNK, (c + 1) * CHUNK)
        for grp in range(GMLP_GROUPS):
            cols = slice(grp * HEAD_DIM, (grp + 1) * HEAD_DIM)
            mixed = _dot(ws_ref[grp].astype(BF16), vn[rows, cols]) + bias[:, cols]
            gm_ref[rows, cols] = (u[rows, cols] * mixed).astype(gm_ref.dtype)

    g = gq_ref[...] * Q_SCALE
    for hd in range(N_MEM_HEADS):
        sl = slice(hd * HEAD_DIM, (hd + 1) * HEAD_DIM)
        qn = _head_rms(acc[:, sl], g).astype(BF16)
        s = _dot(qn, kt_s[sl, :])
        p = jnp.exp2(s - jnp.max(s, axis=-1, keepdims=True))
        l = jnp.sum(p, axis=-1, keepdims=True)
        o = _dot(p.astype(BF16), v_s[:, sl])
        ym_ref[:, sl] = (o / l).astype(ym_ref.dtype)


def _branches(h, w_in, sgu_gain, w_spatial, bias_full, mq_gain, mem, mem_gain, w_mem_kv, mk_gain,
              w_go, w_mo, tm):
    m, d = h.shape
    mt = mem.shape[0]
    row = lambda i: (i, 0)
    fixed = lambda i: (0, 0)
    once = pl.Buffered(1)
    side_in, side_out, side_shapes = _side_specs([w_go, w_mo], m // tm, row)
    return pl.pallas_call(
        functools.partial(_branches_kernel, tm=tm),
        out_shape=[jax.ShapeDtypeStruct((m, GMLP_WIDTH), BF16),
                   jax.ShapeDtypeStruct((m, MEM_WIDTH), BF16)] + side_shapes,
        grid=(m // tm,),
        in_specs=[pl.BlockSpec((tm, d), row),
                  pl.BlockSpec((d, GMLP_WIDTH), lambda i: (0, IN_U_BLK), pipeline_mode=once),
                  pl.BlockSpec((d, GMLP_WIDTH), lambda i: (0, IN_V_BLK), pipeline_mode=once),
                  pl.BlockSpec((d, MEM_WIDTH), lambda i: (0, IN_QM_BLK), pipeline_mode=once),
                  pl.BlockSpec((1, GMLP_WIDTH), fixed),
                  pl.BlockSpec((GMLP_GROUPS, CHUNK, CHUNK), lambda i: (0, 0, 0)),
                  pl.BlockSpec((CHUNK, GMLP_WIDTH), fixed),
                  pl.BlockSpec((1, HEAD_DIM), fixed),
                  pl.BlockSpec((mt, d), fixed, pipeline_mode=once),
                  pl.BlockSpec((1, d), fixed),
                  pl.BlockSpec((d, 2 * MEM_WIDTH), fixed, pipeline_mode=once),
                  pl.BlockSpec((1, HEAD_DIM), fixed)] + side_in,
        out_specs=[pl.BlockSpec((tm, GMLP_WIDTH), row),
                   pl.BlockSpec((tm, MEM_WIDTH), row)] + side_out,
        scratch_shapes=[pltpu.VMEM((d, 2 * GMLP_WIDTH), BF16), pltpu.VMEM((d, MEM_WIDTH), BF16),
                        pltpu.VMEM((MEM_WIDTH, mt), BF16), pltpu.VMEM((mt, MEM_WIDTH), BF16)],
        compiler_params=_params("arbitrary"),
        name="branches",
    )(h, w_in, w_in, w_in, sgu_gain.reshape(1, GMLP_WIDTH), w_spatial, bias_full,
      mq_gain.reshape(1, HEAD_DIM), mem, mem_gain.reshape(1, d), w_mem_kv,
      mk_gain.reshape(1, HEAD_DIM), w_go, w_mo)


def _store_heads(o_ref, out, tq):
    for r in range(Q_PER_KV):
        o_ref[:, r * HEAD_DIM:(r + 1) * HEAD_DIM] = out[r * tq:(r + 1) * tq].astype(o_ref.dtype)


def _attn_unshifted(q_ref, kt_ref, v_ref, o_ref, l_ref, acc_ref, tq, nk):
    l_ref[...] = jnp.zeros_like(l_ref)
    acc_ref[...] = jnp.zeros_like(acc_ref)

    def body(j, carry):
        off = pl.multiple_of(j * KV_BLOCK, KV_BLOCK)
        p = jnp.exp2(_dot(q_ref[0, 0], kt_ref[j]))
        l_ref[...] += p[:, :HEAD_DIM] + p[:, HEAD_DIM:]
        acc_ref[...] += _dot(p.astype(BF16), v_ref[pl.ds(off, KV_BLOCK), :])
        return carry

    lax.fori_loop(0, nk, body, 0, unroll=True)
    l = jnp.sum(l_ref[...], axis=-1, keepdims=True)
    _store_heads(o_ref, acc_ref[...] / l, tq)


def _attn_online(q_ref, kt_ref, v_ref, o_ref, tq, nk):
    q = q_ref[0, 0]
    rows = Q_PER_KV * tq

    def body(j, carry):
        m, l, acc = carry
        off = pl.multiple_of(j * KV_BLOCK, KV_BLOCK)
        s = _dot(q, kt_ref[j])
        m_new = jnp.maximum(m, jnp.max(s, axis=-1, keepdims=True))
        alpha = jnp.exp2(m - m_new)
        p = jnp.exp2(s - m_new)
        l = alpha * l + jnp.sum(p, axis=-1, keepdims=True)
        acc = alpha * acc + _dot(p.astype(BF16), v_ref[pl.ds(off, KV_BLOCK), :])
        return m_new, l, acc

    m0 = jnp.full((rows, 1), -jnp.inf, F32)
    l0 = jnp.zeros((rows, 1), F32)
    acc0 = jnp.zeros((rows, HEAD_DIM), F32)
    _, l, acc = lax.fori_loop(0, nk, body, (m0, l0, acc0))
    _store_heads(o_ref, acc / l, tq)


def _attn_kernel(unshifted_ref, q_ref, kt_ref, v_ref, *rest, tq, nk, nside):
    side = rest[:nside]
    o_ref = rest[nside]
    side_out = rest[nside + 1:2 * nside + 1]
    l_ref, acc_ref = rest[2 * nside + 1:]
    _cast_side_job(zip(side, side_out))
    unshifted = unshifted_ref[0] != 0

    @pl.when(unshifted)
    def _():
        _attn_unshifted(q_ref, kt_ref, v_ref, o_ref, l_ref, acc_ref, tq, nk)

    @pl.when(jnp.logical_not(unshifted))
    def _():
        _attn_online(q_ref, kt_ref, v_ref, o_ref, tq, nk)


def _attention(unshifted, q, kt, v, side_weights, side_cols):
    nq, _, rows, _ = q.shape
    tq = rows // Q_PER_KV
    s = v.shape[0]
    nk = s // KV_BLOCK
    gw = Q_PER_KV * HEAD_DIM
    nside = len(side_weights)
    side_in, side_out, side_shapes = _side_specs(side_weights, N_KV_HEADS * nq,
                                                 lambda g, i: (g * nq + i, 0), side_cols)
    return pl.pallas_call(
        functools.partial(_attn_kernel, tq=tq, nk=nk, nside=nside),
        out_shape=[jax.ShapeDtypeStruct((s, ATTN_Q_W), BF16)] + side_shapes,
        grid=(N_KV_HEADS, nq),
        in_specs=[pl.BlockSpec(memory_space=pltpu.SMEM),
                  pl.BlockSpec((1, 1, rows, HEAD_DIM), lambda g, i: (i, g, 0, 0)),
                  pl.BlockSpec((nk, HEAD_DIM, KV_BLOCK), lambda g, i: (0, g, 0)),
                  pl.BlockSpec((s, HEAD_DIM), lambda g, i: (0, g))] + side_in,
        out_specs=[pl.BlockSpec((tq, gw), lambda g, i: (i, g))] + side_out,
        scratch_shapes=[pltpu.VMEM((rows, HEAD_DIM), F32), pltpu.VMEM((rows, HEAD_DIM), F32)],
        compiler_params=_params("arbitrary", "arbitrary"),
        name="flash_attn",
    )(unshifted, q, kt, v, *side_weights)


def _gated_merge_kernel(h_ref, a_ref, gm_ref, ym_ref, wga_ref, wgg_ref, wgm_ref,
                        wa_ref, wg_ref, wm_ref, o_ref):
    h = h_ref[...]
    y = None
    for y_ref, wgate_ref, w_ref in ((a_ref, wga_ref, wa_ref), (gm_ref, wgg_ref, wg_ref),
                                    (ym_ref, wgm_ref, wm_ref)):
        gate = 1.0 / (1.0 + jnp.exp(-_dot(h, wgate_ref[...])))
        term = gate * _dot(y_ref[...], w_ref[...])
        y = term if y is None else y + term
    o_ref[...] = y.astype(o_ref.dtype)


def _gated_merge(h, attn, gm, ym, w_gate, w_a, w_g, w_m, tm, tn):
    m, d = h.shape
    n = w_a.shape[1]
    nb = n // tn
    row = lambda j, i: (i, 0)
    col = lambda j, i: (0, j)
    return pl.pallas_call(
        _gated_merge_kernel,
        out_shape=jax.ShapeDtypeStruct((m, n), BF16),
        grid=(nb, m // tm),
        in_specs=[pl.BlockSpec((tm, d), row),
                  pl.BlockSpec((tm, attn.shape[1]), row),
                  pl.BlockSpec((tm, gm.shape[1]), row),
                  pl.BlockSpec((tm, ym.shape[1]), row),
                  pl.BlockSpec((d, tn), lambda j, i: (0, j)),
                  pl.BlockSpec((d, tn), lambda j, i: (0, nb + j)),
                  pl.BlockSpec((d, tn), lambda j, i: (0, 2 * nb + j)),
                  pl.BlockSpec((w_a.shape[0], tn), col),
                  pl.BlockSpec((w_g.shape[0], tn), col),
                  pl.BlockSpec((w_m.shape[0], tn), col)],
        out_specs=pl.BlockSpec((tm, tn), lambda j, i: (i, j)),
        compiler_params=_params("arbitrary", "arbitrary"),
        name="gated_merge",
    )(h, attn, gm, ym, w_gate, w_gate, w_gate, w_a, w_g, w_m)


def _outproj_kernel(mg_ref, w_ref, x_ref, g_ref, x1_ref, h2_ref):
    x1 = x_ref[...] + _dot(mg_ref[...], w_ref[...])
    x1_ref[...] = x1
    h2_ref[...] = _head_rms(x1, g_ref[...]).astype(h2_ref.dtype)


def _outproj(merged, w, x, gain, tm):
    m, d = x.shape
    row = lambda i: (i, 0)
    fixed = lambda i: (0, 0)
    return pl.pallas_call(
        _outproj_kernel,
        out_shape=(jax.ShapeDtypeStruct((m, d), F32), jax.ShapeDtypeStruct((m, d), BF16)),
        grid=(m // tm,),
        in_specs=[pl.BlockSpec((tm, d), row), pl.BlockSpec((d, d), fixed),
                  pl.BlockSpec((tm, d), row), pl.BlockSpec((1, d), fixed)],
        out_specs=(pl.BlockSpec((tm, d), row), pl.BlockSpec((tm, d), row)),
        compiler_params=_params("parallel"),
        name="out_proj",
    )(merged, w, x, gain.reshape(1, d))


def _ffn_up_kernel(h_ref, w_ref, o_ref):
    z = jnp.maximum(_dot(h_ref[...], w_ref[...]), 0.0)
    o_ref[...] = (z * z).astype(o_ref.dtype)


def _ffn_up(h, w, tm, tn):
    m, d = h.shape
    n = w.shape[1]
    return pl.pallas_call(
        _ffn_up_kernel,
        out_shape=jax.ShapeDtypeStruct((m, n), BF16),
        grid=(m // tm, n // tn),
        in_specs=[pl.BlockSpec((tm, d), lambda i, j: (i, 0)),
                  pl.BlockSpec((d, tn), lambda i, j: (0, j))],
        out_specs=pl.BlockSpec((tm, tn), lambda i, j: (i, j)),
        compiler_params=_params("parallel", "parallel"),
        name="ffn_up",
    )(h, w)


def _ffn_down_kernel(a_ref, w_ref, x_ref, o_ref):
    k = pl.program_id(2)

    @pl.when(k == 0)
    def _():
        o_ref[...] = x_ref[...] + _dot(a_ref[...], w_ref[...])

    @pl.when(k != 0)
    def _():
        o_ref[...] += _dot(a_ref[...], w_ref[...])


def _ffn_down(a, w, x1, tm, tn, tk):
    m, kdim = a.shape
    n = w.shape[1]
    return pl.pallas_call(
        _ffn_down_kernel,
        out_shape=jax.ShapeDtypeStruct((m, n), F32),
        grid=(m // tm, n // tn, kdim // tk),
        in_specs=[pl.BlockSpec((tm, tk), lambda i, j, k: (i, k)),
                  pl.BlockSpec((tk, tn), lambda i, j, k: (k, j)),
                  pl.BlockSpec((tm, tn), lambda i, j, k: (i, j))],
        out_specs=pl.BlockSpec((tm, tn), lambda i, j, k: (i, j)),
        compiler_params=_params("parallel", "parallel", "arbitrary"),
        name="ffn_down",
    )(a, w, x1)


def _rope_tables(s):
    inv = ROPE_THETA ** (-jnp.arange(ROPE_PAIRS, dtype=F32) / ROPE_PAIRS)
    inv_lane = jnp.tile(inv, HEAD_DIM // ROPE_PAIRS)
    lane_grp = jnp.arange(HEAD_DIM) // ROPE_PAIRS
    is_row = (lane_grp % 2) == 0
    sign = jnp.where(lane_grp < 2, -1.0, 1.0).astype(F32)

    def tables(n):
        ang = jnp.arange(n, dtype=F32)[:, None] * inv_lane
        return jnp.cos(ang), jnp.sin(ang) * sign

    cr, sr = tables(s // GRID_W)
    cc, sc = tables(GRID_W)
    return (jnp.where(is_row, cr, 0.0), jnp.where(is_row, sr, 0.0),
            jnp.where(is_row, 0.0, cc), jnp.where(is_row, 0.0, sc))


def kernel(x, mem, norm_mix, w_in, q_norm, k_norm, sgu_norm, w_spatial, b_spatial, mem_norm,
           w_mem_kv, mq_norm, mk_norm, w_attn_o, w_gmlp_o, w_mem_o, w_out, norm_ffn,
           w_ffn_up, w_ffn_down):
    b, s, d = x.shape
    assert b == 1 and d == D_MODEL and norm_mix.shape[0] == 1
    xs = x[0]
    w_in0 = w_in[0]
    tables = _rope_tables(s)
    bias_full = jnp.repeat(b_spatial[0].T, HEAD_DIM, axis=1)

    h, kt, v, q = _qkvproj(xs, norm_mix[0], w_in0, k_norm[0], q_norm[0], tables, TM_PROJ,
                           ATTN_TQ)
    gm, ym, w_gmlp_o_b, w_mem_o_b = _branches(
        h, w_in0, sgu_norm[0], w_spatial[0], bias_full, mq_norm[0], mem[0], mem_norm[0],
        w_mem_kv[0], mk_norm[0], w_gmlp_o[0], w_mem_o[0], TM_BRANCH)

    logit_bound = (1.02 * LOG2E * math.sqrt(HEAD_DIM)
                   * jnp.max(jnp.abs(q_norm[0])) * jnp.max(jnp.abs(k_norm[0])))
    n_gate = w_in0.shape[1] - IN_GATE_BLK * IN_COL_BLOCK
    unshifted = (logit_bound <= MAX_UNSHIFTED_LOGIT).astype(jnp.int32).reshape(1)
    attn, w_gate_b, w_attn_o_b, w_out_b, w_up_b, w_down_b = _attention(
        unshifted, q, kt, v, (w_in0, w_attn_o[0], w_out[0], w_ffn_up[0], w_ffn_down[0]),
        (n_gate, None, None, None, None))

    merged = _gated_merge(h, attn, gm, ym, w_gate_b, w_attn_o_b, w_gmlp_o_b, w_mem_o_b,
                          TM_MERGE, TN_MERGE)
    x1, h2 = _outproj(merged, w_out_b, xs, norm_ffn[0], TM_OUT)
    a = _ffn_up(h2, w_up_b, TM_UP, TN_UP)
    out = _ffn_down(a, w_down_b, x1, TM_DOWN, TN_DOWN, TK_DOWN)
    return out[None]
```

```python
import functools
import math

import jax
import jax.numpy as jnp
from jax import lax
from jax.experimental import pallas as pl
from jax.experimental.pallas import tpu as pltpu

D_MODEL = 2048
HEAD_DIM = 128
N_Q_HEADS = 8
N_KV_HEADS = 2
Q_PER_KV = N_Q_HEADS // N_KV_HEADS
GRID_W = 64
ROPE_THETA = 10000.0
ROPE_PAIRS = HEAD_DIM // 4
GMLP_GROUPS = 4
GMLP_WIDTH = GMLP_GROUPS * HEAD_DIM
CHUNK = 128
N_MEM_HEADS = 4
MEM_WIDTH = N_MEM_HEADS * HEAD_DIM
EPS = 1e-6
ATTN_Q_W = N_Q_HEADS * HEAD_DIM
ATTN_KV_W = N_KV_HEADS * HEAD_DIM
IN_COL_BLOCK = 512
IN_Q_BLK, IN_KV_BLK, IN_U_BLK, IN_V_BLK, IN_QM_BLK, IN_GATE_BLK = 0, 2, 3, 4, 5, 6

VMEM_LIMIT_BYTES = 56 * 1024 * 1024
LOG2E = math.log2(math.e)
Q_SCALE = HEAD_DIM ** -0.5 * LOG2E
KV_BLOCK = 256
ATTN_TQ = 512
MAX_UNSHIFTED_LOGIT = 64.0

TM_PROJ = 512
TM_MERGE, TN_MERGE = 1024, 512
TM_OUT = 512
TM_UP, TN_UP = 2048, 1024
TM_DOWN, TN_DOWN, TK_DOWN = 1024, 1024, 4096

F32 = jnp.float32
BF16 = jnp.bfloat16


def _params(*sem):
    return pltpu.CompilerParams(dimension_semantics=sem, vmem_limit_bytes=VMEM_LIMIT_BYTES)


def _dot(a, b):
    return jnp.dot(a, b, preferred_element_type=F32)


def _head_rms(x, gain):
    ms = jnp.mean(x * x, axis=-1, keepdims=True)
    return x * lax.rsqrt(ms + EPS) * gain


def _rope(y, cos, sin_signed):
    return y * cos + pltpu.roll(y, HEAD_DIM // 2, 1) * sin_signed


def _to_rope_order(w):
    grp = lax.broadcasted_iota(jnp.int32, w.shape, 1) // ROPE_PAIRS
    from_right = pltpu.roll(w, HEAD_DIM - ROPE_PAIRS, 1)
    from_left = pltpu.roll(w, ROPE_PAIRS, 1)
    return jnp.where(grp == 1, from_right, jnp.where(grp == 2, from_left, w))


def _gain_in_rope_order(g_ref):
    return _to_rope_order(jnp.broadcast_to(g_ref[...], (8, HEAD_DIM)))[0:1]


def _rope_tile(row_ref, col_ref, i, tm):
    nrow = tm // GRID_W
    rows = row_ref[pl.ds(pl.multiple_of(i * nrow, nrow), nrow), :]
    col = col_ref[...]
    return jnp.concatenate([rows[r:r + 1, :] + col for r in range(nrow)], axis=0)


def _lagged_step(i, n_tiles, project, finish, buf_a, buf_b):
    bufs = (buf_a, buf_b)

    @pl.when(i == 0)
    def _():
        project(buf_a)

    for parity in range(2):
        @pl.when((i > 0) & (i < n_tiles) & (i % 2 == parity))
        def _():
            project(bufs[parity])
            finish(bufs[1 - parity])

    @pl.when(i == n_tiles)
    def _():
        finish(bufs[(n_tiles - 1) % 2])


def _lag_index_maps(n_tiles):
    return (lambda i: jnp.minimum(i, n_tiles - 1)), (lambda i: jnp.maximum(i - 1, 0))


def _cast_side_job(pairs):
    for src, dst in pairs:
        skip = src.shape[1] - dst.shape[1]
        dst[...] = src[:, skip:].astype(dst.dtype)


def _side_specs(arrays, nsteps, index, out_cols=None):
    in_specs, out_specs, out_shapes = [], [], []
    for k, a in enumerate(arrays):
        rows, cols = a.shape
        keep = cols if out_cols is None or out_cols[k] is None else out_cols[k]
        in_specs.append(pl.BlockSpec((rows // nsteps, cols), index))
        out_specs.append(pl.BlockSpec((rows // nsteps, keep), index))
        out_shapes.append(jax.ShapeDtypeStruct((rows, keep), BF16))
    return in_specs, out_specs, out_shapes


def _qkvproj_kernel(x_ref, gx_ref, wkv_ref, wq_ref, gk_ref, gq_ref, crow_ref, srow_ref, ccol_ref,
                    scol_ref, h_ref, kt_ref, v_ref, q_ref, wkvb_ref, wqb_ref,
                    kv_a, kv_b, q_a, q_b, *, tm, tq, n_tiles):
    i = pl.program_id(0)

    @pl.when(i == 0)
    def _():
        for hd in range(N_KV_HEADS):
            sl = slice(hd * HEAD_DIM, (hd + 1) * HEAD_DIM)
            wkvb_ref[:, sl] = _to_rope_order(wkv_ref[:, sl]).astype(BF16)
        wkvb_ref[:, ATTN_KV_W:] = wkv_ref[:, ATTN_KV_W:].astype(BF16)
        for hd in range(N_Q_HEADS):
            sl = slice(hd * HEAD_DIM, (hd + 1) * HEAD_DIM)
            wqb_ref[:, sl] = _to_rope_order(wq_ref[:, sl]).astype(BF16)

    def project(bufs):
        kv_acc, q_acc = bufs
        h = _head_rms(x_ref[...], gx_ref[...]).astype(BF16)
        h_ref[...] = h
        kv_acc[...] = _dot(h, wkvb_ref[...])
        q_acc[...] = _dot(h, wqb_ref[...])

    def finish(bufs):
        kv_acc, q_acc = bufs
        prev = jnp.maximum(i - 1, 0)
        cos = _rope_tile(crow_ref, ccol_ref, prev, tm)
        sin = _rope_tile(srow_ref, scol_ref, prev, tm)
        gk = _gain_in_rope_order(gk_ref)
        for hd in range(N_KV_HEADS):
            sl = slice(hd * HEAD_DIM, (hd + 1) * HEAD_DIM)
            y = _rope(_head_rms(kv_acc[:, sl], gk), cos, sin)
            for c in range(tm // KV_BLOCK):
                blk = y[c * KV_BLOCK:(c + 1) * KV_BLOCK, :]
                kt_ref[c, sl, :] = jnp.transpose(blk).astype(kt_ref.dtype)
        v_ref[...] = kv_acc[:, ATTN_KV_W:].astype(v_ref.dtype)
        gq = _gain_in_rope_order(gq_ref) * Q_SCALE
        for hd in range(N_Q_HEADS):
            grp, r = divmod(hd, Q_PER_KV)
            x = q_acc[:, hd * HEAD_DIM:(hd + 1) * HEAD_DIM]
            y = _rope(_head_rms(x, gq), cos, sin).astype(q_ref.dtype)
            for t in range(tm // tq):
                q_ref[t, grp, r * tq:(r + 1) * tq, :] = y[t * tq:(t + 1) * tq]

    _lagged_step(i, n_tiles, project, finish, (kv_a, q_a), (kv_b, q_b))


def _qkvproj(x, gx, w_in, gk, gq, tables, tm, tq):
    m, d = x.shape
    n_tiles = m // tm
    cur, prev = _lag_index_maps(n_tiles)
    fixed = lambda i: (0, 0)
    nkv = 2 * ATTN_KV_W
    tab_specs = [pl.BlockSpec(t.shape, fixed) for t in tables]
    qshape = (m // tq, N_KV_HEADS, Q_PER_KV * tq, HEAD_DIM)
    once = pl.Buffered(1)
    return pl.pallas_call(
        functools.partial(_qkvproj_kernel, tm=tm, tq=tq, n_tiles=n_tiles),
        out_shape=(jax.ShapeDtypeStruct((m, d), BF16),
                   jax.ShapeDtypeStruct((m // KV_BLOCK, ATTN_KV_W, KV_BLOCK), BF16),
                   jax.ShapeDtypeStruct((m, ATTN_KV_W), BF16),
                   jax.ShapeDtypeStruct(qshape, BF16)),
        grid=(n_tiles + 1,),
        in_specs=[pl.BlockSpec((tm, d), lambda i: (cur(i), 0)), pl.BlockSpec((1, d), fixed),
                  pl.BlockSpec((d, nkv), lambda i: (0, IN_KV_BLK), pipeline_mode=once),
                  pl.BlockSpec((d, ATTN_Q_W), lambda i: (0, IN_Q_BLK), pipeline_mode=once),
                  pl.BlockSpec((1, HEAD_DIM), fixed),
                  pl.BlockSpec((1, HEAD_DIM), fixed)] + tab_specs,
        out_specs=(pl.BlockSpec((tm, d), lambda i: (cur(i), 0)),
                   pl.BlockSpec((tm // KV_BLOCK, ATTN_KV_W, KV_BLOCK),
                                lambda i: (prev(i), 0, 0)),
                   pl.BlockSpec((tm, ATTN_KV_W), lambda i: (prev(i), 0)),
                   pl.BlockSpec((tm // tq,) + qshape[1:], lambda i: (prev(i), 0, 0, 0))),
        scratch_shapes=[pltpu.VMEM((d, nkv), BF16), pltpu.VMEM((d, ATTN_Q_W), BF16),
                        pltpu.VMEM((tm, nkv), F32), pltpu.VMEM((tm, nkv), F32),
                        pltpu.VMEM((tm, ATTN_Q_W), F32), pltpu.VMEM((tm, ATTN_Q_W), F32)],
        compiler_params=_params("arbitrary"),
        name="qkv_proj",
    )(x, gx.reshape(1, d), w_in, w_in, gk.reshape(1, HEAD_DIM), gq.reshape(1, HEAD_DIM),
      *tables)


def _memory_kv(mem_ref, gmem_ref, wkv_ref, gk_ref, kt_s, v_s):
    mem_n = _head_rms(mem_ref[...], gmem_ref[...]).astype(BF16)
    kv = _dot(mem_n, wkv_ref[...].astype(BF16))
    gk = gk_ref[...]
    for hd in range(N_MEM_HEADS):
        sl = slice(hd * HEAD_DIM, (hd + 1) * HEAD_DIM)
        kt_s[sl, :] = jnp.transpose(_head_rms(kv[:, sl], gk)).astype(kt_s.dtype)
    v_s[...] = kv[:, MEM_WIDTH:].astype(v_s.dtype)


def _branches_kernel(h_ref, wu_ref, wv_ref, wqm_ref, gs_ref, ws_ref, b_ref, gq_ref,
                     mem_ref, gmem_ref, wkv_ref, gk_ref, wgo_ref, wmo_ref,
                     gm_ref, ym_ref, wgob_ref, wmob_ref, wzb_ref, wqb_ref, kt_s, v_s, *, tm):
    @pl.when(pl.program_id(0) == 0)
    def _():
        wzb_ref[:, :GMLP_WIDTH] = wu_ref[...].astype(BF16)
        wzb_ref[:, GMLP_WIDTH:] = wv_ref[...].astype(BF16)
        wqb_ref[...] = wqm_ref[...].astype(BF16)
        _memory_kv(mem_ref, gmem_ref, wkv_ref, gk_ref, kt_s, v_s)

    _cast_side_job([(wgo_ref, wgob_ref), (wmo_ref, wmob_ref)])
    h = h_ref[...]
    z = jax.nn.gelu(_dot(h, wzb_ref[...]))
    acc = _dot(h, wqb_ref[...])

    u = z[:, :GMLP_WIDTH]
    vn = _head_rms(z[:, GMLP_WIDTH:], gs_ref[...]).astype(BF16)
    bias = b_ref[...]
    for c in range(tm // CHUNK):
        rows = slice(c * CHUNK, (c + 1) * CHUNK)
        for grp in range(GMLP_GROUPS):
            cols = slice(grp * HEAD_DIM, (grp + 1) * HEAD_DIM)
            mixed = _dot(ws_ref[grp].astype(BF16), vn[rows, cols]) + bias[:, cols]
            gm_ref[rows, cols] = (u[rows, cols] * mixed).astype(gm_ref.dtype)

    g = gq_ref[...] * Q_SCALE
    for hd in range(N_MEM_HEADS):
        sl = slice(hd * HEAD_DIM, (hd + 1) * HEAD_DIM)
        qn = _head_rms(acc[:, sl], g).astype(BF16)
        s = _dot(qn, kt_s[sl, :])
        p = jnp.exp2(s - jnp.max(s, axis=-1, keepdims=True))
        l = jnp.sum(p, axis=-1, keepdims=True)
        o = _dot(p.astype(BF16), v_s[:, sl])
        ym_ref[:, sl] = (o / l).astype(ym_ref.dtype)


def _branches(h, w_in, sgu_gain, w_spatial, bias_full, mq_gain, mem, mem_gain, w_mem_kv, mk_gain,
              w_go, w_mo, tm):
    m, d = h.shape
    mt = mem.shape[0]
    row = lambda i: (i, 0)
    fixed = lambda i: (0, 0)
    once = pl.Buffered(1)
    side_in, side_out, side_shapes = _side_specs([w_go, w_mo], m // tm, row)
    return pl.pallas_call(
        functools.partial(_branches_kernel, tm=tm),
        out_shape=[jax.ShapeDtypeStruct((m, GMLP_WIDTH), BF16),
                   jax.ShapeDtypeStruct((m, MEM_WIDTH), BF16)] + side_shapes,
        grid=(m // tm,),
        in_specs=[pl.BlockSpec((tm, d), row),
                  pl.BlockSpec((d, GMLP_WIDTH), lambda i: (0, IN_U_BLK), pipeline_mode=once),
                  pl.BlockSpec((d, GMLP_WIDTH), lambda i: (0, IN_V_BLK), pipeline_mode=once),
                  pl.BlockSpec((d, MEM_WIDTH), lambda i: (0, IN_QM_BLK), pipeline_mode=once),
                  pl.BlockSpec((1, GMLP_WIDTH), fixed),
                  pl.BlockSpec((GMLP_GROUPS, CHUNK, CHUNK), lambda i: (0, 0, 0)),
                  pl.BlockSpec((CHUNK, GMLP_WIDTH), fixed),
                  pl.BlockSpec((1, HEAD_DIM), fixed),
                  pl.BlockSpec((mt, d), fixed, pipeline_mode=once),
                  pl.BlockSpec((1, d), fixed),
                  pl.BlockSpec((d, 2 * MEM_WIDTH), fixed, pipeline_mode=once),
                  pl.BlockSpec((1, HEAD_DIM), fixed)] + side_in,
        out_specs=[pl.BlockSpec((tm, GMLP_WIDTH), row),
                   pl.BlockSpec((tm, MEM_WIDTH), row)] + side_out,
        scratch_shapes=[pltpu.VMEM((d, 2 * GMLP_WIDTH), BF16), pltpu.VMEM((d, MEM_WIDTH), BF16),
                        pltpu.VMEM((MEM_WIDTH, mt), BF16), pltpu.VMEM((mt, MEM_WIDTH), BF16)],
        compiler_params=_params("arbitrary"),
        name="branches",
    )(h, w_in, w_in, w_in, sgu_gain.reshape(1, GMLP_WIDTH), w_spatial, bias_full,
      mq_gain.reshape(1, HEAD_DIM), mem, mem_gain.reshape(1, d), w_mem_kv,
      mk_gain.reshape(1, HEAD_DIM), w_go, w_mo)


def _store_heads(o_ref, out, tq):
    for r in range(Q_PER_KV):
        o_ref[:, r * HEAD_DIM:(r + 1) * HEAD_DIM] = out[r * tq:(r + 1) * tq].astype(o_ref.dtype)


def _attn_unshifted(q_ref, kt_ref, v_ref, o_ref, l_ref, acc_ref, tq, nk):
    l_ref[...] = jnp.zeros_like(l_ref)
    acc_ref[...] = jnp.zeros_like(acc_ref)

    def body(j, carry):
        off = pl.multiple_of(j * KV_BLOCK, KV_BLOCK)
        p = jnp.exp2(_dot(q_ref[0, 0], kt_ref[j]))
        l_ref[...] += p[:, :HEAD_DIM] + p[:, HEAD_DIM:]
        acc_ref[...] += _dot(p.astype(BF16), v_ref[pl.ds(off, KV_BLOCK), :])
        return carry

    lax.fori_loop(0, nk, body, 0, unroll=True)
    l = jnp.sum(l_ref[...], axis=-1, keepdims=True)
    _store_heads(o_ref, acc_ref[...] / l, tq)


def _attn_online(q_ref, kt_ref, v_ref, o_ref, tq, nk):
    q = q_ref[0, 0]
    rows = Q_PER_KV * tq

    def body(j, carry):
        m, l, acc = carry
        off = pl.multiple_of(j * KV_BLOCK, KV_BLOCK)
        s = _dot(q, kt_ref[j])
        m_new = jnp.maximum(m, jnp.max(s, axis=-1, keepdims=True))
        alpha = jnp.exp2(m - m_new)
        p = jnp.exp2(s - m_new)
        l = alpha * l + jnp.sum(p, axis=-1, keepdims=True)
        acc = alpha * acc + _dot(p.astype(BF16), v_ref[pl.ds(off, KV_BLOCK), :])
        return m_new, l, acc

    m0 = jnp.full((rows, 1), -jnp.inf, F32)
    l0 = jnp.zeros((rows, 1), F32)
    acc0 = jnp.zeros((rows, HEAD_DIM), F32)
    _, l, acc = lax.fori_loop(0, nk, body, (m0, l0, acc0))
    _store_heads(o_ref, acc / l, tq)


def _attn_kernel(unshifted_ref, q_ref, kt_ref, v_ref, *rest, tq, nk, nside):
    side = rest[:nside]
    o_ref = rest[nside]
    side_out = rest[nside + 1:2 * nside + 1]
    l_ref, acc_ref = rest[2 * nside + 1:]
    _cast_side_job(zip(side, side_out))
    unshifted = unshifted_ref[0] != 0

    @pl.when(unshifted)
    def _():
        _attn_unshifted(q_ref, kt_ref, v_ref, o_ref, l_ref, acc_ref, tq, nk)

    @pl.when(jnp.logical_not(unshifted))
    def _():
        _attn_online(q_ref, kt_ref, v_ref, o_ref, tq, nk)


def _attention(unshifted, q, kt, v, side_weights, side_cols):
    nq, _, rows, _ = q.shape
    tq = rows // Q_PER_KV
    s = v.shape[0]
    nk = s // KV_BLOCK
    gw = Q_PER_KV * HEAD_DIM
    nside = len(side_weights)
    side_in, side_out, side_shapes = _side_specs(side_weights, N_KV_HEADS * nq,
                                                 lambda g, i: (g * nq + i, 0), side_cols)
    return pl.pallas_call(
        functools.partial(_attn_kernel, tq=tq, nk=nk, nside=nside),
        out_shape=[jax.ShapeDtypeStruct((s, ATTN_Q_W), BF16)] + side_shapes,
        grid=(N_KV_HEADS, nq),
        in_specs=[pl.BlockSpec(memory_space=pltpu.SMEM),
                  pl.BlockSpec((1, 1, rows, HEAD_DIM), lambda g, i: (i, g, 0, 0)),
                  pl.BlockSpec((nk, HEAD_DIM, KV_BLOCK), lambda g, i: (0, g, 0)),
                  pl.BlockSpec((s, HEAD_DIM), lambda g, i: (0, g))] + side_in,
        out_specs=[pl.BlockSpec((tq, gw), lambda g, i: (i, g))] + side_out,
        scratch_shapes=[pltpu.VMEM((rows, HEAD_DIM), F32), pltpu.VMEM((rows, HEAD_DIM), F32)],
        compiler_params=_params("arbitrary", "arbitrary"),
        name="flash_attn",
    )(unshifted, q, kt, v, *side_weights)


def _gated_merge_kernel(h_ref, a_ref, gm_ref, ym_ref, wga_ref, wgg_ref, wgm_ref,
                        wa_ref, wg_ref, wm_ref, o_ref):
    h = h_ref[...]
    y = None
    for y_ref, wgate_ref, w_ref in ((a_ref, wga_ref, wa_ref), (gm_ref, wgg_ref, wg_ref),
                                    (ym_ref, wgm_ref, wm_ref)):
        gate = 1.0 / (1.0 + jnp.exp(-_dot(h, wgate_ref[...])))
        term = gate * _dot(y_ref[...], w_ref[...])
        y = term if y is None else y + term
    o_ref[...] = y.astype(o_ref.dtype)


def _gated_merge(h, attn, gm, ym, w_gate, w_a, w_g, w_m, tm, tn):
    m, d = h.shape
    n = w_a.shape[1]
    nb = n // tn
    row = lambda j, i: (i, 0)
    col = lambda j, i: (0, j)
    return pl.pallas_call(
        _gated_merge_kernel,
        out_shape=jax.ShapeDtypeStruct((m, n), BF16),
        grid=(nb, m // tm),
        in_specs=[pl.BlockSpec((tm, d), row),
                  pl.BlockSpec((tm, attn.shape[1]), row),
                  pl.BlockSpec((tm, gm.shape[1]), row),
                  pl.BlockSpec((tm, ym.shape[1]), row),
                  pl.BlockSpec((d, tn), lambda j, i: (0, j)),
                  pl.BlockSpec((d, tn), lambda j, i: (0, nb + j)),
                  pl.BlockSpec((d, tn), lambda j, i: (0, 2 * nb + j)),
                  pl.BlockSpec((w_a.shape[0], tn), col),
                  pl.BlockSpec((w_g.shape[0], tn), col),
                  pl.BlockSpec((w_m.shape[0], tn), col)],
        out_specs=pl.BlockSpec((tm, tn), lambda j, i: (i, j)),
        compiler_params=_params("arbitrary", "arbitrary"),
        name="gated_merge",
    )(h, attn, gm, ym, w_gate, w_gate, w_gate, w_a, w_g, w_m)


def _outproj_kernel(mg_ref, w_ref, x_ref, g_ref, x1_ref, h2_ref):
    x1 = x_ref[...] + _dot(mg_ref[...], w_ref[...])
    x1_ref[...] = x1
    h2_ref[...] = _head_rms(x1, g_ref[...]).astype(h2_ref.dtype)


def _outproj(merged, w, x, gain, tm):
    m, d = x.shape
    row = lambda i: (i, 0)
    fixed = lambda i: (0, 0)
    return pl.pallas_call(
        _outproj_kernel,
        out_shape=(jax.ShapeDtypeStruct((m, d), F32), jax.ShapeDtypeStruct((m, d), BF16)),
        grid=(m // tm,),
        in_specs=[pl.BlockSpec((tm, d), row), pl.BlockSpec((d, d), fixed),
                  pl.BlockSpec((tm, d), row), pl.BlockSpec((1, d), fixed)],
        out_specs=(pl.BlockSpec((tm, d), row), pl.BlockSpec((tm, d), row)),
        compiler_params=_params("parallel"),
        name="out_proj",
    )(merged, w, x, gain.reshape(1, d))


def _ffn_up_kernel(h_ref, w_ref, o_ref):
    z = jnp.maximum(_dot(h_ref[...], w_ref[...]), 0.0)
    o_ref[...] = (z * z).astype(o_ref.dtype)


def _ffn_up(h, w, tm, tn):
    m, d = h.shape
    n = w.shape[1]
    return pl.pallas_call(
        _ffn_up_kernel,
        out_shape=jax.ShapeDtypeStruct((m, n), BF16),
        grid=(m // tm, n // tn),
        in_specs=[pl.BlockSpec((tm, d), lambda i, j: (i, 0)),
                  pl.BlockSpec((d, tn), lambda i, j: (0, j))],
        out_specs=pl.BlockSpec((tm, tn), lambda i, j: (i, j)),
        compiler_params=_params("parallel", "parallel"),
        name="ffn_up",
    )(h, w)


def _ffn_down_kernel(a_ref, w_ref, x_ref, o_ref):
    k = pl.program_id(2)

    @pl.when(k == 0)
    def _():
        o_ref[...] = x_ref[...] + _dot(a_ref[...], w_ref[...])

    @pl.when(k != 0)
    def _():
        o_ref[...] += _dot(a_ref[...], w_ref[...])


def _ffn_down(a, w, x1, tm, tn, tk):
    m, kdim = a.shape
    n = w.shape[1]
    return pl.pallas_call(
        _ffn_down_kernel,
        out_shape=jax.ShapeDtypeStruct((m, n), F32),
        grid=(m // tm, n // tn, kdim // tk),
        in_specs=[pl.BlockSpec((tm, tk), lambda i, j, k: (i, k)),
                  pl.BlockSpec((tk, tn), lambda i, j, k: (k, j)),
                  pl.BlockSpec((tm, tn), lambda i, j, k: (i, j))],
        out_specs=pl.BlockSpec((tm, tn), lambda i, j, k: (i, j)),
        compiler_params=_params("parallel", "parallel", "arbitrary"),
        name="ffn_down",
    )(a, w, x1)


def _rope_tables(s):
    inv = ROPE_THETA ** (-jnp.arange(ROPE_PAIRS, dtype=F32) / ROPE_PAIRS)
    inv_lane = jnp.tile(inv, HEAD_DIM // ROPE_PAIRS)
    lane_grp = jnp.arange(HEAD_DIM) // ROPE_PAIRS
    is_row = (lane_grp % 2) == 0
    sign = jnp.where(lane_grp < 2, -1.0, 1.0).astype(F32)

    def tables(n):
        ang = jnp.arange(n, dtype=F32)[:, None] * inv_lane
        return jnp.cos(ang), jnp.sin(ang) * sign

    cr, sr = tables(s // GRID_W)
    cc, sc = tables(GRID_W)
    return (jnp.where(is_row, cr, 0.0), jnp.where(is_row, sr, 0.0),
            jnp.where(is_row, 0.0, cc), jnp.where(is_row, 0.0, sc))


def kernel(x, mem, norm_mix, w_in, q_norm, k_norm, sgu_norm, w_spatial, b_spatial, mem_norm,
           w_mem_kv, mq_norm, mk_norm, w_attn_o, w_gmlp_o, w_mem_o, w_out, norm_ffn,
           w_ffn_up, w_ffn_down):
    b, s, d = x.shape
    assert b == 1 and d == D_MODEL and norm_mix.shape[0] == 1
    xs = x[0]
    w_in0 = w_in[0]
    tables = _rope_tables(s)
    bias_full = jnp.repeat(b_spatial[0].T, HEAD_DIM, axis=1)

    h, kt, v, q = _qkvproj(xs, norm_mix[0], w_in0, k_norm[0], q_norm[0], tables, TM_PROJ,
                           ATTN_TQ)
    gm, ym, w_gmlp_o_b, w_mem_o_b = _branches(
        h, w_in0, sgu_norm[0], w_spatial[0], bias_full, mq_norm[0], mem[0], mem_norm[0],
        w_mem_kv[0], mk_norm[0], w_gmlp_o[0], w_mem_o[0], TM_PROJ)

    logit_bound = (1.02 * LOG2E * math.sqrt(HEAD_DIM)
                   * jnp.max(jnp.abs(q_norm[0])) * jnp.max(jnp.abs(k_norm[0])))
    n_gate = w_in0.shape[1] - IN_GATE_BLK * IN_COL_BLOCK
    unshifted = (logit_bound <= MAX_UNSHIFTED_LOGIT).astype(jnp.int32).reshape(1)
    attn, w_gate_b, w_attn_o_b, w_out_b, w_up_b, w_down_b = _attention(
        unshifted, q, kt, v, (w_in0, w_attn_o[0], w_out[0], w_ffn_up[0], w_ffn_down[0]),
        (n_gate, None, None, None, None))

    merged = _gated_merge(h, attn, gm, ym, w_gate_b, w_attn_o_b, w_gmlp_o_b, w_mem_o_b,
                          TM_MERGE, TN_MERGE)
    x1, h2 = _outproj(merged, w_out_b, xs, norm_ffn[0], TM_OUT)
    a = _ffn_up(h2, w_up_b, TM_UP, TN_UP)
    out = _ffn_down(a, w_down_b, x1, TM_DOWN, TN_DOWN, TK_DOWN)
    return out[None]
```

```python
import functools
import math

import jax
import jax.numpy as jnp
from jax import lax
from jax.experimental import pallas as pl
from jax.experimental.pallas import tpu as pltpu

D_MODEL = 2048
HEAD_DIM = 128
N_Q_HEADS = 8
N_KV_HEADS = 2
Q_PER_KV = N_Q_HEADS // N_KV_HEADS
GRID_W = 64
ROPE_THETA = 10000.0
ROPE_PAIRS = HEAD_DIM // 4
GMLP_GROUPS = 4
GMLP_WIDTH = GMLP_GROUPS * HEAD_DIM
CHUNK = 128
N_MEM_HEADS = 4
MEM_WIDTH = N_MEM_HEADS * HEAD_DIM
EPS = 1e-6
ATTN_Q_W = N_Q_HEADS * HEAD_DIM
ATTN_KV_W = N_KV_HEADS * HEAD_DIM
IN_COL_BLOCK = 512
IN_Q_BLK, IN_KV_BLK, IN_U_BLK, IN_V_BLK, IN_QM_BLK, IN_GATE_BLK = 0, 2, 3, 4, 5, 6

VMEM_LIMIT_BYTES = 56 * 1024 * 1024
LOG2E = math.log2(math.e)
Q_SCALE = HEAD_DIM ** -0.5 * LOG2E
KV_BLOCK = 256
ATTN_TQ = 512
MAX_UNSHIFTED_LOGIT = 64.0

TM_PROJ = 512
TM_MERGE, TN_MERGE = 1024, 512
TM_OUT = 512
TM_UP, TN_UP = 2048, 1024
TM_DOWN, TN_DOWN, TK_DOWN = 1024, 1024, 4096

F32 = jnp.float32
BF16 = jnp.bfloat16


def _params(*sem):
    return pltpu.CompilerParams(dimension_semantics=sem, vmem_limit_bytes=VMEM_LIMIT_BYTES)


def _dot(a, b):
    return jnp.dot(a, b, preferred_element_type=F32)


def _head_rms(x, gain):
    ms = jnp.mean(x * x, axis=-1, keepdims=True)
    return x * lax.rsqrt(ms + EPS) * gain


def _rope(y, cos, sin_signed):
    return y * cos + pltpu.roll(y, HEAD_DIM // 2, 1) * sin_signed


def _to_rope_order(w):
    grp = lax.broadcasted_iota(jnp.int32, w.shape, 1) // ROPE_PAIRS
    from_right = pltpu.roll(w, HEAD_DIM - ROPE_PAIRS, 1)
    from_left = pltpu.roll(w, ROPE_PAIRS, 1)
    return jnp.where(grp == 1, from_right, jnp.where(grp == 2, from_left, w))


def _gain_in_rope_order(g_ref):
    return _to_rope_order(jnp.broadcast_to(g_ref[...], (8, HEAD_DIM)))[0:1]


def _rope_tile(row_ref, col_ref, i, tm):
    nrow = tm // GRID_W
    rows = row_ref[pl.ds(pl.multiple_of(i * nrow, nrow), nrow), :]
    col = col_ref[...]
    return jnp.concatenate([rows[r:r + 1, :] + col for r in range(nrow)], axis=0)


def _lagged_step(i, n_tiles, project, finish, buf_a, buf_b):
    bufs = (buf_a, buf_b)

    @pl.when(i == 0)
    def _():
        project(buf_a)

    for parity in range(2):
        @pl.when((i > 0) & (i < n_tiles) & (i % 2 == parity))
        def _():
            project(bufs[parity])
            finish(bufs[1 - parity])

    @pl.when(i == n_tiles)
    def _():
        finish(bufs[(n_tiles - 1) % 2])


def _lag_index_maps(n_tiles):
    return (lambda i: jnp.minimum(i, n_tiles - 1)), (lambda i: jnp.maximum(i - 1, 0))


def _cast_side_job(pairs):
    for src, dst in pairs:
        skip = src.shape[1] - dst.shape[1]
        dst[...] = src[:, skip:].astype(dst.dtype)


def _side_specs(arrays, nsteps, index, out_cols=None):
    in_specs, out_specs, out_shapes = [], [], []
    for k, a in enumerate(arrays):
        rows, cols = a.shape
        keep = cols if out_cols is None or out_cols[k] is None else out_cols[k]
        in_specs.append(pl.BlockSpec((rows // nsteps, cols), index))
        out_specs.append(pl.BlockSpec((rows // nsteps, keep), index))
        out_shapes.append(jax.ShapeDtypeStruct((rows, keep), BF16))
    return in_specs, out_specs, out_shapes


def _qkvproj_kernel(x_ref, gx_ref, wkv_ref, wq_ref, gk_ref, gq_ref, crow_ref, srow_ref, ccol_ref,
                    scol_ref, h_ref, kt_ref, v_ref, q_ref, wkvb_ref, wqb_ref,
                    kv_a, kv_b, q_a, q_b, *, tm, tq, n_tiles):
    i = pl.program_id(0)

    @pl.when(i == 0)
    def _():
        for hd in range(N_KV_HEADS):
            sl = slice(hd * HEAD_DIM, (hd + 1) * HEAD_DIM)
            wkvb_ref[:, sl] = _to_rope_order(wkv_ref[:, sl]).astype(BF16)
        wkvb_ref[:, ATTN_KV_W:] = wkv_ref[:, ATTN_KV_W:].astype(BF16)
        for hd in range(N_Q_HEADS):
            sl = slice(hd * HEAD_DIM, (hd + 1) * HEAD_DIM)
            wqb_ref[:, sl] = _to_rope_order(wq_ref[:, sl]).astype(BF16)

    def project(bufs):
        kv_acc, q_acc = bufs
        h = _head_rms(x_ref[...], gx_ref[...]).astype(BF16)
        h_ref[...] = h
        kv_acc[...] = _dot(h, wkvb_ref[...])
        q_acc[...] = _dot(h, wqb_ref[...])

    def finish(bufs):
        kv_acc, q_acc = bufs
        prev = jnp.maximum(i - 1, 0)
        cos = _rope_tile(crow_ref, ccol_ref, prev, tm)
        sin = _rope_tile(srow_ref, scol_ref, prev, tm)
        gk = _gain_in_rope_order(gk_ref)
        for hd in range(N_KV_HEADS):
            sl = slice(hd * HEAD_DIM, (hd + 1) * HEAD_DIM)
            y = _rope(_head_rms(kv_acc[:, sl], gk), cos, sin)
            for c in range(tm // KV_BLOCK):
                blk = y[c * KV_BLOCK:(c + 1) * KV_BLOCK, :]
                kt_ref[c, sl, :] = jnp.transpose(blk).astype(kt_ref.dtype)
        v_ref[...] = kv_acc[:, ATTN_KV_W:].astype(v_ref.dtype)
        gq = _gain_in_rope_order(gq_ref) * Q_SCALE
        for hd in range(N_Q_HEADS):
            grp, r = divmod(hd, Q_PER_KV)
            x = q_acc[:, hd * HEAD_DIM:(hd + 1) * HEAD_DIM]
            y = _rope(_head_rms(x, gq), cos, sin).astype(q_ref.dtype)
            for t in range(tm // tq):
                q_ref[t, grp, r * tq:(r + 1) * tq, :] = y[t * tq:(t + 1) * tq]

    _lagged_step(i, n_tiles, project, finish, (kv_a, q_a), (kv_b, q_b))


def _qkvproj(x, gx, w_in, gk, gq, tables, tm, tq):
    m, d = x.shape
    n_tiles = m // tm
    cur, prev = _lag_index_maps(n_tiles)
    fixed = lambda i: (0, 0)
    nkv = 2 * ATTN_KV_W
    tab_specs = [pl.BlockSpec(t.shape, fixed) for t in tables]
    qshape = (m // tq, N_KV_HEADS, Q_PER_KV * tq, HEAD_DIM)
    once = pl.Buffered(1)
    return pl.pallas_call(
        functools.partial(_qkvproj_kernel, tm=tm, tq=tq, n_tiles=n_tiles),
        out_shape=(jax.ShapeDtypeStruct((m, d), BF16),
                   jax.ShapeDtypeStruct((m // KV_BLOCK, ATTN_KV_W, KV_BLOCK), BF16),
                   jax.ShapeDtypeStruct((m, ATTN_KV_W), BF16),
                   jax.ShapeDtypeStruct(qshape, BF16)),
        grid=(n_tiles + 1,),
        in_specs=[pl.BlockSpec((tm, d), lambda i: (cur(i), 0)), pl.BlockSpec((1, d), fixed),
                  pl.BlockSpec((d, nkv), lambda i: (0, IN_KV_BLK), pipeline_mode=once),
                  pl.BlockSpec((d, ATTN_Q_W), lambda i: (0, IN_Q_BLK), pipeline_mode=once),
                  pl.BlockSpec((1, HEAD_DIM), fixed),
                  pl.BlockSpec((1, HEAD_DIM), fixed)] + tab_specs,
        out_specs=(pl.BlockSpec((tm, d), lambda i: (cur(i), 0)),
                   pl.BlockSpec((tm // KV_BLOCK, ATTN_KV_W, KV_BLOCK),
                                lambda i: (prev(i), 0, 0)),
                   pl.BlockSpec((tm, ATTN_KV_W), lambda i: (prev(i), 0)),
                   pl.BlockSpec((tm // tq,) + qshape[1:], lambda i: (prev(i), 0, 0, 0))),
        scratch_shapes=[pltpu.VMEM((d, nkv), BF16), pltpu.VMEM((d, ATTN_Q_W), BF16),
                        pltpu.VMEM((tm, nkv), F32), pltpu.VMEM((tm, nkv), F32),
                        pltpu.VMEM((tm, ATTN_Q_W), F32), pltpu.VMEM((tm, ATTN_Q_W), F32)],
        compiler_params=_params("arbitrary"),
        name="qkv_proj",
    )(x, gx.reshape(1, d), w_in, w_in, gk.reshape(1, HEAD_DIM), gq.reshape(1, HEAD_DIM),
      *tables)


def _memory_kv(mem_ref, gmem_ref, wkv_ref, gk_ref, kt_s, v_s):
    mem_n = _head_rms(mem_ref[...], gmem_ref[...]).astype(BF16)
    kv = _dot(mem_n, wkv_ref[...].astype(BF16))
    gk = gk_ref[...]
    for hd in range(N_MEM_HEADS):
        sl = slice(hd * HEAD_DIM, (hd + 1) * HEAD_DIM)
        kt_s[sl, :] = jnp.transpose(_head_rms(kv[:, sl], gk)).astype(kt_s.dtype)
    v_s[...] = kv[:, MEM_WIDTH:].astype(v_s.dtype)


def _branches_kernel(h_ref, wu_ref, wv_ref, wqm_ref, gs_ref, ws_ref, b_ref, gq_ref,
                     mem_ref, gmem_ref, wkv_ref, gk_ref, wgo_ref, wmo_ref,
                     gm_ref, ym_ref, wgob_ref, wmob_ref, wzb_ref, wqb_ref, kt_s, v_s, *, tm):
    @pl.when(pl.program_id(0) == 0)
    def _():
        wzb_ref[:, :GMLP_WIDTH] = wu_ref[...].astype(BF16)
        wzb_ref[:, GMLP_WIDTH:] = wv_ref[...].astype(BF16)
        wqb_ref[...] = wqm_ref[...].astype(BF16)
        _memory_kv(mem_ref, gmem_ref, wkv_ref, gk_ref, kt_s, v_s)

    _cast_side_job([(wgo_ref, wgob_ref), (wmo_ref, wmob_ref)])
    h = h_ref[...]
    z = jax.nn.gelu(_dot(h, wzb_ref[...]))
    acc = _dot(h, wqb_ref[...])

    u = z[:, :GMLP_WIDTH]
    vn = _head_rms(z[:, GMLP_WIDTH:], gs_ref[...]).astype(BF16)
    bias = b_ref[...]
    for c in range(tm // CHUNK):
        rows = slice(c * CHUNK, (c + 1) * CHUNK)
        for grp in range(GMLP_GROUPS):
            cols = slice(grp * HEAD_DIM, (grp + 1) * HEAD_DIM)
            mixed = _dot(ws_ref[grp].astype(BF16), vn[rows, cols]) + bias[:, cols]
            gm_ref[rows, cols] = (u[rows, cols] * mixed).astype(gm_ref.dtype)

    g = gq_ref[...] * Q_SCALE
    for hd in range(N_MEM_HEADS):
        sl = slice(hd * HEAD_DIM, (hd + 1) * HEAD_DIM)
        qn = _head_rms(acc[:, sl], g).astype(BF16)
        s = _dot(qn, kt_s[sl, :])
        p = jnp.exp2(s - jnp.max(s, axis=-1, keepdims=True))
        l = jnp.sum(p, axis=-1, keepdims=True)
        o = _dot(p.astype(BF16), v_s[:, sl])
        ym_ref[:, sl] = (o / l).astype(ym_ref.dtype)


def _branches(h, w_in, sgu_gain, w_spatial, bias_full, mq_gain, mem, mem_gain, w_mem_kv, mk_gain,
              w_go, w_mo, tm):
    m, d = h.shape
    mt = mem.shape[0]
    row = lambda i: (i, 0)
    fixed = lambda i: (0, 0)
    once = pl.Buffered(1)
    side_in, side_out, side_shapes = _side_specs([w_go, w_mo], m // tm, row)
    return pl.pallas_call(
        functools.partial(_branches_kernel, tm=tm),
        out_shape=[jax.ShapeDtypeStruct((m, GMLP_WIDTH), BF16),
                   jax.ShapeDtypeStruct((m, MEM_WIDTH), BF16)] + side_shapes,
        grid=(m // tm,),
        in_specs=[pl.BlockSpec((tm, d), row),
                  pl.BlockSpec((d, GMLP_WIDTH), lambda i: (0, IN_U_BLK), pipeline_mode=once),
                  pl.BlockSpec((d, GMLP_WIDTH), lambda i: (0, IN_V_BLK), pipeline_mode=once),
                  pl.BlockSpec((d, MEM_WIDTH), lambda i: (0, IN_QM_BLK), pipeline_mode=once),
                  pl.BlockSpec((1, GMLP_WIDTH), fixed),
                  pl.BlockSpec((GMLP_GROUPS, CHUNK, CHUNK), lambda i: (0, 0, 0)),
                  pl.BlockSpec((CHUNK, GMLP_WIDTH), fixed),
                  pl.BlockSpec((1, HEAD_DIM), fixed),
                  pl.BlockSpec((mt, d), fixed, pipeline_mode=once),
                  pl.BlockSpec((1, d), fixed),
                  pl.BlockSpec((d, 2 * MEM_WIDTH), fixed, pipeline_mode=once),
                  pl.BlockSpec((1, HEAD_DIM), fixed)] + side_in,
        out_specs=[pl.BlockSpec((tm, GMLP_WIDTH), row),
                   pl.BlockSpec((tm, MEM_WIDTH), row)] + side_out,
        scratch_shapes=[pltpu.VMEM((d, 2 * GMLP_WIDTH), BF16), pltpu.VMEM((d, MEM_WIDTH), BF16),
                        pltpu.VMEM((MEM_WIDTH, mt), BF16), pltpu.VMEM((mt, MEM_WIDTH), BF16)],
        compiler_params=_params("arbitrary"),
        name="branches",
    )(h, w_in, w_in, w_in, sgu_gain.reshape(1, GMLP_WIDTH), w_spatial, bias_full,
      mq_gain.reshape(1, HEAD_DIM), mem, mem_gain.reshape(1, d), w_mem_kv,
      mk_gain.reshape(1, HEAD_DIM), w_go, w_mo)


def _store_heads(o_ref, out, tq):
    for r in range(Q_PER_KV):
        o_ref[:, r * HEAD_DIM:(r + 1) * HEAD_DIM] = out[r * tq:(r + 1) * tq].astype(o_ref.dtype)


def _attn_unshifted(q_ref, kt_ref, v_ref, o_ref, l_ref, acc_ref, tq, nk):
    l_ref[...] = jnp.zeros_like(l_ref)
    acc_ref[...] = jnp.zeros_like(acc_ref)

    def body(j, carry):
        off = pl.multiple_of(j * KV_BLOCK, KV_BLOCK)
        p = jnp.exp2(_dot(q_ref[0, 0], kt_ref[j]))
        l_ref[...] += p[:, :HEAD_DIM] + p[:, HEAD_DIM:]
        acc_ref[...] += _dot(p.astype(BF16), v_ref[pl.ds(off, KV_BLOCK), :])
        return carry

    lax.fori_loop(0, nk, body, 0, unroll=True)
    l = jnp.sum(l_ref[...], axis=-1, keepdims=True)
    _store_heads(o_ref, acc_ref[...] / l, tq)


def _attn_online(q_ref, kt_ref, v_ref, o_ref, tq, nk):
    q = q_ref[0, 0]
    rows = Q_PER_KV * tq

    def body(j, carry):
        m, l, acc = carry
        off = pl.multiple_of(j * KV_BLOCK, KV_BLOCK)
        s = _dot(q, kt_ref[j])
        m_new = jnp.maximum(m, jnp.max(s, axis=-1, keepdims=True))
        alpha = jnp.exp2(m - m_new)
        p = jnp.exp2(s - m_new)
        l = alpha * l + jnp.sum(p, axis=-1, keepdims=True)
        acc = alpha * acc + _dot(p.astype(BF16), v_ref[pl.ds(off, KV_BLOCK), :])
        return m_new, l, acc

    m0 = jnp.full((rows, 1), -jnp.inf, F32)
    l0 = jnp.zeros((rows, 1), F32)
    acc0 = jnp.zeros((rows, HEAD_DIM), F32)
    _, l, acc = lax.fori_loop(0, nk, body, (m0, l0, acc0))
    _store_heads(o_ref, acc / l, tq)


def _attn_kernel(gq_ref, gk_ref, q_ref, kt_ref, v_ref, *rest, tq, nk, nside):
    side = rest[:nside]
    o_ref = rest[nside]
    side_out = rest[nside + 1:2 * nside + 1]
    l_ref, acc_ref = rest[2 * nside + 1:]
    _cast_side_job(zip(side, side_out))
    logit_bound = (1.02 * LOG2E * math.sqrt(HEAD_DIM)
                   * jnp.max(jnp.abs(gq_ref[...])) * jnp.max(jnp.abs(gk_ref[...])))
    unshifted = logit_bound <= MAX_UNSHIFTED_LOGIT

    @pl.when(unshifted)
    def _():
        _attn_unshifted(q_ref, kt_ref, v_ref, o_ref, l_ref, acc_ref, tq, nk)

    @pl.when(jnp.logical_not(unshifted))
    def _():
        _attn_online(q_ref, kt_ref, v_ref, o_ref, tq, nk)


def _attention(gq, gk, q, kt, v, side_weights, side_cols):
    nq, _, rows, _ = q.shape
    tq = rows // Q_PER_KV
    s = v.shape[0]
    nk = s // KV_BLOCK
    gw = Q_PER_KV * HEAD_DIM
    nside = len(side_weights)
    side_in, side_out, side_shapes = _side_specs(side_weights, N_KV_HEADS * nq,
                                                 lambda g, i: (g * nq + i, 0), side_cols)
    return pl.pallas_call(
        functools.partial(_attn_kernel, tq=tq, nk=nk, nside=nside),
        out_shape=[jax.ShapeDtypeStruct((s, ATTN_Q_W), BF16)] + side_shapes,
        grid=(N_KV_HEADS, nq),
        in_specs=[pl.BlockSpec((1, HEAD_DIM), lambda g, i: (0, 0)),
                  pl.BlockSpec((1, HEAD_DIM), lambda g, i: (0, 0)),
                  pl.BlockSpec((1, 1, rows, HEAD_DIM), lambda g, i: (i, g, 0, 0)),
                  pl.BlockSpec((nk, HEAD_DIM, KV_BLOCK), lambda g, i: (0, g, 0)),
                  pl.BlockSpec((s, HEAD_DIM), lambda g, i: (0, g))] + side_in,
        out_specs=[pl.BlockSpec((tq, gw), lambda g, i: (i, g))] + side_out,
        scratch_shapes=[pltpu.VMEM((rows, HEAD_DIM), F32), pltpu.VMEM((rows, HEAD_DIM), F32)],
        compiler_params=_params("arbitrary", "arbitrary"),
        name="flash_attn",
    )(gq.reshape(1, HEAD_DIM), gk.reshape(1, HEAD_DIM), q, kt, v, *side_weights)


def _gated_merge_kernel(h_ref, a_ref, gm_ref, ym_ref, wga_ref, wgg_ref, wgm_ref,
                        wa_ref, wg_ref, wm_ref, o_ref):
    h = h_ref[...]
    y = None
    for y_ref, wgate_ref, w_ref in ((a_ref, wga_ref, wa_ref), (gm_ref, wgg_ref, wg_ref),
                                    (ym_ref, wgm_ref, wm_ref)):
        gate = 1.0 / (1.0 + jnp.exp(-_dot(h, wgate_ref[...])))
        term = gate * _dot(y_ref[...], w_ref[...])
        y = term if y is None else y + term
    o_ref[...] = y.astype(o_ref.dtype)


def _gated_merge(h, attn, gm, ym, w_gate, w_a, w_g, w_m, tm, tn):
    m, d = h.shape
    n = w_a.shape[1]
    nb = n // tn
    row = lambda j, i: (i, 0)
    col = lambda j, i: (0, j)
    return pl.pallas_call(
        _gated_merge_kernel,
        out_shape=jax.ShapeDtypeStruct((m, n), BF16),
        grid=(nb, m // tm),
        in_specs=[pl.BlockSpec((tm, d), row),
                  pl.BlockSpec((tm, attn.shape[1]), row),
                  pl.BlockSpec((tm, gm.shape[1]), row),
                  pl.BlockSpec((tm, ym.shape[1]), row),
                  pl.BlockSpec((d, tn), lambda j, i: (0, j)),
                  pl.BlockSpec((d, tn), lambda j, i: (0, nb + j)),
                  pl.BlockSpec((d, tn), lambda j, i: (0, 2 * nb + j)),
                  pl.BlockSpec((w_a.shape[0], tn), col),
                  pl.BlockSpec((w_g.shape[0], tn), col),
                  pl.BlockSpec((w_m.shape[0], tn), col)],
        out_specs=pl.BlockSpec((tm, tn), lambda j, i: (i, j)),
        compiler_params=_params("arbitrary", "arbitrary"),
        name="gated_merge",
    )(h, attn, gm, ym, w_gate, w_gate, w_gate, w_a, w_g, w_m)


def _outproj_kernel(mg_ref, w_ref, x_ref, g_ref, x1_ref, h2_ref):
    x1 = x_ref[...] + _dot(mg_ref[...], w_ref[...])
    x1_ref[...] = x1
    h2_ref[...] = _head_rms(x1, g_ref[...]).astype(h2_ref.dtype)


def _outproj(merged, w, x, gain, tm):
    m, d = x.shape
    row = lambda i: (i, 0)
    fixed = lambda i: (0, 0)
    return pl.pallas_call(
        _outproj_kernel,
        out_shape=(jax.ShapeDtypeStruct((m, d), F32), jax.ShapeDtypeStruct((m, d), BF16)),
        grid=(m // tm,),
        in_specs=[pl.BlockSpec((tm, d), row), pl.BlockSpec((d, d), fixed),
                  pl.BlockSpec((tm, d), row), pl.BlockSpec((1, d), fixed)],
        out_specs=(pl.BlockSpec((tm, d), row), pl.BlockSpec((tm, d), row)),
        compiler_params=_params("parallel"),
        name="out_proj",
    )(merged, w, x, gain.reshape(1, d))


def _ffn_up_kernel(h_ref, w_ref, o_ref):
    z = jnp.maximum(_dot(h_ref[...], w_ref[...]), 0.0)
    o_ref[...] = (z * z).astype(o_ref.dtype)


def _ffn_up(h, w, tm, tn):
    m, d = h.shape
    n = w.shape[1]
    return pl.pallas_call(
        _ffn_up_kernel,
        out_shape=jax.ShapeDtypeStruct((m, n), BF16),
        grid=(m // tm, n // tn),
        in_specs=[pl.BlockSpec((tm, d), lambda i, j: (i, 0)),
                  pl.BlockSpec((d, tn), lambda i, j: (0, j))],
        out_specs=pl.BlockSpec((tm, tn), lambda i, j: (i, j)),
        compiler_params=_params("parallel", "parallel"),
        name="ffn_up",
    )(h, w)


def _ffn_down_kernel(a_ref, w_ref, x_ref, o_ref):
    k = pl.program_id(2)

    @pl.when(k == 0)
    def _():
        o_ref[...] = x_ref[...] + _dot(a_ref[...], w_ref[...])

    @pl.when(k != 0)
    def _():
        o_ref[...] += _dot(a_ref[...], w_ref[...])


def _ffn_down(a, w, x1, tm, tn, tk):
    m, kdim = a.shape
    n = w.shape[1]
    return pl.pallas_call(
        _ffn_down_kernel,
        out_shape=jax.ShapeDtypeStruct((m, n), F32),
        grid=(m // tm, n // tn, kdim // tk),
        in_specs=[pl.BlockSpec((tm, tk), lambda i, j, k: (i, k)),
                  pl.BlockSpec((tk, tn), lambda i, j, k: (k, j)),
                  pl.BlockSpec((tm, tn), lambda i, j, k: (i, j))],
        out_specs=pl.BlockSpec((tm, tn), lambda i, j, k: (i, j)),
        compiler_params=_params("parallel", "parallel", "arbitrary"),
        name="ffn_down",
    )(a, w, x1)


def _rope_tables(s):
    inv = ROPE_THETA ** (-jnp.arange(ROPE_PAIRS, dtype=F32) / ROPE_PAIRS)
    inv_lane = jnp.tile(inv, HEAD_DIM // ROPE_PAIRS)
    lane_grp = jnp.arange(HEAD_DIM) // ROPE_PAIRS
    is_row = (lane_grp % 2) == 0
    sign = jnp.where(lane_grp < 2, -1.0, 1.0).astype(F32)

    def tables(n):
        ang = jnp.arange(n, dtype=F32)[:, None] * inv_lane
        return jnp.cos(ang), jnp.sin(ang) * sign

    cr, sr = tables(s // GRID_W)
    cc, sc = tables(GRID_W)
    return (jnp.where(is_row, cr, 0.0), jnp.where(is_row, sr, 0.0),
            jnp.where(is_row, 0.0, cc), jnp.where(is_row, 0.0, sc))


def kernel(x, mem, norm_mix, w_in, q_norm, k_norm, sgu_norm, w_spatial, b_spatial, mem_norm,
           w_mem_kv, mq_norm, mk_norm, w_attn_o, w_gmlp_o, w_mem_o, w_out, norm_ffn,
           w_ffn_up, w_ffn_down):
    b, s, d = x.shape
    assert b == 1 and d == D_MODEL and norm_mix.shape[0] == 1
    xs = x[0]
    w_in0 = w_in[0]
    tables = _rope_tables(s)
    bias_full = jnp.repeat(b_spatial[0].T, HEAD_DIM, axis=1)

    h, kt, v, q = _qkvproj(xs, norm_mix[0], w_in0, k_norm[0], q_norm[0], tables, TM_PROJ,
                           ATTN_TQ)
    gm, ym, w_gmlp_o_b, w_mem_o_b = _branches(
        h, w_in0, sgu_norm[0], w_spatial[0], bias_full, mq_norm[0], mem[0], mem_norm[0],
        w_mem_kv[0], mk_norm[0], w_gmlp_o[0], w_mem_o[0], TM_PROJ)

    n_gate = w_in0.shape[1] - IN_GATE_BLK * IN_COL_BLOCK
    attn, w_gate_b, w_attn_o_b, w_out_b, w_up_b, w_down_b = _attention(
        q_norm[0], k_norm[0], q, kt, v,
        (w_in0, w_attn_o[0], w_out[0], w_ffn_up[0], w_ffn_down[0]),
        (n_gate, None, None, None, None))

    merged = _gated_merge(h, attn, gm, ym, w_gate_b, w_attn_o_b, w_gmlp_o_b, w_mem_o_b,
                          TM_MERGE, TN_MERGE)
    x1, h2 = _outproj(merged, w_out_b, xs, norm_ffn[0], TM_OUT)
    a = _ffn_up(h2, w_up_b, TM_UP, TN_UP)
    out = _ffn_down(a, w_down_b, x1, TM_DOWN, TN_DOWN, TK_DOWN)
    return out[None]
```

```python
import functools
import math

import jax
import jax.numpy as jnp
from jax import lax
from jax.experimental import pallas as pl
from jax.experimental.pallas import tpu as pltpu

D_MODEL = 2048
HEAD_DIM = 128
N_Q_HEADS = 8
N_KV_HEADS = 2
Q_PER_KV = N_Q_HEADS // N_KV_HEADS
GRID_W = 64
ROPE_THETA = 10000.0
ROPE_PAIRS = HEAD_DIM // 4
GMLP_GROUPS = 4
GMLP_WIDTH = GMLP_GROUPS * HEAD_DIM
CHUNK = 128
N_MEM_HEADS = 4
MEM_WIDTH = N_MEM_HEADS * HEAD_DIM
EPS = 1e-6
ATTN_Q_W = N_Q_HEADS * HEAD_DIM
ATTN_KV_W = N_KV_HEADS * HEAD_DIM
IN_COL_BLOCK = 512
IN_Q_BLK, IN_KV_BLK, IN_U_BLK, IN_V_BLK, IN_QM_BLK, IN_GATE_BLK = 0, 2, 3, 4, 5, 6

VMEM_LIMIT_BYTES = 56 * 1024 * 1024
LOG2E = math.log2(math.e)
Q_SCALE = HEAD_DIM ** -0.5 * LOG2E
KV_BLOCK = 256
ATTN_TQ = 512
MAX_UNSHIFTED_LOGIT = 64.0

TM_PROJ = 512
TM_MERGE, TN_MERGE = 1024, 512
TM_OUT = 512
TM_UP, TN_UP = 2048, 1024
TM_DOWN, TN_DOWN, TK_DOWN = 1024, 1024, 4096

F32 = jnp.float32
BF16 = jnp.bfloat16


def _params(*sem):
    return pltpu.CompilerParams(dimension_semantics=sem, vmem_limit_bytes=VMEM_LIMIT_BYTES)


def _dot(a, b):
    return jnp.dot(a, b, preferred_element_type=F32)


def _head_rms(x, gain):
    ms = jnp.mean(x * x, axis=-1, keepdims=True)
    return x * lax.rsqrt(ms + EPS) * gain


def _rope(y, cos, sin_signed):
    return y * cos + pltpu.roll(y, HEAD_DIM // 2, 1) * sin_signed


def _to_rope_order(w):
    grp = lax.broadcasted_iota(jnp.int32, w.shape, 1) // ROPE_PAIRS
    from_right = pltpu.roll(w, HEAD_DIM - ROPE_PAIRS, 1)
    from_left = pltpu.roll(w, ROPE_PAIRS, 1)
    return jnp.where(grp == 1, from_right, jnp.where(grp == 2, from_left, w))


def _gain_in_rope_order(g_ref):
    return _to_rope_order(jnp.broadcast_to(g_ref[...], (8, HEAD_DIM)))[0:1]


def _fill_rope_tables(crow_ref, srow_ref, ccol_ref, scol_ref):
    def fill(cos_ref, sin_ref, want_row):
        n = cos_ref.shape[0]
        lane = lax.broadcasted_iota(jnp.int32, (n, HEAD_DIM), 1)
        grp = lane // ROPE_PAIRS
        freq = (lane % ROPE_PAIRS).astype(F32)
        inv = jnp.exp(freq * (-math.log(ROPE_THETA) / ROPE_PAIRS))
        pos = lax.broadcasted_iota(jnp.int32, (n, HEAD_DIM), 0).astype(F32)
        ang = pos * inv
        keep = ((grp % 2) == 0) == want_row
        sign = jnp.where(grp < 2, -1.0, 1.0)
        cos_ref[...] = jnp.where(keep, jnp.cos(ang), 0.0)
        sin_ref[...] = jnp.where(keep, jnp.sin(ang) * sign, 0.0)

    fill(crow_ref, srow_ref, True)
    fill(ccol_ref, scol_ref, False)


def _rope_tile(row_ref, col_ref, i, tm):
    nrow = tm // GRID_W
    rows = row_ref[pl.ds(pl.multiple_of(i * nrow, nrow), nrow), :]
    col = col_ref[...]
    return jnp.concatenate([rows[r:r + 1, :] + col for r in range(nrow)], axis=0)


def _lagged_step(i, n_tiles, project, finish, buf_a, buf_b):
    bufs = (buf_a, buf_b)

    @pl.when(i == 0)
    def _():
        project(buf_a)

    for parity in range(2):
        @pl.when((i > 0) & (i < n_tiles) & (i % 2 == parity))
        def _():
            project(bufs[parity])
            finish(bufs[1 - parity])

    @pl.when(i == n_tiles)
    def _():
        finish(bufs[(n_tiles - 1) % 2])


def _lag_index_maps(n_tiles):
    return (lambda i: jnp.minimum(i, n_tiles - 1)), (lambda i: jnp.maximum(i - 1, 0))


def _cast_side_job(pairs):
    for src, dst in pairs:
        skip = src.shape[1] - dst.shape[1]
        dst[...] = src[:, skip:].astype(dst.dtype)


def _side_specs(arrays, nsteps, index, out_cols=None):
    in_specs, out_specs, out_shapes = [], [], []
    for k, a in enumerate(arrays):
        rows, cols = a.shape
        keep = cols if out_cols is None or out_cols[k] is None else out_cols[k]
        in_specs.append(pl.BlockSpec((rows // nsteps, cols), index))
        out_specs.append(pl.BlockSpec((rows // nsteps, keep), index))
        out_shapes.append(jax.ShapeDtypeStruct((rows, keep), BF16))
    return in_specs, out_specs, out_shapes


def _qkvproj_kernel(x_ref, gx_ref, wkv_ref, wq_ref, gk_ref, gq_ref,
                    h_ref, kt_ref, v_ref, q_ref, wkvb_ref, wqb_ref,
                    kv_a, kv_b, q_a, q_b, crow_ref, srow_ref, ccol_ref, scol_ref,
                    *, tm, tq, n_tiles):
    i = pl.program_id(0)

    @pl.when(i == 0)
    def _():
        for hd in range(N_KV_HEADS):
            sl = slice(hd * HEAD_DIM, (hd + 1) * HEAD_DIM)
            wkvb_ref[:, sl] = _to_rope_order(wkv_ref[:, sl]).astype(BF16)
        wkvb_ref[:, ATTN_KV_W:] = wkv_ref[:, ATTN_KV_W:].astype(BF16)
        for hd in range(N_Q_HEADS):
            sl = slice(hd * HEAD_DIM, (hd + 1) * HEAD_DIM)
            wqb_ref[:, sl] = _to_rope_order(wq_ref[:, sl]).astype(BF16)
        _fill_rope_tables(crow_ref, srow_ref, ccol_ref, scol_ref)

    def project(bufs):
        kv_acc, q_acc = bufs
        h = _head_rms(x_ref[...], gx_ref[...]).astype(BF16)
        h_ref[...] = h
        kv_acc[...] = _dot(h, wkvb_ref[...])
        q_acc[...] = _dot(h, wqb_ref[...])

    def finish(bufs):
        kv_acc, q_acc = bufs
        prev = jnp.maximum(i - 1, 0)
        cos = _rope_tile(crow_ref, ccol_ref, prev, tm)
        sin = _rope_tile(srow_ref, scol_ref, prev, tm)
        gk = _gain_in_rope_order(gk_ref)
        for hd in range(N_KV_HEADS):
            sl = slice(hd * HEAD_DIM, (hd + 1) * HEAD_DIM)
            y = _rope(_head_rms(kv_acc[:, sl], gk), cos, sin)
            for c in range(tm // KV_BLOCK):
                blk = y[c * KV_BLOCK:(c + 1) * KV_BLOCK, :]
                kt_ref[c, sl, :] = jnp.transpose(blk).astype(kt_ref.dtype)
        v_ref[...] = kv_acc[:, ATTN_KV_W:].astype(v_ref.dtype)
        gq = _gain_in_rope_order(gq_ref) * Q_SCALE
        for hd in range(N_Q_HEADS):
            grp, r = divmod(hd, Q_PER_KV)
            x = q_acc[:, hd * HEAD_DIM:(hd + 1) * HEAD_DIM]
            y = _rope(_head_rms(x, gq), cos, sin).astype(q_ref.dtype)
            for t in range(tm // tq):
                q_ref[t, grp, r * tq:(r + 1) * tq, :] = y[t * tq:(t + 1) * tq]

    _lagged_step(i, n_tiles, project, finish, (kv_a, q_a), (kv_b, q_b))


def _qkvproj(x, gx, w_in, gk, gq, tm, tq):
    m, d = x.shape
    n_tiles = m // tm
    cur, prev = _lag_index_maps(n_tiles)
    fixed = lambda i: (0, 0)
    nkv = 2 * ATTN_KV_W
    qshape = (m // tq, N_KV_HEADS, Q_PER_KV * tq, HEAD_DIM)
    once = pl.Buffered(1)
    return pl.pallas_call(
        functools.partial(_qkvproj_kernel, tm=tm, tq=tq, n_tiles=n_tiles),
        out_shape=(jax.ShapeDtypeStruct((m, d), BF16),
                   jax.ShapeDtypeStruct((m // KV_BLOCK, ATTN_KV_W, KV_BLOCK), BF16),
                   jax.ShapeDtypeStruct((m, ATTN_KV_W), BF16),
                   jax.ShapeDtypeStruct(qshape, BF16)),
        grid=(n_tiles + 1,),
        in_specs=[pl.BlockSpec((tm, d), lambda i: (cur(i), 0)), pl.BlockSpec((1, d), fixed),
                  pl.BlockSpec((d, nkv), lambda i: (0, IN_KV_BLK), pipeline_mode=once),
                  pl.BlockSpec((d, ATTN_Q_W), lambda i: (0, IN_Q_BLK), pipeline_mode=once),
                  pl.BlockSpec((1, HEAD_DIM), fixed),
                  pl.BlockSpec((1, HEAD_DIM), fixed)],
        out_specs=(pl.BlockSpec((tm, d), lambda i: (cur(i), 0)),
                   pl.BlockSpec((tm // KV_BLOCK, ATTN_KV_W, KV_BLOCK),
                                lambda i: (prev(i), 0, 0)),
                   pl.BlockSpec((tm, ATTN_KV_W), lambda i: (prev(i), 0)),
                   pl.BlockSpec((tm // tq,) + qshape[1:], lambda i: (prev(i), 0, 0, 0))),
        scratch_shapes=[pltpu.VMEM((d, nkv), BF16), pltpu.VMEM((d, ATTN_Q_W), BF16),
                        pltpu.VMEM((tm, nkv), F32), pltpu.VMEM((tm, nkv), F32),
                        pltpu.VMEM((tm, ATTN_Q_W), F32), pltpu.VMEM((tm, ATTN_Q_W), F32),
                        pltpu.VMEM((m // GRID_W, HEAD_DIM), F32), pltpu.VMEM((m // GRID_W, HEAD_DIM), F32),
                        pltpu.VMEM((GRID_W, HEAD_DIM), F32), pltpu.VMEM((GRID_W, HEAD_DIM), F32)],
        compiler_params=_params("arbitrary"),
        name="qkv_proj",
    )(x, gx.reshape(1, d), w_in, w_in, gk.reshape(1, HEAD_DIM), gq.reshape(1, HEAD_DIM))


def _memory_kv(mem_ref, gmem_ref, wkv_ref, gk_ref, kt_s, v_s):
    mem_n = _head_rms(mem_ref[...], gmem_ref[...]).astype(BF16)
    kv = _dot(mem_n, wkv_ref[...].astype(BF16))
    gk = gk_ref[...]
    for hd in range(N_MEM_HEADS):
        sl = slice(hd * HEAD_DIM, (hd + 1) * HEAD_DIM)
        kt_s[sl, :] = jnp.transpose(_head_rms(kv[:, sl], gk)).astype(kt_s.dtype)
    v_s[...] = kv[:, MEM_WIDTH:].astype(v_s.dtype)


def _branches_kernel(h_ref, wu_ref, wv_ref, wqm_ref, gs_ref, ws_ref, b_ref, gq_ref,
                     mem_ref, gmem_ref, wkv_ref, gk_ref, wgo_ref, wmo_ref,
                     gm_ref, ym_ref, wgob_ref, wmob_ref, wzb_ref, wqb_ref, kt_s, v_s, bias_s, *, tm):
    @pl.when(pl.program_id(0) == 0)
    def _():
        wzb_ref[:, :GMLP_WIDTH] = wu_ref[...].astype(BF16)
        wzb_ref[:, GMLP_WIDTH:] = wv_ref[...].astype(BF16)
        wqb_ref[...] = wqm_ref[...].astype(BF16)
        _memory_kv(mem_ref, gmem_ref, wkv_ref, gk_ref, kt_s, v_s)
        b_t = jnp.transpose(jnp.concatenate(
            [b_ref[...], jnp.zeros((8 - GMLP_GROUPS, CHUNK), F32)], axis=0))
        for grp in range(GMLP_GROUPS):
            bias_s[:, grp * HEAD_DIM:(grp + 1) * HEAD_DIM] = jnp.broadcast_to(
                b_t[:, grp:grp + 1], (CHUNK, HEAD_DIM))

    _cast_side_job([(wgo_ref, wgob_ref), (wmo_ref, wmob_ref)])
    h = h_ref[...]
    z = jax.nn.gelu(_dot(h, wzb_ref[...]))
    acc = _dot(h, wqb_ref[...])

    u = z[:, :GMLP_WIDTH]
    vn = _head_rms(z[:, GMLP_WIDTH:], gs_ref[...]).astype(BF16)
    bias = bias_s[...]
    for c in range(tm // CHUNK):
        rows = slice(c * CHUNK, (c + 1) * CHUNK)
        for grp in range(GMLP_GROUPS):
            cols = slice(grp * HEAD_DIM, (grp + 1) * HEAD_DIM)
            mixed = _dot(ws_ref[grp].astype(BF16), vn[rows, cols]) + bias[:, cols]
            gm_ref[rows, cols] = (u[rows, cols] * mixed).astype(gm_ref.dtype)

    g = gq_ref[...] * Q_SCALE
    for hd in range(N_MEM_HEADS):
        sl = slice(hd * HEAD_DIM, (hd + 1) * HEAD_DIM)
        qn = _head_rms(acc[:, sl], g).astype(BF16)
        s = _dot(qn, kt_s[sl, :])
        p = jnp.exp2(s - jnp.max(s, axis=-1, keepdims=True))
        l = jnp.sum(p, axis=-1, keepdims=True)
        o = _dot(p.astype(BF16), v_s[:, sl])
        ym_ref[:, sl] = (o / l).astype(ym_ref.dtype)


def _branches(h, w_in, sgu_gain, w_spatial, bias_full, mq_gain, mem, mem_gain, w_mem_kv, mk_gain,
              w_go, w_mo, tm):
    m, d = h.shape
    mt = mem.shape[0]
    row = lambda i: (i, 0)
    fixed = lambda i: (0, 0)
    once = pl.Buffered(1)
    side_in, side_out, side_shapes = _side_specs([w_go, w_mo], m // tm, row)
    return pl.pallas_call(
        functools.partial(_branches_kernel, tm=tm),
        out_shape=[jax.ShapeDtypeStruct((m, GMLP_WIDTH), BF16),
                   jax.ShapeDtypeStruct((m, MEM_WIDTH), BF16)] + side_shapes,
        grid=(m // tm,),
        in_specs=[pl.BlockSpec((tm, d), row),
                  pl.BlockSpec((d, GMLP_WIDTH), lambda i: (0, IN_U_BLK), pipeline_mode=once),
                  pl.BlockSpec((d, GMLP_WIDTH), lambda i: (0, IN_V_BLK), pipeline_mode=once),
                  pl.BlockSpec((d, MEM_WIDTH), lambda i: (0, IN_QM_BLK), pipeline_mode=once),
                  pl.BlockSpec((1, GMLP_WIDTH), fixed),
                  pl.BlockSpec((GMLP_GROUPS, CHUNK, CHUNK), lambda i: (0, 0, 0)),
                  pl.BlockSpec((GMLP_GROUPS, CHUNK), fixed),
                  pl.BlockSpec((1, HEAD_DIM), fixed),
                  pl.BlockSpec((mt, d), fixed, pipeline_mode=once),
                  pl.BlockSpec((1, d), fixed),
                  pl.BlockSpec((d, 2 * MEM_WIDTH), fixed, pipeline_mode=once),
                  pl.BlockSpec((1, HEAD_DIM), fixed)] + side_in,
        out_specs=[pl.BlockSpec((tm, GMLP_WIDTH), row),
                   pl.BlockSpec((tm, MEM_WIDTH), row)] + side_out,
        scratch_shapes=[pltpu.VMEM((d, 2 * GMLP_WIDTH), BF16), pltpu.VMEM((d, MEM_WIDTH), BF16),
                        pltpu.VMEM((MEM_WIDTH, mt), BF16), pltpu.VMEM((mt, MEM_WIDTH), BF16),
                        pltpu.VMEM((CHUNK, GMLP_WIDTH), F32)],
        compiler_params=_params("arbitrary"),
        name="branches",
    )(h, w_in, w_in, w_in, sgu_gain.reshape(1, GMLP_WIDTH), w_spatial, bias_full,
      mq_gain.reshape(1, HEAD_DIM), mem, mem_gain.reshape(1, d), w_mem_kv,
      mk_gain.reshape(1, HEAD_DIM), w_go, w_mo)


def _store_heads(o_ref, out, tq):
    for r in range(Q_PER_KV):
        o_ref[:, r * HEAD_DIM:(r + 1) * HEAD_DIM] = out[r * tq:(r + 1) * tq].astype(o_ref.dtype)


def _attn_unshifted(q_ref, kt_ref, v_ref, o_ref, l_ref, acc_ref, tq, nk):
    l_ref[...] = jnp.zeros_like(l_ref)
    acc_ref[...] = jnp.zeros_like(acc_ref)

    def body(j, carry):
        off = pl.multiple_of(j * KV_BLOCK, KV_BLOCK)
        p = jnp.exp2(_dot(q_ref[0, 0], kt_ref[j]))
        l_ref[...] += p[:, :HEAD_DIM] + p[:, HEAD_DIM:]
        acc_ref[...] += _dot(p.astype(BF16), v_ref[pl.ds(off, KV_BLOCK), :])
        return carry

    lax.fori_loop(0, nk, body, 0, unroll=True)
    l = jnp.sum(l_ref[...], axis=-1, keepdims=True)
    _store_heads(o_ref, acc_ref[...] / l, tq)


def _attn_online(q_ref, kt_ref, v_ref, o_ref, tq, nk):
    q = q_ref[0, 0]
    rows = Q_PER_KV * tq

    def body(j, carry):
        m, l, acc = carry
        off = pl.multiple_of(j * KV_BLOCK, KV_BLOCK)
        s = _dot(q, kt_ref[j])
        m_new = jnp.maximum(m, jnp.max(s, axis=-1, keepdims=True))
        alpha = jnp.exp2(m - m_new)
        p = jnp.exp2(s - m_new)
        l = alpha * l + jnp.sum(p, axis=-1, keepdims=True)
        acc = alpha * acc + _dot(p.astype(BF16), v_ref[pl.ds(off, KV_BLOCK), :])
        return m_new, l, acc

    m0 = jnp.full((rows, 1), -jnp.inf, F32)
    l0 = jnp.zeros((rows, 1), F32)
    acc0 = jnp.zeros((rows, HEAD_DIM), F32)
    _, l, acc = lax.fori_loop(0, nk, body, (m0, l0, acc0))
    _store_heads(o_ref, acc / l, tq)


def _attn_kernel(gq_ref, gk_ref, q_ref, kt_ref, v_ref, *rest, tq, nk, nside):
    side = rest[:nside]
    o_ref = rest[nside]
    side_out = rest[nside + 1:2 * nside + 1]
    l_ref, acc_ref = rest[2 * nside + 1:]
    _cast_side_job(zip(side, side_out))
    logit_bound = (1.02 * LOG2E * math.sqrt(HEAD_DIM)
                   * jnp.max(jnp.abs(gq_ref[...])) * jnp.max(jnp.abs(gk_ref[...])))
    unshifted = logit_bound <= MAX_UNSHIFTED_LOGIT

    @pl.when(unshifted)
    def _():
        _attn_unshifted(q_ref, kt_ref, v_ref, o_ref, l_ref, acc_ref, tq, nk)

    @pl.when(jnp.logical_not(unshifted))
    def _():
        _attn_online(q_ref, kt_ref, v_ref, o_ref, tq, nk)


def _attention(gq, gk, q, kt, v, side_weights, side_cols):
    nq, _, rows, _ = q.shape
    tq = rows // Q_PER_KV
    s = v.shape[0]
    nk = s // KV_BLOCK
    gw = Q_PER_KV * HEAD_DIM
    nside = len(side_weights)
    side_in, side_out, side_shapes = _side_specs(side_weights, N_KV_HEADS * nq,
                                                 lambda g, i: (g * nq + i, 0), side_cols)
    return pl.pallas_call(
        functools.partial(_attn_kernel, tq=tq, nk=nk, nside=nside),
        out_shape=[jax.ShapeDtypeStruct((s, ATTN_Q_W), BF16)] + side_shapes,
        grid=(N_KV_HEADS, nq),
        in_specs=[pl.BlockSpec((1, HEAD_DIM), lambda g, i: (0, 0)),
                  pl.BlockSpec((1, HEAD_DIM), lambda g, i: (0, 0)),
                  pl.BlockSpec((1, 1, rows, HEAD_DIM), lambda g, i: (i, g, 0, 0)),
                  pl.BlockSpec((nk, HEAD_DIM, KV_BLOCK), lambda g, i: (0, g, 0)),
                  pl.BlockSpec((s, HEAD_DIM), lambda g, i: (0, g))] + side_in,
        out_specs=[pl.BlockSpec((tq, gw), lambda g, i: (i, g))] + side_out,
        scratch_shapes=[pltpu.VMEM((rows, HEAD_DIM), F32), pltpu.VMEM((rows, HEAD_DIM), F32)],
        compiler_params=_params("arbitrary", "arbitrary"),
        name="flash_attn",
    )(gq.reshape(1, HEAD_DIM), gk.reshape(1, HEAD_DIM), q, kt, v, *side_weights)


def _gated_merge_kernel(h_ref, a_ref, gm_ref, ym_ref, wga_ref, wgg_ref, wgm_ref,
                        wa_ref, wg_ref, wm_ref, o_ref):
    h = h_ref[...]
    y = None
    for y_ref, wgate_ref, w_ref in ((a_ref, wga_ref, wa_ref), (gm_ref, wgg_ref, wg_ref),
                                    (ym_ref, wgm_ref, wm_ref)):
        gate = 1.0 / (1.0 + jnp.exp(-_dot(h, wgate_ref[...])))
        term = gate * _dot(y_ref[...], w_ref[...])
        y = term if y is None else y + term
    o_ref[...] = y.astype(o_ref.dtype)


def _gated_merge(h, attn, gm, ym, w_gate, w_a, w_g, w_m, tm, tn):
    m, d = h.shape
    n = w_a.shape[1]
    nb = n // tn
    row = lambda j, i: (i, 0)
    col = lambda j, i: (0, j)
    return pl.pallas_call(
        _gated_merge_kernel,
        out_shape=jax.ShapeDtypeStruct((m, n), BF16),
        grid=(nb, m // tm),
        in_specs=[pl.BlockSpec((tm, d), row),
                  pl.BlockSpec((tm, attn.shape[1]), row),
                  pl.BlockSpec((tm, gm.shape[1]), row),
                  pl.BlockSpec((tm, ym.shape[1]), row),
                  pl.BlockSpec((d, tn), lambda j, i: (0, j)),
                  pl.BlockSpec((d, tn), lambda j, i: (0, nb + j)),
                  pl.BlockSpec((d, tn), lambda j, i: (0, 2 * nb + j)),
                  pl.BlockSpec((w_a.shape[0], tn), col),
                  pl.BlockSpec((w_g.shape[0], tn), col),
                  pl.BlockSpec((w_m.shape[0], tn), col)],
        out_specs=pl.BlockSpec((tm, tn), lambda j, i: (i, j)),
        compiler_params=_params("arbitrary", "arbitrary"),
        name="gated_merge",
    )(h, attn, gm, ym, w_gate, w_gate, w_gate, w_a, w_g, w_m)


def _outproj_kernel(mg_ref, w_ref, x_ref, g_ref, x1_ref, h2_ref):
    x1 = x_ref[...] + _dot(mg_ref[...], w_ref[...])
    x1_ref[...] = x1
    h2_ref[...] = _head_rms(x1, g_ref[...]).astype(h2_ref.dtype)


def _outproj(merged, w, x, gain, tm):
    m, d = x.shape
    row = lambda i: (i, 0)
    fixed = lambda i: (0, 0)
    return pl.pallas_call(
        _outproj_kernel,
        out_shape=(jax.ShapeDtypeStruct((m, d), F32), jax.ShapeDtypeStruct((m, d), BF16)),
        grid=(m // tm,),
        in_specs=[pl.BlockSpec((tm, d), row), pl.BlockSpec((d, d), fixed),
                  pl.BlockSpec((tm, d), row), pl.BlockSpec((1, d), fixed)],
        out_specs=(pl.BlockSpec((tm, d), row), pl.BlockSpec((tm, d), row)),
        compiler_params=_params("parallel"),
        name="out_proj",
    )(merged, w, x, gain.reshape(1, d))


def _ffn_up_kernel(h_ref, w_ref, o_ref):
    z = jnp.maximum(_dot(h_ref[...], w_ref[...]), 0.0)
    o_ref[...] = (z * z).astype(o_ref.dtype)


def _ffn_up(h, w, tm, tn):
    m, d = h.shape
    n = w.shape[1]
    return pl.pallas_call(
        _ffn_up_kernel,
        out_shape=jax.ShapeDtypeStruct((m, n), BF16),
        grid=(m // tm, n // tn),
        in_specs=[pl.BlockSpec((tm, d), lambda i, j: (i, 0)),
                  pl.BlockSpec((d, tn), lambda i, j: (0, j))],
        out_specs=pl.BlockSpec((tm, tn), lambda i, j: (i, j)),
        compiler_params=_params("parallel", "parallel"),
        name="ffn_up",
    )(h, w)


def _ffn_down_kernel(a_ref, w_ref, x_ref, o_ref):
    k = pl.program_id(2)

    @pl.when(k == 0)
    def _():
        o_ref[...] = x_ref[...] + _dot(a_ref[...], w_ref[...])

    @pl.when(k != 0)
    def _():
        o_ref[...] += _dot(a_ref[...], w_ref[...])


def _ffn_down(a, w, x1, tm, tn, tk):
    m, kdim = a.shape
    n = w.shape[1]
    return pl.pallas_call(
        _ffn_down_kernel,
        out_shape=jax.ShapeDtypeStruct((m, n), F32),
        grid=(m // tm, n // tn, kdim // tk),
        in_specs=[pl.BlockSpec((tm, tk), lambda i, j, k: (i, k)),
                  pl.BlockSpec((tk, tn), lambda i, j, k: (k, j)),
                  pl.BlockSpec((tm, tn), lambda i, j, k: (i, j))],
        out_specs=pl.BlockSpec((tm, tn), lambda i, j, k: (i, j)),
        compiler_params=_params("parallel", "parallel", "arbitrary"),
        name="ffn_down",
    )(a, w, x1)


def kernel(x, mem, norm_mix, w_in, q_norm, k_norm, sgu_norm, w_spatial, b_spatial, mem_norm,
           w_mem_kv, mq_norm, mk_norm, w_attn_o, w_gmlp_o, w_mem_o, w_out, norm_ffn,
           w_ffn_up, w_ffn_down):
    b, s, d = x.shape
    assert b == 1 and d == D_MODEL and norm_mix.shape[0] == 1
    xs = x[0]
    w_in0 = w_in[0]

    h, kt, v, q = _qkvproj(xs, norm_mix[0], w_in0, k_norm[0], q_norm[0], TM_PROJ,
                           ATTN_TQ)
    gm, ym, w_gmlp_o_b, w_mem_o_b = _branches(
        h, w_in0, sgu_norm[0], w_spatial[0], b_spatial[0], mq_norm[0], mem[0], mem_norm[0],
        w_mem_kv[0], mk_norm[0], w_gmlp_o[0], w_mem_o[0], TM_PROJ)

    n_gate = w_in0.shape[1] - IN_GATE_BLK * IN_COL_BLOCK
    attn, w_gate_b, w_attn_o_b, w_out_b, w_up_b, w_down_b = _attention(
        q_norm[0], k_norm[0], q, kt, v,
        (w_in0, w_attn_o[0], w_out[0], w_ffn_up[0], w_ffn_down[0]),
        (n_gate, None, None, None, None))

    merged = _gated_merge(h, attn, gm, ym, w_gate_b, w_attn_o_b, w_gmlp_o_b, w_mem_o_b,
                          TM_MERGE, TN_MERGE)
    x1, h2 = _outproj(merged, w_out_b, xs, norm_ffn[0], TM_OUT)
    a = _ffn_up(h2, w_up_b, TM_UP, TN_UP)
    out = _ffn_down(a, w_down_b, x1, TM_DOWN, TN_DOWN, TK_DOWN)
    return out[None]
```
